```python
import math
import jax, jax.numpy as jnp
from jax import lax
import numpy as np

D_MODEL = 1024
BATCH = 4
SEQ = 4096
DEPTH = 2

CHUNK = 64
Q_BLOCK = 128
HEAD_DIM = 64
D_MIX = D_MODEL
D_FF = 4 * D_MODEL
EPS = 1e-6
ROPE_THETA = 500000.0
ROPE_DIM = HEAD_DIM // 4

A_HEADS = 4
A_QK_DIM = HEAD_DIM
A_V_DIM = 2 * HEAD_DIM
A_WIDTH = A_HEADS * A_V_DIM
B_HEADS = 4
B_LEFT_CHUNKS = 8
B_BAND = B_LEFT_CHUNKS + 1
REL_CLIP = 128
B_WIDTH = B_HEADS * HEAD_DIM
C_HEADS = 4
C_WIDTH = C_HEADS * HEAD_DIM

IN_SPLIT_SIZES = [A_HEADS * 2 * A_QK_DIM, A_HEADS * 2 * A_QK_DIM, A_WIDTH,
                  B_WIDTH, B_WIDTH, B_WIDTH,
                  C_WIDTH, C_WIDTH, C_WIDTH]
D_IN = sum(IN_SPLIT_SIZES)
IN_SPLITS = [int(v) for v in np.cumsum(IN_SPLIT_SIZES)[:-1]]

kernel_name = "hybrid_diff_chunkrel_stickbreak_block"


def rms_norm(x, g):
    xf = x.astype(jnp.float32)
    y = xf * lax.rsqrt(jnp.mean(xf * xf, axis=-1, keepdims=True) + EPS)
    return (y * g.astype(jnp.float32)).astype(x.dtype)


def head_rms_norm(o, g, n_heads):
    bsz, s, w = o.shape
    d = w // n_heads
    y = rms_norm(o.reshape(bsz, s, n_heads, d), g.reshape(n_heads, d))
    return y.reshape(bsz, s, w)


def rope_tables(seq):
    pos = jnp.arange(seq, dtype=jnp.float32)
    inv_freq = ROPE_THETA ** (-jnp.arange(0, ROPE_DIM, 2, dtype=jnp.float32) / ROPE_DIM)
    ang = pos[:, None] * inv_freq[None, :]
    return jnp.cos(ang), jnp.sin(ang)


def partial_rope(x, cos, sin):
    s = x.shape[1]
    shp = (1, s) + (1,) * (x.ndim - 3) + (ROPE_DIM // 2,)
    c = cos.reshape(shp).astype(x.dtype)
    sn = sin.reshape(shp).astype(x.dtype)
    x1 = x[..., : ROPE_DIM // 2]
    x2 = x[..., ROPE_DIM // 2: ROPE_DIM]
    rot = jnp.concatenate([x1 * c - x2 * sn, x2 * c + x1 * sn], axis=-1)
    return jnp.concatenate([rot, x[..., ROPE_DIM:]], axis=-1)


def diff_attention(q, k, v, lam, lam_init, subln_g):
    bsz, s = q.shape[0], q.shape[1]
    scale = A_QK_DIM ** -0.5
    pos = jnp.arange(s)
    outs = []
    for b0 in range(0, s, Q_BLOCK):
        e = b0 + Q_BLOCK
        sc = jnp.einsum('bqhnd,bkhnd->bhnqk', q[:, b0:e], k[:, :e]).astype(jnp.float32) * scale
        mask = (pos[:e][None, :] // CHUNK) <= (pos[b0:e][:, None] // CHUNK)
        sc = jnp.where(mask[None, None, None], sc, -jnp.inf)
        p = jax.nn.softmax(sc, axis=-1)
        w = p[:, :, 0] - lam * p[:, :, 1]
        outs.append(jnp.einsum('bhqk,bkhe->bqhe', w.astype(v.dtype), v[:, :e]))
    o = jnp.concatenate(outs, axis=1)
    o = rms_norm(o, subln_g) * (1.0 - lam_init)
    return o.reshape(bsz, s, A_WIDTH)


def chunk_rel_attention(q, k, v, rel_bias):
    bsz, s, h, d = q.shape
    nc = s // CHUNK
    qc = q.reshape(bsz, nc, CHUNK, h, d)
    pad = [(0, 0), (B_LEFT_CHUNKS * CHUNK, 0), (0, 0), (0, 0)]
    kp = jnp.pad(k, pad).reshape(bsz, nc + B_LEFT_CHUNKS, CHUNK, h, d)
    vp = jnp.pad(v, pad).reshape(bsz, nc + B_LEFT_CHUNKS, CHUNK, h, d)
    kband = jnp.concatenate([kp[:, j:j + nc] for j in range(B_BAND)], axis=2)
    vband = jnp.concatenate([vp[:, j:j + nc] for j in range(B_BAND)], axis=2)
    sc = jnp.einsum('bcqhd,bckhd->bchqk', qc, kband).astype(jnp.float32) * (d ** -0.5)
    qi = jnp.arange(CHUNK)
    kj = jnp.arange(B_BAND * CHUNK)
    rel = qi[:, None] + B_LEFT_CHUNKS * CHUNK - kj[None, :]
    idx = jnp.clip(rel, -REL_CLIP, REL_CLIP) + REL_CLIP
    bias = rel_bias[:, idx].astype(jnp.float32)
    ci = jnp.arange(nc)
    valid = (ci[:, None] - B_LEFT_CHUNKS + kj[None, :] // CHUNK) >= 0
    sc = jnp.where(valid[None, :, None, None, :], sc + bias[None, None], -jnp.inf)
    p = jax.nn.softmax(sc, axis=-1)
    o = jnp.einsum('bchqk,bckhd->bcqhd', p.astype(v.dtype), vband)
    return o.reshape(bsz, s, h * d)


def stick_breaking_attention(q, k, v):
    bsz, s, h, d = q.shape
    scale = d ** -0.5
    pos = jnp.arange(s)
    outs = []
    for b0 in range(0, s, Q_BLOCK):
        e = b0 + Q_BLOCK
        z = jnp.einsum('bqhd,bkhd->bhqk', q[:, b0:e], k[:, :e]).astype(jnp.float32) * scale
        causal = (pos[:e][None, :] < pos[b0:e][:, None])[None, None]
        log_beta = jax.nn.log_sigmoid(z)
        log_stay = jnp.where(causal, jax.nn.log_sigmoid(-z), 0.0)
        rem = lax.cumsum(log_stay, axis=3, reverse=True) - log_stay
        a = jnp.where(causal, jnp.exp(log_beta + rem), 0.0)
        outs.append(jnp.einsum('bhqk,bkhd->bqhd', a.astype(v.dtype), v[:, :e]))
    return jnp.concatenate(outs, axis=1).reshape(bsz, s, h * d)


def setup_inputs(seed: int = 0) -> dict:
    key = jax.random.key(seed)
    ks = jax.random.split(key, 20)
    f32 = jnp.float32
    nrm = lambda k, shp, sc: jax.random.normal(k, shp, f32) * sc
    gain = lambda k, n: 1.0 + 0.05 * jax.random.normal(k, (DEPTH, n), f32)
    return {
        "x": jax.random.normal(ks[0], (BATCH, SEQ, D_MODEL), f32),
        "norm_pre_mix": gain(ks[1], D_MODEL),
        "w_in": nrm(ks[2], (DEPTH, D_MODEL, D_IN), D_MODEL ** -0.5),
        "lam_q1": nrm(ks[3], (DEPTH, A_QK_DIM), 0.1),
        "lam_k1": nrm(ks[4], (DEPTH, A_QK_DIM), 0.1),
        "lam_q2": nrm(ks[5], (DEPTH, A_QK_DIM), 0.1),
        "lam_k2": nrm(ks[6], (DEPTH, A_QK_DIM), 0.1),
        "subln_a": gain(ks[7], A_V_DIM),
        "rel_bias": nrm(ks[8], (DEPTH, B_HEADS, 2 * REL_CLIP + 1), 0.2),
        "gn_b": gain(ks[9], B_WIDTH),
        "gn_c": gain(ks[10], C_WIDTH),
        "w_out": nrm(ks[11], (DEPTH, D_MIX, D_MODEL), D_MIX ** -0.5),
        "norm_post_mix": gain(ks[12], D_MODEL),
        "norm_pre_mlp": gain(ks[13], D_MODEL),
        "w_up": nrm(ks[14], (DEPTH, D_MODEL, D_FF), D_MODEL ** -0.5),
        "w_down": nrm(ks[15], (DEPTH, D_FF, D_MODEL), D_FF ** -0.5),
        "norm_post_mlp": gain(ks[16], D_MODEL),
    }


def reference(x, norm_pre_mix, w_in, lam_q1, lam_k1, lam_q2, lam_k2, subln_a, rel_bias,
              gn_b, gn_c, w_out, norm_post_mix, norm_pre_mlp, w_up, w_down, norm_post_mlp):
    bsz, s, _ = x.shape
    cos, sin = rope_tables(s)
    for l in range(DEPTH):
        h = rms_norm(x, norm_pre_mix[l])
        proj = h @ w_in[l]
        a_q, a_k, a_v, b_q, b_k, b_v, c_q, c_k, c_v = jnp.split(proj, IN_SPLITS, axis=-1)

        lam_init = 0.8 - 0.6 * math.exp(-0.3 * l)
        lam = (jnp.exp(jnp.sum(lam_q1[l].astype(jnp.float32) * lam_k1[l].astype(jnp.float32)))
               - jnp.exp(jnp.sum(lam_q2[l].astype(jnp.float32) * lam_k2[l].astype(jnp.float32)))
               + lam_init)
        qa = partial_rope(a_q.reshape(bsz, s, A_HEADS, 2, A_QK_DIM), cos, sin)
        ka = partial_rope(a_k.reshape(bsz, s, A_HEADS, 2, A_QK_DIM), cos, sin)
        va = a_v.reshape(bsz, s, A_HEADS, A_V_DIM)
        o_a = diff_attention(qa, ka, va, lam, lam_init, subln_a[l])

        o_b = chunk_rel_attention(b_q.reshape(bsz, s, B_HEADS, HEAD_DIM),
                                  b_k.reshape(bsz, s, B_HEADS, HEAD_DIM),
                                  b_v.reshape(bsz, s, B_HEADS, HEAD_DIM), rel_bias[l])
        o_b = head_rms_norm(o_b, gn_b[l], B_HEADS)

        o_c = stick_breaking_attention(c_q.reshape(bsz, s, C_HEADS, HEAD_DIM),
                                       c_k.reshape(bsz, s, C_HEADS, HEAD_DIM),
                                       c_v.reshape(bsz, s, C_HEADS, HEAD_DIM))
        o_c = head_rms_norm(o_c, gn_c[l], C_HEADS)

        y = jnp.concatenate([o_a, o_b, o_c], axis=-1) @ w_out[l]
        x = x + rms_norm(y, norm_post_mix[l])

        h = rms_norm(x, norm_pre_mlp[l])
        m = jnp.square(jax.nn.relu(h @ w_up[l])) @ w_down[l]
        x = x + rms_norm(m, norm_post_mlp[l])
    return x
```

```python
import functools
import math

import jax
import jax.numpy as jnp
from jax import lax
from jax.experimental import pallas as pl
from jax.experimental.pallas import tpu as pltpu

F32 = jnp.float32
BF16 = jnp.bfloat16

EPS = 1e-6
ROPE_THETA = 500000.0
HEAD_DIM = 64
ROPE_DIM = HEAD_DIM // 4
CHUNK = 64
LANES = 128
A_HEADS = 4
B_HEADS = 4
C_HEADS = 4
B_LEFT_CHUNKS = 8
REL_CLIP = 128
NEG = -1e30

A_Q_BLK, A_K_BLK, A_V_BLK = 0, 4, 8
B_Q_BLK, B_K_BLK, B_V_BLK = 12, 14, 16
C_Q_BLK, C_K_BLK, C_V_BLK = 18, 20, 22
D_IN = 24 * LANES

Q_TILE = 256
K_TILE = 256
B_KEYS = 3 * K_TILE
B_TABLE_W = B_KEYS + 2 * K_TILE

VMEM_LIMIT = 56 * 1024 * 1024


def _params(n_axes, vmem=VMEM_LIMIT):
    return pltpu.CompilerParams(dimension_semantics=("arbitrary",) * n_axes, vmem_limit_bytes=vmem)


def _rms(x, g):
    return x * lax.rsqrt(jnp.mean(x * x, axis=-1, keepdims=True) + EPS) * g


def _dot_t(a, b):
    return lax.dot_general(a, b, (((1,), (1,)), ((), ())), preferred_element_type=F32)


def _dot(a, b):
    return jnp.dot(a, b, preferred_element_type=F32)


def _inproj_kernel(x_ref, g_ref, w_ref, rc_ref, rs1_ref, rs2_ref, o_ref, *, n_chunk):
    h = _rms(x_ref[...], g_ref[...]).astype(BF16)
    rc, rs1, rs2 = rc_ref[...], rs1_ref[...], rs2_ref[...]
    for n0 in range(0, D_IN, n_chunk):
        y = _dot(h, w_ref[:, n0:n0 + n_chunk])
        for j in range(n_chunk // LANES):
            yj = y[:, j * LANES:(j + 1) * LANES]
            if (n0 // LANES + j) < A_V_BLK:
                yj = yj * rc + pltpu.roll(yj, LANES - ROPE_DIM // 2, 1) * rs1 + pltpu.roll(yj, ROPE_DIM // 2, 1) * rs2
            o_ref[:, n0 + j * LANES:n0 + (j + 1) * LANES] = yj.astype(BF16)


def _inproj(x, g, w, rc, rs1, rs2, seq, tm=512, n_chunk=512):
    m, d = x.shape
    pos_blocks = seq // tm
    rope_spec = pl.BlockSpec((tm, LANES), lambda i: (i % pos_blocks, 0))
    return pl.pallas_call(
        functools.partial(_inproj_kernel, n_chunk=n_chunk),
        out_shape=jax.ShapeDtypeStruct((m, D_IN), BF16),
        grid=(m // tm,),
        in_specs=[
            pl.BlockSpec((tm, d), lambda i: (i, 0)),
            pl.BlockSpec((1, d), lambda i: (0, 0)),
            pl.BlockSpec((d, D_IN), lambda i: (0, 0)),
            rope_spec, rope_spec, rope_spec,
        ],
        out_specs=pl.BlockSpec((tm, D_IN), lambda i: (i, 0)),
        compiler_params=_params(1),
        name="inproj",
    )(x, g, w, rc, rs1, rs2)


def _attn_a_kernel(q_ref, k_ref, v_ref, lam_ref, g_ref, o_ref, *, lam_init):
    qi = pl.program_id(2)
    q = q_ref[...]
    lane = lax.broadcasted_iota(jnp.int32, q.shape, 1)
    zero = jnp.zeros_like(q)
    qs = (jnp.where(lane < HEAD_DIM, q, zero), jnp.where(lane >= HEAD_DIM, q, zero))

    def tile(kj, carry, masked):
        start = pl.multiple_of(kj * K_TILE, K_TILE)
        k = k_ref[pl.ds(start, K_TILE), :]
        v = v_ref[pl.ds(start, K_TILE), :]
        new = []
        for n in range(2):
            m, l, acc = carry[n]
            s = _dot_t(qs[n], k)
            if masked:
                row = lax.broadcasted_iota(jnp.int32, s.shape, 0)
                col = lax.broadcasted_iota(jnp.int32, s.shape, 1)
                s = jnp.where((col // CHUNK) <= (row // CHUNK), s, NEG)
            m_new = jnp.maximum(m, jnp.max(s, axis=-1, keepdims=True))
            alpha = jnp.exp(m - m_new)
            p = jnp.exp(s - m_new)
            l = alpha * l + jnp.sum(p, axis=-1, keepdims=True)
            acc = alpha * acc + _dot(p.astype(BF16), v)
            new.append((m_new, l, acc))
        return tuple(new)

    init_one = (jnp.full((Q_TILE, 1), NEG, F32), jnp.zeros((Q_TILE, 1), F32), jnp.zeros((Q_TILE, LANES), F32))
    carry = lax.fori_loop(0, qi, lambda kj, c: tile(kj, c, False), (init_one, init_one))
    (_, l0, acc0), (_, l1, acc1) = tile(qi, carry, True)

    lv = lam_ref[...]
    lam = (jnp.exp(jnp.sum(lv[0:1] * lv[1:2], axis=-1, keepdims=True))
           - jnp.exp(jnp.sum(lv[2:3] * lv[3:4], axis=-1, keepdims=True)) + lam_init)
    o = acc0 / l0 - lam * (acc1 / l1)
    o_ref[...] = (_rms(o, g_ref[...]) * (1.0 - lam_init)).astype(BF16)


def _attn_a(proj, lam_vecs, subln_g, lam_init, bsz, seq):
    nq = seq // Q_TILE
    kv_spec = lambda blk: pl.BlockSpec((seq, LANES), lambda b, h, i: (b, blk + h))
    return pl.pallas_call(
        functools.partial(_attn_a_kernel, lam_init=lam_init),
        out_shape=jax.ShapeDtypeStruct((bsz * seq, A_HEADS * LANES), BF16),
        grid=(bsz, A_HEADS, nq),
        in_specs=[
            pl.BlockSpec((Q_TILE, LANES), lambda b, h, i: (b * nq + i, A_Q_BLK + h)),
            kv_spec(A_K_BLK), kv_spec(A_V_BLK),
            pl.BlockSpec((4, HEAD_DIM), lambda b, h, i: (0, 0)),
            pl.BlockSpec((1, LANES), lambda b, h, i: (0, 0)),
        ],
        out_specs=pl.BlockSpec((Q_TILE, LANES), lambda b, h, i: (b * nq + i, h)),
        compiler_params=_params(3),
        name="attn_a",
    )(proj, proj, proj, lam_vecs, subln_g)


def _bias_kernel(rb_ref, o_ref):
    h = pl.program_id(0)
    far = rb_ref[h, 2 * REL_CLIP]
    shape = (Q_TILE, B_TABLE_W)
    row = lax.broadcasted_iota(jnp.int32, shape, 0)
    col = lax.broadcasted_iota(jnp.int32, shape, 1)
    kc = col // CHUNK - row // CHUNK
    band = (kc >= 0) & (kc <= B_LEFT_CHUNKS)

    win = 3 * LANES
    wrow = lax.broadcasted_iota(jnp.int32, (CHUNK, win), 0)
    wcol = lax.broadcasted_iota(jnp.int32, (CHUNK, win), 1)
    idxs = tuple(jnp.clip(wrow - wcol + REL_CLIP + shift, -REL_CLIP, REL_CLIP) + REL_CLIP for shift in (0, CHUNK))

    def step(r, ws):
        val = rb_ref[h, r]
        return tuple(jnp.where(idx == r, val, w) for idx, w in zip(idxs, ws))

    init = tuple(jnp.full((CHUNK, win), far, F32) for _ in idxs)
    ws = lax.fori_loop(0, 2 * REL_CLIP, step, init)

    win_start = (3 * LANES, 3 * LANES, 4 * LANES, 4 * LANES)
    for c in range(Q_TILE // CHUNK):
        rows = slice(c * CHUNK, (c + 1) * CHUNK)
        pieces = []
        for j in range(B_TABLE_W // LANES):
            lo = j * LANES
            bnd = band[rows, lo:lo + LANES]
            if win_start[c] <= lo < win_start[c] + win:
                off = lo - win_start[c]
                val = ws[c % 2][:, off:off + LANES]
                pieces.append(jnp.where(bnd, val, NEG))
            else:
                pieces.append(jnp.where(bnd, far, NEG))
        for v in range(3):
            j0 = (2 - v) * (K_TILE // LANES)
            for j in range(B_KEYS // LANES):
                o_ref[v, rows, j * LANES:(j + 1) * LANES] = pieces[j0 + j]


def _bias_tiles(rel_bias):
    heads = rel_bias.shape[0]
    return pl.pallas_call(
        _bias_kernel,
        out_shape=jax.ShapeDtypeStruct((3, heads, Q_TILE, B_KEYS), F32),
        grid=(heads,),
        in_specs=[pl.BlockSpec(memory_space=pltpu.SMEM)],
        out_specs=pl.BlockSpec((3, None, Q_TILE, B_KEYS), lambda h: (0, h, 0, 0)),
        compiler_params=_params(1),
        name="bias_tiles",
    )(rel_bias)


def _head_rms_pair(o, g):
    lane = lax.broadcasted_iota(jnp.int32, o.shape, 1)
    first = lane < HEAD_DIM
    sq = o * o
    ms0 = jnp.sum(jnp.where(first, sq, 0.0), axis=-1, keepdims=True) * (1.0 / HEAD_DIM)
    ms1 = jnp.sum(jnp.where(first, 0.0, sq), axis=-1, keepdims=True) * (1.0 / HEAD_DIM)
    rs = jnp.where(first, lax.rsqrt(ms0 + EPS), lax.rsqrt(ms1 + EPS))
    return o * rs * g


def _attn_b_kernel(q_ref, k_ref, v_ref, bias_ref, g_ref, o_ref):
    i = pl.program_id(2)
    start = pl.multiple_of(jnp.maximum(i - 2, 0) * K_TILE, K_TILE)
    q = q_ref[...]
    k = k_ref[pl.ds(start, B_KEYS), :]
    v = v_ref[pl.ds(start, B_KEYS), :]
    lane = lax.broadcasted_iota(jnp.int32, q.shape, 1)
    zero = jnp.zeros_like(q)
    outs = []
    for hh in range(2):
        qh = jnp.where((lane < HEAD_DIM) == (hh == 0), q, zero)
        s = _dot_t(qh, k) + bias_ref[hh]
        m = jnp.max(s, axis=-1, keepdims=True)
        p = jnp.exp(s - m)
        l = jnp.sum(p, axis=-1, keepdims=True)
        outs.append(_dot(p.astype(BF16), v) / l)
    o = jnp.where(lane < HEAD_DIM, outs[0], outs[1])
    o_ref[...] = _head_rms_pair(o, g_ref[...]).astype(BF16)


def _attn_b(proj, bias_tiles, gn, bsz, seq):
    nq = seq // Q_TILE
    pairs = B_HEADS // 2
    kv_spec = lambda blk: pl.BlockSpec((seq, LANES), lambda b, h, i: (b, blk + h))
    return pl.pallas_call(
        _attn_b_kernel,
        out_shape=jax.ShapeDtypeStruct((bsz * seq, pairs * LANES), BF16),
        grid=(bsz, pairs, nq),
        in_specs=[
            pl.BlockSpec((Q_TILE, LANES), lambda b, h, i: (b * nq + i, B_Q_BLK + h)),
            kv_spec(B_K_BLK), kv_spec(B_V_BLK),
            pl.BlockSpec((None, 2, Q_TILE, B_KEYS), lambda b, h, i: (jnp.minimum(i, 2), h, 0, 0)),
            pl.BlockSpec((1, LANES), lambda b, h, i: (0, h)),
        ],
        out_specs=pl.BlockSpec((Q_TILE, LANES), lambda b, h, i: (b * nq + i, h)),
        compiler_params=_params(3),
        name="attn_b",
    )(proj, proj, proj, bias_tiles, gn)


def _attn_c_kernel(q_ref, k_ref, v_ref, g_ref, o_ref):
    qi = pl.program_id(2)
    q = q_ref[...]
    lane = lax.broadcasted_iota(jnp.int32, q.shape, 1)
    zero = jnp.zeros_like(q)
    qs = (jnp.where(lane < HEAD_DIM, q, zero), jnp.where(lane >= HEAD_DIM, q, zero))
    tri_r = lax.broadcasted_iota(jnp.int32, (K_TILE, K_TILE), 0)
    tri_c = lax.broadcasted_iota(jnp.int32, (K_TILE, K_TILE), 1)
    later = jnp.where(tri_r > tri_c, 1.0, 0.0).astype(BF16)

    def tile(kj, carry, diag):
        start = pl.multiple_of(kj * K_TILE, K_TILE)
        k = k_ref[pl.ds(start, K_TILE), :]
        v = v_ref[pl.ds(start, K_TILE), :]
        new = []
        for n in range(2):
            c, acc = carry[n]
            z = _dot_t(qs[n], k)
            sp = jnp.maximum(z, 0.0) + jnp.log1p(jnp.exp(-jnp.abs(z)))
            stay = -sp
            if diag:
                causal = tri_c < tri_r
                stay = jnp.where(causal, stay, 0.0)
            hi = stay.astype(BF16)
            lo = (stay - hi.astype(F32)).astype(BF16)
            rem = _dot(hi, later) + _dot(lo, later)
            a = jnp.exp((z - sp) + rem + c)
            if diag:
                a = jnp.where(causal, a, 0.0)
            c = c + jnp.sum(stay, axis=-1, keepdims=True)
            acc = acc + _dot(a.astype(BF16), v)
            new.append((c, acc))
        return tuple(new)

    init_one = (jnp.zeros((Q_TILE, 1), F32), jnp.zeros((Q_TILE, LANES), F32))
    carry = tile(qi, (init_one, init_one), True)
    carry = lax.fori_loop(0, qi, lambda t, c: tile(qi - 1 - t, c, False), carry)
    o = jnp.where(lane < HEAD_DIM, carry[0][1], carry[1][1])
    o_ref[...] = _head_rms_pair(o, g_ref[...]).astype(BF16)


def _attn_c(proj, gn, bsz, seq):
    nq = seq // Q_TILE
    pairs = C_HEADS // 2
    kv_spec = lambda blk: pl.BlockSpec((seq, LANES), lambda b, h, i: (b, blk + h))
    return pl.pallas_call(
        _attn_c_kernel,
        out_shape=jax.ShapeDtypeStruct((bsz * seq, pairs * LANES), BF16),
        grid=(bsz, pairs, nq),
        in_specs=[
            pl.BlockSpec((Q_TILE, LANES), lambda b, h, i: (b * nq + i, C_Q_BLK + h)),
            kv_spec(C_K_BLK), kv_spec(C_V_BLK),
            pl.BlockSpec((1, LANES), lambda b, h, i: (0, h)),
        ],
        out_specs=pl.BlockSpec((Q_TILE, LANES), lambda b, h, i: (b * nq + i, h)),
        compiler_params=_params(3),
        name="attn_c",
    )(proj, proj, proj, gn)


def _outmlp_kernel(x_ref, oa_ref, ob_ref, oc_ref, wo_ref, g1_ref, g2_ref, wu_ref, wd_ref, g3_ref, o_ref, acc_ref,
                   *, f_chunk):
    wa, wb = oa_ref.shape[1], ob_ref.shape[1]
    y = (_dot(oa_ref[...], wo_ref[0:wa, :]) + _dot(ob_ref[...], wo_ref[wa:wa + wb, :])
         + _dot(oc_ref[...], wo_ref[wa + wb:, :]))
    x1 = x_ref[...] + _rms(y, g1_ref[...])
    h = _rms(x1, g2_ref[...]).astype(BF16)
    d_ff = wu_ref.shape[1]
    for f0 in range(0, d_ff, f_chunk):
        u = jnp.maximum(_dot(h, wu_ref[:, f0:f0 + f_chunk]), 0.0)
        part = _dot((u * u).astype(BF16), wd_ref[f0:f0 + f_chunk, :])
        if f0 == 0:
            acc_ref[...] = part
        else:
            acc_ref[...] += part
    o_ref[...] = x1 + _rms(acc_ref[...], g3_ref[...])


def _outmlp(x, oa, ob, oc, w_out, g_post, g_pre, w_up, w_down, g_post2, tm=512, f_chunk=1024):
    m, d = x.shape
    d_ff = w_up.shape[1]
    row = lambda w: pl.BlockSpec((tm, w), lambda i: (i, 0))
    full = lambda a, b: pl.BlockSpec((a, b), lambda i: (0, 0), pipeline_mode=pl.Buffered(1))
    return pl.pallas_call(
        functools.partial(_outmlp_kernel, f_chunk=f_chunk),
        out_shape=jax.ShapeDtypeStruct((m, d), F32),
        grid=(m // tm,),
        in_specs=[
            row(d), row(oa.shape[1]), row(ob.shape[1]), row(oc.shape[1]),
            full(d, d), full(1, d), full(1, d), full(d, d_ff), full(d_ff, d), full(1, d),
        ],
        out_specs=row(d),
        scratch_shapes=[pltpu.VMEM((tm, d), F32)],
        compiler_params=_params(1),
        name="outproj_mlp",
    )(x, oa, ob, oc, w_out, g_post, g_pre, w_up, w_down, g_post2)


def _rope_tables(seq):
    pos = jnp.arange(seq, dtype=F32)
    inv_freq = ROPE_THETA ** (-jnp.arange(0, ROPE_DIM, 2, dtype=F32) / ROPE_DIM)
    ang = pos[:, None] * inv_freq[None, :]
    cos, sin = jnp.cos(ang), jnp.sin(ang)
    half = ROPE_DIM // 2
    d = jnp.arange(LANES) % HEAD_DIM
    f = d % half
    rc = jnp.where(d[None, :] < ROPE_DIM, cos[:, f], 1.0)
    rs1 = jnp.where(d[None, :] < half, -sin[:, f], 0.0)
    rs2 = jnp.where((d[None, :] >= half) & (d[None, :] < ROPE_DIM), sin[:, f], 0.0)
    return rc.astype(F32), rs1.astype(F32), rs2.astype(F32)


def _q_scale_columns():
    blk = jnp.arange(D_IN) // LANES
    is_q = (blk < A_K_BLK) | ((blk >= B_Q_BLK) & (blk < B_K_BLK)) | ((blk >= C_Q_BLK) & (blk < C_K_BLK))
    return jnp.where(is_q, HEAD_DIM ** -0.5, 1.0).astype(F32)


def kernel(x, norm_pre_mix, w_in, lam_q1, lam_k1, lam_q2, lam_k2, subln_a, rel_bias, gn_b, gn_c, w_out,
           norm_post_mix, norm_pre_mlp, w_up, w_down, norm_post_mlp):
    bsz, seq, d = x.shape
    depth = w_in.shape[0]
    xf = x.reshape(bsz * seq, d)
    rc, rs1, rs2 = _rope_tables(seq)
    q_scale = _q_scale_columns()
    row = lambda v: v.reshape(1, -1)
    for l in range(depth):
        lam_init = 0.8 - 0.6 * math.exp(-0.3 * l)
        w_in_l = (w_in[l] * q_scale[None, :]).astype(BF16)
        proj = _inproj(xf, row(norm_pre_mix[l]), w_in_l, rc, rs1, rs2, seq)
        lam_vecs = jnp.stack([lam_q1[l], lam_k1[l], lam_q2[l], lam_k2[l]]).astype(F32)
        oa = _attn_a(proj, lam_vecs, row(subln_a[l]), lam_init, bsz, seq)
        ob = _attn_b(proj, _bias_tiles(rel_bias[l]), row(gn_b[l]), bsz, seq)
        oc = _attn_c(proj, row(gn_c[l]), bsz, seq)
        xf = _outmlp(xf, oa, ob, oc, w_out[l].astype(BF16), row(norm_post_mix[l]), row(norm_pre_mlp[l]),
                     w_up[l].astype(BF16), w_down[l].astype(BF16), row(norm_post_mlp[l]))
    return xf.reshape(bsz, seq, d)
```

```python
import functools
import math

import jax
import jax.numpy as jnp
from jax import lax
from jax.experimental import pallas as pl
from jax.experimental.pallas import tpu as pltpu

F32 = jnp.float32
BF16 = jnp.bfloat16

EPS = 1e-6
ROPE_THETA = 500000.0
HEAD_DIM = 64
ROPE_DIM = HEAD_DIM // 4
CHUNK = 64
LANES = 128
A_HEADS = 4
B_HEADS = 4
C_HEADS = 4
B_LEFT_CHUNKS = 8
REL_CLIP = 128
NEG = -1e30
M_RESET = -5e29
C_EXIT = -110.0

A_Q_BLK, A_K_BLK, A_V_BLK = 0, 4, 8
B_Q_BLK, B_K_BLK, B_V_BLK = 12, 14, 16
C_Q_BLK, C_K_BLK, C_V_BLK = 18, 20, 22
D_IN = 24 * LANES

Q_TILE = 256
K_TILE = 256
A_TILE = 512
B_KEYS = 3 * K_TILE
B_TABLE_W = B_KEYS + 2 * K_TILE

VMEM_LIMIT = 56 * 1024 * 1024


def _params(n_axes, vmem=VMEM_LIMIT):
    return pltpu.CompilerParams(dimension_semantics=("arbitrary",) * n_axes, vmem_limit_bytes=vmem)


def _rms(x, g):
    return x * lax.rsqrt(jnp.mean(x * x, axis=-1, keepdims=True) + EPS) * g


def _dot_t(a, b):
    return lax.dot_general(a, b, (((1,), (1,)), ((), ())), preferred_element_type=F32)


def _dot(a, b):
    return jnp.dot(a, b, preferred_element_type=F32)


def _split_heads(q):
    lane = lax.broadcasted_iota(jnp.int32, q.shape, 1)
    zero = jnp.zeros_like(q)
    return jnp.where(lane < HEAD_DIM, q, zero), jnp.where(lane >= HEAD_DIM, q, zero)


def _inproj_kernel(x_ref, g_ref, w_ref, rc_ref, rs1_ref, rs2_ref, o_ref, *, n_chunk):
    h = _rms(x_ref[...], g_ref[...]).astype(BF16)
    rc, rs1, rs2 = rc_ref[...], rs1_ref[...], rs2_ref[...]
    for n0 in range(0, D_IN, n_chunk):
        y = _dot(h, w_ref[:, n0:n0 + n_chunk])
        for j in range(n_chunk // LANES):
            yj = y[:, j * LANES:(j + 1) * LANES]
            if (n0 // LANES + j) < A_V_BLK:
                yj = yj * rc + pltpu.roll(yj, LANES - ROPE_DIM // 2, 1) * rs1 + pltpu.roll(yj, ROPE_DIM // 2, 1) * rs2
            o_ref[:, n0 + j * LANES:n0 + (j + 1) * LANES] = yj.astype(BF16)


def _inproj(x, g, w, rc, rs1, rs2, seq, tm=512, n_chunk=512):
    m, d = x.shape
    pos_blocks = seq // tm
    rope_spec = pl.BlockSpec((tm, LANES), lambda i: (i % pos_blocks, 0))
    return pl.pallas_call(
        functools.partial(_inproj_kernel, n_chunk=n_chunk),
        out_shape=jax.ShapeDtypeStruct((m, D_IN), BF16),
        grid=(m // tm,),
        in_specs=[
            pl.BlockSpec((tm, d), lambda i: (i, 0)),
            pl.BlockSpec((1, d), lambda i: (0, 0)),
            pl.BlockSpec((d, D_IN), lambda i: (0, 0)),
            rope_spec, rope_spec, rope_spec,
        ],
        out_specs=pl.BlockSpec((tm, D_IN), lambda i: (i, 0)),
        compiler_params=_params(1),
        name="inproj",
    )(x, g, w, rc, rs1, rs2)


def _attn_a_kernel(q_ref, k_ref, vt_ref, lam_ref, gt_ref, o_ref, qx_ref, st_ref, s_ref, p_ref, acc_ref,
                   *, lam_init, nq):
    tq = A_TILE
    n_chunks = tq // CHUNK
    fr = lax.broadcasted_iota(jnp.int32, (2 * tq, LANES), 0)
    fc = lax.broadcasted_iota(jnp.int32, (2 * tq, LANES), 1)
    q_feat = jnp.where((fc < n_chunks) & ((fr % tq) // CHUNK == fc), 1.0, 0.0).astype(BF16)
    for blk in range(nq):
        q0, q1 = _split_heads(q_ref[blk * tq:(blk + 1) * tq, :])
        qx_ref[blk, 0:tq, 0:LANES] = q0
        qx_ref[blk, tq:2 * tq, 0:LANES] = q1
        qx_ref[blk, :, LANES:2 * LANES] = q_feat
    kr = lax.broadcasted_iota(jnp.int32, (tq, LANES), 0)
    kc_ = lax.broadcasted_iota(jnp.int32, (tq, LANES), 1)
    k_feat = jnp.where((kc_ < n_chunks) & (kr // CHUNK > kc_), NEG, 0.0).astype(BF16)
    k_nofeat = jnp.zeros_like(k_feat)

    lv = lam_ref[...]
    lam = (jnp.exp(jnp.sum(lv[0:1] * lv[1:2], axis=-1, keepdims=True))
           - jnp.exp(jnp.sum(lv[2:3] * lv[3:4], axis=-1, keepdims=True)) + lam_init)

    st_ref[0:1, :] = jnp.full((1, 2 * tq), M_RESET, F32)
    st_ref[1:2, :] = jnp.zeros((1, 2 * tq), F32)
    st_ref[2:3, :] = jnp.ones((1, 2 * tq), F32)
    s_ref[...] = jnp.full(s_ref.shape, NEG, F32)
    p_ref[...] = jnp.zeros(p_ref.shape, BF16)
    acc_ref[...] = jnp.zeros(acc_ref.shape, F32)

    def body(f, st):
        qa, ka, qb, kb, qc, kc = st
        m, l, alpha = st_ref[0:1, :], st_ref[1:2, :], st_ref[2:3, :]
        acc = alpha * acc_ref[...] + _dot(vt_ref[jnp.maximum(kc, 0)], p_ref[...])
        acc_ref[...] = acc
        s = s_ref[...]
        m_old = jnp.where(kb == 0, M_RESET, m)
        m_new = jnp.maximum(m_old, jnp.max(s, axis=0, keepdims=True))
        alpha_new = jnp.exp(m_old - m_new)
        pf = jnp.exp(s - m_new)
        p_ref[...] = pf.astype(BF16)
        st_ref[0:1, :] = m_new
        st_ref[1:2, :] = alpha_new * l + jnp.sum(pf, axis=0, keepdims=True)
        st_ref[2:3, :] = alpha_new
        k = k_ref[pl.ds(pl.multiple_of(ka * tq, tq), tq), :]
        kx = jnp.concatenate([k, jnp.where(ka == qa, k_feat, k_nofeat)], axis=1)
        s_ref[...] = _dot_t(kx, qx_ref[jnp.minimum(qa, nq - 1)])

        @pl.when(kc == qc)
        def _():
            on = acc * (1.0 / l)
            o = on[:, :tq] - lam * on[:, tq:]
            o = o * lax.rsqrt(jnp.mean(o * o, axis=0, keepdims=True) + EPS) * gt_ref[...] * (1.0 - lam_init)
            o_ref[pl.ds(pl.multiple_of(qc * tq, tq), tq), :] = o.T.astype(BF16)

        last = ka == qa
        return (jnp.where(last, qa + 1, qa), jnp.where(last, 0, ka + 1), qa, ka, qb, kb)

    zero, neg1 = jnp.int32(0), jnp.int32(-1)
    lax.fori_loop(0, nq * (nq + 1) // 2 + 2, body, (zero, zero, zero, neg1, zero, neg1))


def _attn_a(proj, vt, lam_vecs, subln_gt, lam_init, bsz, seq):
    nq = seq // A_TILE
    return pl.pallas_call(
        functools.partial(_attn_a_kernel, lam_init=lam_init, nq=nq),
        out_shape=jax.ShapeDtypeStruct((bsz * seq, A_HEADS * LANES), BF16),
        grid=(bsz, A_HEADS),
        in_specs=[
            pl.BlockSpec((seq, LANES), lambda b, h: (b, A_Q_BLK + h)),
            pl.BlockSpec((seq, LANES), lambda b, h: (b, A_K_BLK + h)),
            pl.BlockSpec((None, None, nq, LANES, A_TILE), lambda b, h: (b, h, 0, 0, 0)),
            pl.BlockSpec((4, HEAD_DIM), lambda b, h: (0, 0)),
            pl.BlockSpec((LANES, A_TILE), lambda b, h: (0, 0)),
        ],
        out_specs=pl.BlockSpec((seq, LANES), lambda b, h: (b, h)),
        scratch_shapes=[pltpu.VMEM((nq, 2 * A_TILE, 2 * LANES), BF16), pltpu.VMEM((8, 2 * A_TILE), F32),
                        pltpu.VMEM((A_TILE, 2 * A_TILE), F32), pltpu.VMEM((A_TILE, 2 * A_TILE), BF16),
                        pltpu.VMEM((LANES, 2 * A_TILE), F32)],
        compiler_params=_params(2),
        name="attn_a",
    )(proj, proj, vt, lam_vecs, subln_gt)


def _bias_kernel(rb_ref, o_ref):
    h = pl.program_id(0)
    far = rb_ref[h, 2 * REL_CLIP]
    shape = (Q_TILE, B_TABLE_W)
    row = lax.broadcasted_iota(jnp.int32, shape, 0)
    col = lax.broadcasted_iota(jnp.int32, shape, 1)
    kc = col // CHUNK - row // CHUNK
    band = (kc >= 0) & (kc <= B_LEFT_CHUNKS)

    win = 3 * LANES
    wrow = lax.broadcasted_iota(jnp.int32, (CHUNK, win), 0)
    wcol = lax.broadcasted_iota(jnp.int32, (CHUNK, win), 1)
    idxs = tuple(jnp.clip(wrow - wcol + REL_CLIP + shift, -REL_CLIP, REL_CLIP) + REL_CLIP for shift in (0, CHUNK))

    def step(r, ws):
        val = rb_ref[h, r]
        return tuple(jnp.where(idx == r, val, w) for idx, w in zip(idxs, ws))

    init = tuple(jnp.full((CHUNK, win), far, F32) for _ in idxs)
    ws = lax.fori_loop(0, 2 * REL_CLIP, step, init)

    win_start = (3 * LANES, 3 * LANES, 4 * LANES, 4 * LANES)
    for c in range(Q_TILE // CHUNK):
        rows = slice(c * CHUNK, (c + 1) * CHUNK)
        pieces = []
        for j in range(B_TABLE_W // LANES):
            lo = j * LANES
            bnd = band[rows, lo:lo + LANES]
            if win_start[c] <= lo < win_start[c] + win:
                off = lo - win_start[c]
                val = ws[c % 2][:, off:off + LANES]
                pieces.append(jnp.where(bnd, val, NEG))
            else:
                pieces.append(jnp.where(bnd, far, NEG))
        for v in range(3):
            j0 = (2 - v) * (K_TILE // LANES)
            for j in range(B_KEYS // LANES):
                o_ref[v, rows, j * LANES:(j + 1) * LANES] = pieces[j0 + j]


def _bias_tiles(rel_bias):
    heads = rel_bias.shape[0]
    return pl.pallas_call(
        _bias_kernel,
        out_shape=jax.ShapeDtypeStruct((3, heads, Q_TILE, B_KEYS), F32),
        grid=(heads,),
        in_specs=[pl.BlockSpec(memory_space=pltpu.SMEM)],
        out_specs=pl.BlockSpec((3, None, Q_TILE, B_KEYS), lambda h: (0, h, 0, 0)),
        compiler_params=_params(1),
        name="bias_tiles",
    )(rel_bias)


def _head_rms_pair(o, g):
    lane = lax.broadcasted_iota(jnp.int32, o.shape, 1)
    first = lane < HEAD_DIM
    sq = o * o
    ms0 = jnp.sum(jnp.where(first, sq, 0.0), axis=-1, keepdims=True) * (1.0 / HEAD_DIM)
    ms1 = jnp.sum(jnp.where(first, 0.0, sq), axis=-1, keepdims=True) * (1.0 / HEAD_DIM)
    rs = jnp.where(first, lax.rsqrt(ms0 + EPS), lax.rsqrt(ms1 + EPS))
    return o * rs * g


def _attn_b_kernel(q_ref, k_ref, v_ref, bias_ref, g_ref, o_ref):
    i = pl.program_id(2)
    start = pl.multiple_of(jnp.maximum(i - 2, 0) * K_TILE, K_TILE)
    q = q_ref[...]
    k = k_ref[pl.ds(start, B_KEYS), :]
    v = v_ref[pl.ds(start, B_KEYS), :]
    lane = lax.broadcasted_iota(jnp.int32, q.shape, 1)
    outs = []
    for hh, qh in enumerate(_split_heads(q)):
        s = _dot_t(qh, k) + bias_ref[hh]
        m = jnp.max(s, axis=-1, keepdims=True)
        p = jnp.exp(s - m)
        l = jnp.sum(p, axis=-1, keepdims=True)
        outs.append(_dot(p.astype(BF16), v) / l)
    o = jnp.where(lane < HEAD_DIM, outs[0], outs[1])
    o_ref[...] = _head_rms_pair(o, g_ref[...]).astype(BF16)


def _attn_b(proj, bias_tiles, gn, bsz, seq):
    nq = seq // Q_TILE
    pairs = B_HEADS // 2
    kv_spec = lambda blk: pl.BlockSpec((seq, LANES), lambda b, h, i: (b, blk + h))
    return pl.pallas_call(
        _attn_b_kernel,
        out_shape=jax.ShapeDtypeStruct((bsz * seq, pairs * LANES), BF16),
        grid=(bsz, pairs, nq),
        in_specs=[
            pl.BlockSpec((Q_TILE, LANES), lambda b, h, i: (b * nq + i, B_Q_BLK + h)),
            kv_spec(B_K_BLK), kv_spec(B_V_BLK),
            pl.BlockSpec((None, 2, Q_TILE, B_KEYS), lambda b, h, i: (jnp.minimum(i, 2), h, 0, 0)),
            pl.BlockSpec((1, LANES), lambda b, h, i: (0, h)),
        ],
        out_specs=pl.BlockSpec((Q_TILE, LANES), lambda b, h, i: (b * nq + i, h)),
        compiler_params=_params(3),
        name="attn_b",
    )(proj, proj, proj, bias_tiles, gn)


def _attn_c_kernel(q_ref, k_ref, vt_ref, gt_ref, o_ref):
    qi = pl.program_id(2)
    qs = _split_heads(q_ref[...])
    tri_r = lax.broadcasted_iota(jnp.int32, (K_TILE, K_TILE), 0)
    tri_c = lax.broadcasted_iota(jnp.int32, (K_TILE, K_TILE), 1)
    later = jnp.where(tri_c > tri_r, 1.0, 0.0).astype(BF16)

    def tile(kj, carry, diag):
        start = pl.multiple_of(kj * K_TILE, K_TILE)
        k = k_ref[pl.ds(start, K_TILE), :]
        vt = vt_ref[kj]
        new = []
        for n in range(2):
            c, acc = carry[n]
            z = _dot_t(k, qs[n])
            sp = jnp.maximum(z, 0.0) + jnp.log1p(jnp.exp(-jnp.abs(z)))
            stay = -sp
            if diag:
                causal = tri_r < tri_c
                stay = jnp.where(causal, stay, 0.0)
            hi = stay.astype(BF16)
            lo = (stay - hi.astype(F32)).astype(BF16)
            rem = _dot(later, hi) + _dot(later, lo)
            a = jnp.exp((z - sp) + rem + c)
            if diag:
                a = jnp.where(causal, a, 0.0)
            c = c + jnp.sum(stay, axis=0, keepdims=True)
            acc = acc + _dot(vt[n * HEAD_DIM:(n + 1) * HEAD_DIM, :], a.astype(BF16))
            new.append((c, acc))
        return tuple(new)

    init_one = (jnp.zeros((1, Q_TILE), F32), jnp.zeros((HEAD_DIM, Q_TILE), F32))
    carry = tile(qi, (init_one, init_one), True)

    def live(state):
        t, cr = state
        return (t < qi) & (jnp.maximum(jnp.max(cr[0][0]), jnp.max(cr[1][0])) > C_EXIT)

    def step(state):
        t, cr = state
        return t + 1, tile(qi - 1 - t, cr, False)

    _, carry = lax.while_loop(live, step, (jnp.int32(0), carry))
    gt = gt_ref[...]
    outs = []
    for n in range(2):
        o = carry[n][1]
        rs = lax.rsqrt(jnp.mean(o * o, axis=0, keepdims=True) + EPS)
        outs.append(o * rs * gt[n * HEAD_DIM:(n + 1) * HEAD_DIM, :])
    o_ref[...] = jnp.concatenate(outs, axis=0).T.astype(BF16)


def _attn_c(proj, vt, gn_t, bsz, seq):
    nq = seq // Q_TILE
    nk = seq // K_TILE
    pairs = C_HEADS // 2
    return pl.pallas_call(
        _attn_c_kernel,
        out_shape=jax.ShapeDtypeStruct((bsz * seq, pairs * LANES), BF16),
        grid=(bsz, pairs, nq),
        in_specs=[
            pl.BlockSpec((Q_TILE, LANES), lambda b, h, i: (b * nq + i, C_Q_BLK + h)),
            pl.BlockSpec((seq, LANES), lambda b, h, i: (b, C_K_BLK + h)),
            pl.BlockSpec((None, None, nk, LANES, K_TILE), lambda b, h, i: (b, h, 0, 0, 0)),
            pl.BlockSpec((LANES, Q_TILE), lambda b, h, i: (h, 0)),
        ],
        out_specs=pl.BlockSpec((Q_TILE, LANES), lambda b, h, i: (b * nq + i, h)),
        compiler_params=_params(3),
        name="attn_c",
    )(proj, proj, vt, gn_t)


def _outmlp_kernel(x_ref, oa_ref, ob_ref, oc_ref, wo_ref, g1_ref, g2_ref, wu_ref, wd_ref, g3_ref, o_ref, acc_ref,
                   *, f_chunk):
    wa, wb = oa_ref.shape[1], ob_ref.shape[1]
    y = (_dot(oa_ref[...], wo_ref[0:wa, :]) + _dot(ob_ref[...], wo_ref[wa:wa + wb, :])
         + _dot(oc_ref[...], wo_ref[wa + wb:, :]))
    x1 = x_ref[...] + _rms(y, g1_ref[...])
    h = _rms(x1, g2_ref[...]).astype(BF16)
    d_ff = wu_ref.shape[1]
    for f0 in range(0, d_ff, f_chunk):
        u = jnp.maximum(_dot(h, wu_ref[:, f0:f0 + f_chunk]), 0.0)
        part = _dot((u * u).astype(BF16), wd_ref[f0:f0 + f_chunk, :])
        if f0 == 0:
            acc_ref[...] = part
        else:
            acc_ref[...] += part
    o_ref[...] = x1 + _rms(acc_ref[...], g3_ref[...])


def _outmlp(x, oa, ob, oc, w_out, g_post, g_pre, w_up, w_down, g_post2, tm=512, f_chunk=1024):
    m, d = x.shape
    d_ff = w_up.shape[1]
    row = lambda w: pl.BlockSpec((tm, w), lambda i: (i, 0))
    full = lambda a, b: pl.BlockSpec((a, b), lambda i: (0, 0), pipeline_mode=pl.Buffered(1))
    return pl.pallas_call(
        functools.partial(_outmlp_kernel, f_chunk=f_chunk),
        out_shape=jax.ShapeDtypeStruct((m, d), F32),
        grid=(m // tm,),
        in_specs=[
            row(d), row(oa.shape[1]), row(ob.shape[1]), row(oc.shape[1]),
            full(d, d), full(1, d), full(1, d), full(d, d_ff), full(d_ff, d), full(1, d),
        ],
        out_specs=row(d),
        scratch_shapes=[pltpu.VMEM((tm, d), F32)],
        compiler_params=_params(1),
        name="outproj_mlp",
    )(x, oa, ob, oc, w_out, g_post, g_pre, w_up, w_down, g_post2)


def _rope_tables(seq):
    pos = jnp.arange(seq, dtype=F32)
    inv_freq = ROPE_THETA ** (-jnp.arange(0, ROPE_DIM, 2, dtype=F32) / ROPE_DIM)
    ang = pos[:, None] * inv_freq[None, :]
    cos, sin = jnp.cos(ang), jnp.sin(ang)
    half = ROPE_DIM // 2
    d = jnp.arange(LANES) % HEAD_DIM
    f = d % half
    rc = jnp.where(d[None, :] < ROPE_DIM, cos[:, f], 1.0)
    rs1 = jnp.where(d[None, :] < half, -sin[:, f], 0.0)
    rs2 = jnp.where((d[None, :] >= half) & (d[None, :] < ROPE_DIM), sin[:, f], 0.0)
    return rc.astype(F32), rs1.astype(F32), rs2.astype(F32)


def _q_scale_columns():
    blk = jnp.arange(D_IN) // LANES
    is_q = (blk < A_K_BLK) | ((blk >= B_Q_BLK) & (blk < B_K_BLK)) | ((blk >= C_Q_BLK) & (blk < C_K_BLK))
    return jnp.where(is_q, HEAD_DIM ** -0.5, 1.0).astype(F32)


def _value_tiles_t(proj, blk, n_blk, bsz, seq, tile):
    v = proj[:, blk * LANES:(blk + n_blk) * LANES].reshape(bsz, seq // tile, tile, n_blk, LANES)
    return v.transpose(0, 3, 1, 4, 2)


def _gain_t(g, n_blk, width):
    return jnp.broadcast_to(g.reshape(n_blk * LANES, 1), (n_blk * LANES, width)).astype(F32)


def kernel(x, norm_pre_mix, w_in, lam_q1, lam_k1, lam_q2, lam_k2, subln_a, rel_bias, gn_b, gn_c, w_out,
           norm_post_mix, norm_pre_mlp, w_up, w_down, norm_post_mlp):
    bsz, seq, d = x.shape
    depth = w_in.shape[0]
    xf = x.reshape(bsz * seq, d)
    rc, rs1, rs2 = _rope_tables(seq)
    q_scale = _q_scale_columns()
    row = lambda v: v.reshape(1, -1)
    for l in range(depth):
        lam_init = 0.8 - 0.6 * math.exp(-0.3 * l)
        w_in_l = (w_in[l] * q_scale[None, :]).astype(BF16)
        proj = _inproj(xf, row(norm_pre_mix[l]), w_in_l, rc, rs1, rs2, seq)
        lam_vecs = jnp.stack([lam_q1[l], lam_k1[l], lam_q2[l], lam_k2[l]]).astype(F32)
        oa = _attn_a(proj, _value_tiles_t(proj, A_V_BLK, A_HEADS, bsz, seq, A_TILE), lam_vecs,
                     _gain_t(subln_a[l], 1, A_TILE), lam_init, bsz, seq)
        ob = _attn_b(proj, _bias_tiles(rel_bias[l]), row(gn_b[l]), bsz, seq)
        oc = _attn_c(proj, _value_tiles_t(proj, C_V_BLK, C_HEADS // 2, bsz, seq, K_TILE),
                     _gain_t(gn_c[l], C_HEADS // 2, Q_TILE), bsz, seq)
        xf = _outmlp(xf, oa, ob, oc, w_out[l].astype(BF16), row(norm_post_mix[l]), row(norm_pre_mlp[l]),
                     w_up[l].astype(BF16), w_down[l].astype(BF16), row(norm_post_mlp[l]))
    return xf.reshape(bsz, seq, d)
```

```python
import functools
import math

import jax
import jax.numpy as jnp
from jax import lax
from jax.experimental import pallas as pl
from jax.experimental.pallas import tpu as pltpu

F32 = jnp.float32
BF16 = jnp.bfloat16

EPS = 1e-6
ROPE_THETA = 500000.0
HEAD_DIM = 64
ROPE_DIM = HEAD_DIM // 4
CHUNK = 64
LANES = 128
A_HEADS = 4
B_HEADS = 4
C_HEADS = 4
B_LEFT_CHUNKS = 8
REL_CLIP = 128
NEG = -1e30
M_RESET = -5e29
C_EXIT = -110.0

A_Q_BLK, A_K_BLK, A_V_BLK = 0, 4, 8
B_Q_BLK, B_K_BLK, B_V_BLK = 12, 14, 16
C_Q_BLK, C_K_BLK, C_V_BLK = 18, 20, 22
D_IN = 24 * LANES

Q_TILE = 256
K_TILE = 256
A_TILE = 512
A_UNROLL = 2
V_ROWS = LANES + 16
B_KEYS = 3 * K_TILE
B_TABLE_W = B_KEYS + 2 * K_TILE

VMEM_LIMIT = 56 * 1024 * 1024


def _params(n_axes, vmem=VMEM_LIMIT):
    return pltpu.CompilerParams(dimension_semantics=("arbitrary",) * n_axes, vmem_limit_bytes=vmem)


def _rms(x, g):
    return x * lax.rsqrt(jnp.mean(x * x, axis=-1, keepdims=True) + EPS) * g


def _dot_t(a, b):
    return lax.dot_general(a, b, (((1,), (1,)), ((), ())), preferred_element_type=F32)


def _dot(a, b):
    return jnp.dot(a, b, preferred_element_type=F32)


def _split_heads(q):
    lane = lax.broadcasted_iota(jnp.int32, q.shape, 1)
    zero = jnp.zeros_like(q)
    return jnp.where(lane < HEAD_DIM, q, zero), jnp.where(lane >= HEAD_DIM, q, zero)


def _inproj_kernel(x_ref, g_ref, w_ref, rc_ref, rs1_ref, rs2_ref, o_ref, *, n_chunk):
    h = _rms(x_ref[...], g_ref[...]).astype(BF16)
    rc, rs1, rs2 = rc_ref[...], rs1_ref[...], rs2_ref[...]
    for n0 in range(0, D_IN, n_chunk):
        y = _dot(h, w_ref[:, n0:n0 + n_chunk])
        for j in range(n_chunk // LANES):
            yj = y[:, j * LANES:(j + 1) * LANES]
            if (n0 // LANES + j) < A_V_BLK:
                yj = yj * rc + pltpu.roll(yj, LANES - ROPE_DIM // 2, 1) * rs1 + pltpu.roll(yj, ROPE_DIM // 2, 1) * rs2
            o_ref[:, n0 + j * LANES:n0 + (j + 1) * LANES] = yj.astype(BF16)


def _inproj(x, g, w, rc, rs1, rs2, seq, tm=512, n_chunk=512):
    m, d = x.shape
    pos_blocks = seq // tm
    rope_spec = pl.BlockSpec((tm, LANES), lambda i: (i % pos_blocks, 0))
    return pl.pallas_call(
        functools.partial(_inproj_kernel, n_chunk=n_chunk),
        out_shape=jax.ShapeDtypeStruct((m, D_IN), BF16),
        grid=(m // tm,),
        in_specs=[
            pl.BlockSpec((tm, d), lambda i: (i, 0)),
            pl.BlockSpec((1, d), lambda i: (0, 0)),
            pl.BlockSpec((d, D_IN), lambda i: (0, 0)),
            rope_spec, rope_spec, rope_spec,
        ],
        out_specs=pl.BlockSpec((tm, D_IN), lambda i: (i, 0)),
        compiler_params=_params(1),
        name="inproj",
    )(x, g, w, rc, rs1, rs2)


def _attn_a_kernel(q_ref, k_ref, vt_ref, lam_ref, gt_ref, o_ref, qx_ref, m_ref, mx_ref, s_ref, acc_ref,
                   *, lam_init, nq):
    tq = A_TILE
    n_chunks = tq // CHUNK
    fr = lax.broadcasted_iota(jnp.int32, (2 * tq, LANES), 0)
    fc = lax.broadcasted_iota(jnp.int32, (2 * tq, LANES), 1)
    q_feat = jnp.where((fc < n_chunks) & ((fr % tq) // CHUNK == fc), 1.0, 0.0).astype(BF16)
    for blk in range(nq):
        q0, q1 = _split_heads(q_ref[blk * tq:(blk + 1) * tq, :])
        qx_ref[blk, 0:tq, 0:LANES] = q0
        qx_ref[blk, tq:2 * tq, 0:LANES] = q1
        qx_ref[blk, :, LANES:2 * LANES] = q_feat
    kr = lax.broadcasted_iota(jnp.int32, (tq, LANES), 0)
    kc_ = lax.broadcasted_iota(jnp.int32, (tq, LANES), 1)
    k_feat = jnp.where((kc_ < n_chunks) & (kr // CHUNK > kc_), NEG, 0.0).astype(BF16)
    k_nofeat = jnp.zeros_like(k_feat)

    m_ref[...] = jnp.full(m_ref.shape, M_RESET, F32)
    mx_ref[...] = jnp.full(mx_ref.shape, NEG, F32)
    s_ref[1] = jnp.full(s_ref.shape[1:], NEG, F32)
    acc_ref[...] = jnp.zeros(acc_ref.shape, F32)

    def step(st, slot):
        qa, ka, qb, kb = st
        other = 1 - slot
        k = k_ref[pl.ds(pl.multiple_of(jnp.minimum(ka, nq - 1) * tq, tq), tq), :]
        kx = jnp.concatenate([k, jnp.where(ka == qa, k_feat, k_nofeat)], axis=1)
        s_new = _dot_t(kx, qx_ref[jnp.minimum(qa, nq - 1)])
        s_ref[slot] = s_new
        mx_ref[slot, 0:1, :] = jnp.max(s_new, axis=0, keepdims=True)
        m_old = jnp.where(kb == 0, M_RESET, m_ref[0:1, :])
        m_new = jnp.maximum(m_old, mx_ref[other, 0:1, :])
        m_ref[0:1, :] = m_new
        p = jnp.exp2(s_ref[other] - m_new).astype(BF16)
        blk = jnp.minimum(qb, nq)
        acc_ref[blk] = jnp.exp2(m_old - m_new) * acc_ref[blk] + _dot(vt_ref[jnp.clip(kb, 0, nq - 1)], p)
        last = ka == qa
        return (jnp.where(last, qa + 1, qa), jnp.where(last, 0, ka + 1), qa, ka)

    def steps(i, st):
        for u in range(A_UNROLL):
            st = step(st, u % 2)
        return st

    n_steps = nq * (nq + 1) // 2 + 1
    zero, neg1 = jnp.int32(0), jnp.int32(-1)
    lax.fori_loop(0, -(-n_steps // A_UNROLL), steps, (zero, zero, zero, neg1))

    lv = lam_ref[...]
    lam = (jnp.exp(jnp.sum(lv[0:1] * lv[1:2], axis=-1, keepdims=True))
           - jnp.exp(jnp.sum(lv[2:3] * lv[3:4], axis=-1, keepdims=True)) + lam_init)
    for blk in range(nq):
        acc = acc_ref[blk]
        on = acc[0:LANES] * (1.0 / acc[LANES:LANES + 1])
        o = on[:, :tq] - lam * on[:, tq:]
        o = o * lax.rsqrt(jnp.mean(o * o, axis=0, keepdims=True) + EPS) * gt_ref[...] * (1.0 - lam_init)
        o_ref[blk * tq:(blk + 1) * tq, :] = o.T.astype(BF16)


def _attn_a(proj, vt, lam_vecs, subln_gt, lam_init, bsz, seq):
    nq = seq // A_TILE
    return pl.pallas_call(
        functools.partial(_attn_a_kernel, lam_init=lam_init, nq=nq),
        out_shape=jax.ShapeDtypeStruct((bsz * seq, A_HEADS * LANES), BF16),
        grid=(bsz, A_HEADS),
        in_specs=[
            pl.BlockSpec((seq, LANES), lambda b, h: (b, A_Q_BLK + h)),
            pl.BlockSpec((seq, LANES), lambda b, h: (b, A_K_BLK + h)),
            pl.BlockSpec((None, None, nq, V_ROWS, A_TILE), lambda b, h: (b, h, 0, 0, 0)),
            pl.BlockSpec((4, HEAD_DIM), lambda b, h: (0, 0)),
            pl.BlockSpec((LANES, A_TILE), lambda b, h: (0, 0)),
        ],
        out_specs=pl.BlockSpec((seq, LANES), lambda b, h: (b, h)),
        scratch_shapes=[pltpu.VMEM((nq, 2 * A_TILE, 2 * LANES), BF16), pltpu.VMEM((8, 2 * A_TILE), F32),
                        pltpu.VMEM((2, 8, 2 * A_TILE), F32),
                        pltpu.VMEM((2, A_TILE, 2 * A_TILE), F32),
                        pltpu.VMEM((nq + 1, V_ROWS, 2 * A_TILE), F32)],
        compiler_params=_params(2),
        name="attn_a",
    )(proj, proj, vt, lam_vecs, subln_gt)


def _bias_kernel(rb_ref, o_ref):
    h = pl.program_id(0)
    far = rb_ref[h, 2 * REL_CLIP]
    shape = (Q_TILE, B_TABLE_W)
    row = lax.broadcasted_iota(jnp.int32, shape, 0)
    col = lax.broadcasted_iota(jnp.int32, shape, 1)
    kc = col // CHUNK - row // CHUNK
    band = (kc >= 0) & (kc <= B_LEFT_CHUNKS)

    win = 3 * LANES
    wrow = lax.broadcasted_iota(jnp.int32, (CHUNK, win), 0)
    wcol = lax.broadcasted_iota(jnp.int32, (CHUNK, win), 1)
    idxs = tuple(jnp.clip(wrow - wcol + REL_CLIP + shift, -REL_CLIP, REL_CLIP) + REL_CLIP for shift in (0, CHUNK))

    def step(r, ws):
        val = rb_ref[h, r]
        return tuple(jnp.where(idx == r, val, w) for idx, w in zip(idxs, ws))

    init = tuple(jnp.full((CHUNK, win), far, F32) for _ in idxs)
    ws = lax.fori_loop(0, 2 * REL_CLIP, step, init)

    win_start = (3 * LANES, 3 * LANES, 4 * LANES, 4 * LANES)
    for c in range(Q_TILE // CHUNK):
        rows = slice(c * CHUNK, (c + 1) * CHUNK)
        pieces = []
        for j in range(B_TABLE_W // LANES):
            lo = j * LANES
            bnd = band[rows, lo:lo + LANES]
            if win_start[c] <= lo < win_start[c] + win:
                off = lo - win_start[c]
                val = ws[c % 2][:, off:off + LANES]
                pieces.append(jnp.where(bnd, val, NEG))
            else:
                pieces.append(jnp.where(bnd, far, NEG))
        for v in range(3):
            j0 = (2 - v) * (K_TILE // LANES)
            for j in range(B_KEYS // LANES):
                o_ref[v, rows, j * LANES:(j + 1) * LANES] = pieces[j0 + j]


def _bias_tiles(rel_bias):
    heads = rel_bias.shape[0]
    return pl.pallas_call(
        _bias_kernel,
        out_shape=jax.ShapeDtypeStruct((3, heads, Q_TILE, B_KEYS), F32),
        grid=(heads,),
        in_specs=[pl.BlockSpec(memory_space=pltpu.SMEM)],
        out_specs=pl.BlockSpec((3, None, Q_TILE, B_KEYS), lambda h: (0, h, 0, 0)),
        compiler_params=_params(1),
        name="bias_tiles",
    )(rel_bias)


def _head_rms_pair(o, g):
    lane = lax.broadcasted_iota(jnp.int32, o.shape, 1)
    first = lane < HEAD_DIM
    sq = o * o
    ms0 = jnp.sum(jnp.where(first, sq, 0.0), axis=-1, keepdims=True) * (1.0 / HEAD_DIM)
    ms1 = jnp.sum(jnp.where(first, 0.0, sq), axis=-1, keepdims=True) * (1.0 / HEAD_DIM)
    rs = jnp.where(first, lax.rsqrt(ms0 + EPS), lax.rsqrt(ms1 + EPS))
    return o * rs * g


def _attn_b_kernel(q_ref, k_ref, v_ref, bias_ref, g_ref, o_ref):
    i = pl.program_id(2)
    start = pl.multiple_of(jnp.maximum(i - 2, 0) * K_TILE, K_TILE)
    q = q_ref[...]
    k = k_ref[pl.ds(start, B_KEYS), :]
    v = v_ref[pl.ds(start, B_KEYS), :]
    lane = lax.broadcasted_iota(jnp.int32, q.shape, 1)
    outs = []
    for hh, qh in enumerate(_split_heads(q)):
        s = _dot_t(qh, k) + bias_ref[hh]
        m = jnp.max(s, axis=-1, keepdims=True)
        p = jnp.exp(s - m)
        l = jnp.sum(p, axis=-1, keepdims=True)
        outs.append(_dot(p.astype(BF16), v) / l)
    o = jnp.where(lane < HEAD_DIM, outs[0], outs[1])
    o_ref[...] = _head_rms_pair(o, g_ref[...]).astype(BF16)


def _attn_b(proj, bias_tiles, gn, bsz, seq):
    nq = seq // Q_TILE
    pairs = B_HEADS // 2
    kv_spec = lambda blk: pl.BlockSpec((seq, LANES), lambda b, h, i: (b, blk + h))
    return pl.pallas_call(
        _attn_b_kernel,
        out_shape=jax.ShapeDtypeStruct((bsz * seq, pairs * LANES), BF16),
        grid=(bsz, pairs, nq),
        in_specs=[
            pl.BlockSpec((Q_TILE, LANES), lambda b, h, i: (b * nq + i, B_Q_BLK + h)),
            kv_spec(B_K_BLK), kv_spec(B_V_BLK),
            pl.BlockSpec((None, 2, Q_TILE, B_KEYS), lambda b, h, i: (jnp.minimum(i, 2), h, 0, 0)),
            pl.BlockSpec((1, LANES), lambda b, h, i: (0, h)),
        ],
        out_specs=pl.BlockSpec((Q_TILE, LANES), lambda b, h, i: (b * nq + i, h)),
        compiler_params=_params(3),
        name="attn_b",
    )(proj, proj, proj, bias_tiles, gn)


def _softplus(z):
    return jnp.maximum(z, 0.0) + jnp.log1p(jnp.exp(-jnp.abs(z)))


def _split_bf16(x):
    hi = x.astype(BF16)
    return hi, (x - hi.astype(F32)).astype(BF16)


def _attn_c_kernel(q_ref, k_ref, vt_ref, gt_ref, o_ref):
    qi = pl.program_id(2)
    tq, tk = Q_TILE, K_TILE
    nc = 2 * tq
    qcat = jnp.concatenate(_split_heads(q_ref[...]), axis=0)
    tri_r = lax.broadcasted_iota(jnp.int32, (tk, tk), 0)
    tri_c = lax.broadcasted_iota(jnp.int32, (tk, tk), 1)
    later = jnp.where(tri_c > tri_r, 1.0, 0.0).astype(BF16)

    def suffix_sums(stay):
        hi, lo = _split_bf16(stay)
        return _dot(later, hi) + _dot(later, lo)

    def values(vt, a):
        return tuple(_dot(vt[n * HEAD_DIM:(n + 1) * HEAD_DIM, :], a[:, n * tq:(n + 1) * tq]) for n in range(2))

    j0 = jnp.maximum(qi - 1, 0)
    z = _dot_t(k_ref[pl.ds(pl.multiple_of(j0 * tk, tk), 2 * tk), :], qcat)
    krow = lax.broadcasted_iota(jnp.int32, z.shape, 0)
    qcol = lax.broadcasted_iota(jnp.int32, z.shape, 1) % tq
    causal = krow < qcol + (qi - j0) * tk
    sp = _softplus(z)
    stay = jnp.where(causal, -sp, 0.0)
    rem_hi = suffix_sums(stay[tk:])
    rem_lo = suffix_sums(stay[:tk]) + jnp.sum(stay[tk:], axis=0, keepdims=True)
    a = jnp.where(causal, jnp.exp((z - sp) + jnp.concatenate([rem_lo, rem_hi], axis=0)), 0.0).astype(BF16)
    c = jnp.sum(stay, axis=0, keepdims=True)
    acc = tuple(x + y for x, y in zip(values(vt_ref[j0], a[:tk]), values(vt_ref[j0 + 1], a[tk:])))

    def live(state):
        t, c, _, _ = state
        return (t < j0) & (jnp.max(c) > C_EXIT)

    def sweep(state):
        t, c, acc0, acc1 = state
        kj = j0 - 1 - t
        z = _dot_t(k_ref[pl.ds(pl.multiple_of(kj * tk, tk), tk), :], qcat)
        sp = _softplus(z)
        a = jnp.exp((z - sp) + suffix_sums(-sp) + c).astype(BF16)
        d0, d1 = values(vt_ref[kj], a)
        return t + 1, c - jnp.sum(sp, axis=0, keepdims=True), acc0 + d0, acc1 + d1

    _, _, acc0, acc1 = lax.while_loop(live, sweep, (jnp.int32(0), c, acc[0], acc[1]))
    gt = gt_ref[...]
    outs = []
    for n, o in enumerate((acc0, acc1)):
        rs = lax.rsqrt(jnp.mean(o * o, axis=0, keepdims=True) + EPS)
        outs.append(o * rs * gt[n * HEAD_DIM:(n + 1) * HEAD_DIM, :])
    o_ref[...] = jnp.concatenate(outs, axis=0).T.astype(BF16)


def _attn_c(proj, vt, gn_t, bsz, seq):
    nq = seq // Q_TILE
    nk = seq // K_TILE
    pairs = C_HEADS // 2
    return pl.pallas_call(
        _attn_c_kernel,
        out_shape=jax.ShapeDtypeStruct((bsz * seq, pairs * LANES), BF16),
        grid=(bsz, pairs, nq),
        in_specs=[
            pl.BlockSpec((Q_TILE, LANES), lambda b, h, i: (b * nq + i, C_Q_BLK + h)),
            pl.BlockSpec((seq, LANES), lambda b, h, i: (b, C_K_BLK + h)),
            pl.BlockSpec((None, None, nk, LANES, K_TILE), lambda b, h, i: (b, h, 0, 0, 0)),
            pl.BlockSpec((LANES, Q_TILE), lambda b, h, i: (h, 0)),
        ],
        out_specs=pl.BlockSpec((Q_TILE, LANES), lambda b, h, i: (b * nq + i, h)),
        compiler_params=_params(3),
        name="attn_c",
    )(proj, proj, vt, gn_t)


def _outmlp_kernel(x_ref, oa_ref, ob_ref, oc_ref, wo_ref, g1_ref, g2_ref, wu_ref, wd_ref, g3_ref, o_ref, acc_ref,
                   *, f_chunk):
    wa, wb = oa_ref.shape[1], ob_ref.shape[1]
    y = (_dot(oa_ref[...], wo_ref[0:wa, :]) + _dot(ob_ref[...], wo_ref[wa:wa + wb, :])
         + _dot(oc_ref[...], wo_ref[wa + wb:, :]))
    x1 = x_ref[...] + _rms(y, g1_ref[...])
    h = _rms(x1, g2_ref[...]).astype(BF16)
    d_ff = wu_ref.shape[1]
    for f0 in range(0, d_ff, f_chunk):
        u = jnp.maximum(_dot(h, wu_ref[:, f0:f0 + f_chunk]), 0.0)
        part = _dot((u * u).astype(BF16), wd_ref[f0:f0 + f_chunk, :])
        if f0 == 0:
            acc_ref[...] = part
        else:
            acc_ref[...] += part
    o_ref[...] = x1 + _rms(acc_ref[...], g3_ref[...])


def _outmlp(x, oa, ob, oc, w_out, g_post, g_pre, w_up, w_down, g_post2, tm=512, f_chunk=1024):
    m, d = x.shape
    d_ff = w_up.shape[1]
    row = lambda w: pl.BlockSpec((tm, w), lambda i: (i, 0))
    full = lambda a, b: pl.BlockSpec((a, b), lambda i: (0, 0), pipeline_mode=pl.Buffered(1))
    return pl.pallas_call(
        functools.partial(_outmlp_kernel, f_chunk=f_chunk),
        out_shape=jax.ShapeDtypeStruct((m, d), F32),
        grid=(m // tm,),
        in_specs=[
            row(d), row(oa.shape[1]), row(ob.shape[1]), row(oc.shape[1]),
            full(d, d), full(1, d), full(1, d), full(d, d_ff), full(d_ff, d), full(1, d),
        ],
        out_specs=row(d),
        scratch_shapes=[pltpu.VMEM((tm, d), F32)],
        compiler_params=_params(1),
        name="outproj_mlp",
    )(x, oa, ob, oc, w_out, g_post, g_pre, w_up, w_down, g_post2)


def _rope_tables(seq):
    pos = jnp.arange(seq, dtype=F32)
    inv_freq = ROPE_THETA ** (-jnp.arange(0, ROPE_DIM, 2, dtype=F32) / ROPE_DIM)
    ang = pos[:, None] * inv_freq[None, :]
    cos, sin = jnp.cos(ang), jnp.sin(ang)
    half = ROPE_DIM // 2
    d = jnp.arange(LANES) % HEAD_DIM
    f = d % half
    rc = jnp.where(d[None, :] < ROPE_DIM, cos[:, f], 1.0)
    rs1 = jnp.where(d[None, :] < half, -sin[:, f], 0.0)
    rs2 = jnp.where((d[None, :] >= half) & (d[None, :] < ROPE_DIM), sin[:, f], 0.0)
    return rc.astype(F32), rs1.astype(F32), rs2.astype(F32)


def _q_scale_columns():
    blk = jnp.arange(D_IN) // LANES
    is_q = (blk < A_K_BLK) | ((blk >= B_Q_BLK) & (blk < B_K_BLK)) | ((blk >= C_Q_BLK) & (blk < C_K_BLK))
    scale = jnp.where(is_q, HEAD_DIM ** -0.5, 1.0)
    return jnp.where(blk < A_K_BLK, scale * math.log2(math.e), scale).astype(F32)


def _value_tiles_t(proj, blk, n_blk, bsz, seq, tile):
    v = proj[:, blk * LANES:(blk + n_blk) * LANES].reshape(bsz, seq // tile, tile, n_blk, LANES)
    return v.transpose(0, 3, 1, 4, 2)


def _gain_t(g, n_blk, width):
    return jnp.broadcast_to(g.reshape(n_blk * LANES, 1), (n_blk * LANES, width)).astype(F32)


def kernel(x, norm_pre_mix, w_in, lam_q1, lam_k1, lam_q2, lam_k2, subln_a, rel_bias, gn_b, gn_c, w_out,
           norm_post_mix, norm_pre_mlp, w_up, w_down, norm_post_mlp):
    bsz, seq, d = x.shape
    depth = w_in.shape[0]
    xf = x.reshape(bsz * seq, d)
    rc, rs1, rs2 = _rope_tables(seq)
    q_scale = _q_scale_columns()
    row = lambda v: v.reshape(1, -1)
    for l in range(depth):
        lam_init = 0.8 - 0.6 * math.exp(-0.3 * l)
        w_in_l = (w_in[l] * q_scale[None, :]).astype(BF16)
        proj = _inproj(xf, row(norm_pre_mix[l]), w_in_l, rc, rs1, rs2, seq)
        lam_vecs = jnp.stack([lam_q1[l], lam_k1[l], lam_q2[l], lam_k2[l]]).astype(F32)
        vt_a = _value_tiles_t(proj, A_V_BLK, A_HEADS, bsz, seq, A_TILE)
        ones_rows = jnp.zeros(vt_a.shape[:3] + (V_ROWS - LANES, A_TILE), BF16).at[..., 0, :].set(1.0)
        oa = _attn_a(proj, jnp.concatenate([vt_a, ones_rows], axis=3), lam_vecs,
                     _gain_t(subln_a[l], 1, A_TILE), lam_init, bsz, seq)
        ob = _attn_b(proj, _bias_tiles(rel_bias[l]), row(gn_b[l]), bsz, seq)
        oc = _attn_c(proj, _value_tiles_t(proj, C_V_BLK, C_HEADS // 2, bsz, seq, K_TILE),
                     _gain_t(gn_c[l], C_HEADS // 2, Q_TILE), bsz, seq)
        xf = _outmlp(xf, oa, ob, oc, w_out[l].astype(BF16), row(norm_post_mix[l]), row(norm_pre_mlp[l]),
                     w_up[l].astype(BF16), w_down[l].astype(BF16), row(norm_post_mlp[l]))
    return xf.reshape(bsz, seq, d)
```

```python
import functools
import math

import jax
import jax.numpy as jnp
from jax import lax
from jax.experimental import pallas as pl
from jax.experimental.pallas import tpu as pltpu

F32 = jnp.float32
BF16 = jnp.bfloat16

EPS = 1e-6
ROPE_THETA = 500000.0
HEAD_DIM = 64
ROPE_DIM = HEAD_DIM // 4
CHUNK = 64
LANES = 128
A_HEADS = 4
B_HEADS = 4
C_HEADS = 4
B_LEFT_CHUNKS = 8
REL_CLIP = 128
NEG = -1e30
M_RESET = -5e29
C_EXIT = -110.0

A_Q_BLK, A_K_BLK, A_V_BLK = 0, 4, 8
B_Q_BLK, B_K_BLK, B_V_BLK = 12, 14, 16
C_Q_BLK, C_K_BLK, C_V_BLK = 18, 20, 22
D_IN = 24 * LANES

Q_TILE = 256
K_TILE = 256
A_TILE = 512
A_UNROLL = 2
V_ROWS = LANES + 16
B_KEYS = 3 * K_TILE
B_TABLE_W = B_KEYS + 2 * K_TILE
B_V_ROWS = HEAD_DIM + 16

VMEM_LIMIT = 56 * 1024 * 1024


def _params(n_axes, vmem=VMEM_LIMIT):
    return pltpu.CompilerParams(dimension_semantics=("arbitrary",) * n_axes, vmem_limit_bytes=vmem)


def _rms(x, g):
    return x * lax.rsqrt(jnp.mean(x * x, axis=-1, keepdims=True) + EPS) * g


def _dot_t(a, b):
    return lax.dot_general(a, b, (((1,), (1,)), ((), ())), preferred_element_type=F32)


def _dot(a, b):
    return jnp.dot(a, b, preferred_element_type=F32)


def _split_heads(q):
    lane = lax.broadcasted_iota(jnp.int32, q.shape, 1)
    zero = jnp.zeros_like(q)
    return jnp.where(lane < HEAD_DIM, q, zero), jnp.where(lane >= HEAD_DIM, q, zero)


def _inproj_kernel(x_ref, g_ref, w_ref, rc_ref, rs1_ref, rs2_ref, o_ref, vta_ref, vtb_ref, vtc_ref, *, n_chunk):
    h = _rms(x_ref[...], g_ref[...]).astype(BF16)
    rc, rs1, rs2 = rc_ref[...], rs1_ref[...], rs2_ref[...]
    tm = x_ref.shape[0]
    pad_rows = V_ROWS - LANES
    ones_rows = jnp.where(lax.broadcasted_iota(jnp.int32, (pad_rows, tm), 0) == 0, 1.0, 0.0).astype(BF16)
    ones_rows_k = jnp.where(lax.broadcasted_iota(jnp.int32, (pad_rows, K_TILE), 0) == 0, 1.0, 0.0).astype(BF16)
    kt = tm // K_TILE
    for n0 in range(0, D_IN, n_chunk):
        y = _dot(h, w_ref[:, n0:n0 + n_chunk])
        for j in range(n_chunk // LANES):
            blk = n0 // LANES + j
            yj = y[:, j * LANES:(j + 1) * LANES]
            if blk < A_V_BLK:
                yj = yj * rc + pltpu.roll(yj, LANES - ROPE_DIM // 2, 1) * rs1 + pltpu.roll(yj, ROPE_DIM // 2, 1) * rs2
            o_ref[:, n0 + j * LANES:n0 + (j + 1) * LANES] = yj.astype(BF16)
            if A_V_BLK <= blk < B_Q_BLK:
                vta_ref[blk - A_V_BLK, 0:LANES, :] = yj.T.astype(BF16)
                vta_ref[blk - A_V_BLK, LANES:V_ROWS, :] = ones_rows
            if B_V_BLK <= blk < C_Q_BLK:
                yt = yj.T.astype(BF16)
                for t in range(kt):
                    for n in range(2):
                        r0 = n * B_V_ROWS
                        vtb_ref[t, blk - B_V_BLK, r0:r0 + HEAD_DIM, :] = (
                            yt[n * HEAD_DIM:(n + 1) * HEAD_DIM, t * K_TILE:(t + 1) * K_TILE])
                        vtb_ref[t, blk - B_V_BLK, r0 + HEAD_DIM:r0 + B_V_ROWS, :] = ones_rows_k
            if blk >= C_V_BLK:
                yt = yj.T.astype(BF16)
                for t in range(kt):
                    vtc_ref[t, blk - C_V_BLK] = yt[:, t * K_TILE:(t + 1) * K_TILE]


def _inproj(x, g, w, rc, rs1, rs2, seq, tm=A_TILE, n_chunk=512):
    m, d = x.shape
    pos_blocks = seq // tm
    rope_spec = pl.BlockSpec((tm, LANES), lambda i: (i % pos_blocks, 0))
    kt = tm // K_TILE
    return pl.pallas_call(
        functools.partial(_inproj_kernel, n_chunk=n_chunk),
        out_shape=(jax.ShapeDtypeStruct((m, D_IN), BF16),
                   jax.ShapeDtypeStruct((m // tm, A_HEADS, V_ROWS, tm), BF16),
                   jax.ShapeDtypeStruct((m // K_TILE, B_HEADS // 2, 2 * B_V_ROWS, K_TILE), BF16),
                   jax.ShapeDtypeStruct((m // K_TILE, C_HEADS // 2, LANES, K_TILE), BF16)),
        grid=(m // tm,),
        in_specs=[
            pl.BlockSpec((tm, d), lambda i: (i, 0)),
            pl.BlockSpec((1, d), lambda i: (0, 0)),
            pl.BlockSpec((d, D_IN), lambda i: (0, 0)),
            rope_spec, rope_spec, rope_spec,
        ],
        out_specs=(pl.BlockSpec((tm, D_IN), lambda i: (i, 0)),
                   pl.BlockSpec((None, A_HEADS, V_ROWS, tm), lambda i: (i, 0, 0, 0)),
                   pl.BlockSpec((kt, B_HEADS // 2, 2 * B_V_ROWS, K_TILE), lambda i: (i, 0, 0, 0)),
                   pl.BlockSpec((kt, C_HEADS // 2, LANES, K_TILE), lambda i: (i, 0, 0, 0))),
        compiler_params=_params(1),
        name="inproj",
    )(x, g, w, rc, rs1, rs2)


def _attn_a_kernel(q_ref, k_ref, vt_ref, lam_ref, gt_ref, o_ref, qx_ref, m_ref, mx_ref, s_ref, acc_ref,
                   *, lam_init, nq):
    tq = A_TILE
    n_chunks = tq // CHUNK
    fr = lax.broadcasted_iota(jnp.int32, (2 * tq, LANES), 0)
    fc = lax.broadcasted_iota(jnp.int32, (2 * tq, LANES), 1)
    q_feat = jnp.where((fc < n_chunks) & ((fr % tq) // CHUNK == fc), 1.0, 0.0).astype(BF16)
    for blk in range(nq):
        q0, q1 = _split_heads(q_ref[blk * tq:(blk + 1) * tq, :])
        qx_ref[blk, 0:tq, 0:LANES] = q0
        qx_ref[blk, tq:2 * tq, 0:LANES] = q1
        qx_ref[blk, :, LANES:2 * LANES] = q_feat
    kr = lax.broadcasted_iota(jnp.int32, (tq, LANES), 0)
    kc_ = lax.broadcasted_iota(jnp.int32, (tq, LANES), 1)
    k_feat = jnp.where((kc_ < n_chunks) & (kr // CHUNK > kc_), NEG, 0.0).astype(BF16)
    k_nofeat = jnp.zeros_like(k_feat)

    m_ref[...] = jnp.full(m_ref.shape, M_RESET, F32)
    mx_ref[...] = jnp.full(mx_ref.shape, NEG, F32)
    s_ref[1] = jnp.full(s_ref.shape[1:], NEG, F32)
    acc_ref[...] = jnp.zeros(acc_ref.shape, F32)

    def step(st, slot):
        qa, ka, qb, kb = st
        other = 1 - slot
        k = k_ref[pl.ds(pl.multiple_of(jnp.minimum(ka, nq - 1) * tq, tq), tq), :]
        kx = jnp.concatenate([k, jnp.where(ka == qa, k_feat, k_nofeat)], axis=1)
        s_new = _dot_t(kx, qx_ref[jnp.minimum(qa, nq - 1)])
        s_ref[slot] = s_new
        mx_ref[slot, 0:1, :] = jnp.max(s_new, axis=0, keepdims=True)
        m_old = jnp.where(kb == 0, M_RESET, m_ref[0:1, :])
        m_new = jnp.maximum(m_old, mx_ref[other, 0:1, :])
        m_ref[0:1, :] = m_new
        p = jnp.exp2(s_ref[other] - m_new).astype(BF16)
        blk = jnp.minimum(qb, nq)
        acc_ref[blk] = jnp.exp2(m_old - m_new) * acc_ref[blk] + _dot(vt_ref[jnp.clip(kb, 0, nq - 1)], p)
        last = ka == qa
        return (jnp.where(last, qa + 1, qa), jnp.where(last, 0, ka + 1), qa, ka)

    def steps(i, st):
        for u in range(A_UNROLL):
            st = step(st, u % 2)
        return st

    n_steps = nq * (nq + 1) // 2 + 1
    zero, neg1 = jnp.int32(0), jnp.int32(-1)
    lax.fori_loop(0, -(-n_steps // A_UNROLL), steps, (zero, zero, zero, neg1))

    lv = lam_ref[...]
    lam = (jnp.exp(jnp.sum(lv[0:1] * lv[1:2], axis=-1, keepdims=True))
           - jnp.exp(jnp.sum(lv[2:3] * lv[3:4], axis=-1, keepdims=True)) + lam_init)
    for blk in range(nq):
        acc = acc_ref[blk]
        on = acc[0:LANES] * (1.0 / acc[LANES:LANES + 1])
        o = on[:, :tq] - lam * on[:, tq:]
        o = o * lax.rsqrt(jnp.mean(o * o, axis=0, keepdims=True) + EPS) * gt_ref[...] * (1.0 - lam_init)
        o_ref[blk * tq:(blk + 1) * tq, :] = o.T.astype(BF16)


def _attn_a(proj, vt, lam_vecs, subln_gt, lam_init, bsz, seq):
    nq = seq // A_TILE
    return pl.pallas_call(
        functools.partial(_attn_a_kernel, lam_init=lam_init, nq=nq),
        out_shape=jax.ShapeDtypeStruct((bsz * seq, A_HEADS * LANES), BF16),
        grid=(bsz, A_HEADS),
        in_specs=[
            pl.BlockSpec((seq, LANES), lambda b, h: (b, A_Q_BLK + h)),
            pl.BlockSpec((seq, LANES), lambda b, h: (b, A_K_BLK + h)),
            pl.BlockSpec((nq, None, V_ROWS, A_TILE), lambda b, h: (b, h, 0, 0)),
            pl.BlockSpec((4, HEAD_DIM), lambda b, h: (0, 0)),
            pl.BlockSpec((LANES, A_TILE), lambda b, h: (0, 0)),
        ],
        out_specs=pl.BlockSpec((seq, LANES), lambda b, h: (b, h)),
        scratch_shapes=[pltpu.VMEM((nq, 2 * A_TILE, 2 * LANES), BF16), pltpu.VMEM((8, 2 * A_TILE), F32),
                        pltpu.VMEM((2, 8, 2 * A_TILE), F32),
                        pltpu.VMEM((2, A_TILE, 2 * A_TILE), F32),
                        pltpu.VMEM((nq + 1, V_ROWS, 2 * A_TILE), F32)],
        compiler_params=_params(2),
        name="attn_a",
    )(proj, proj, vt, lam_vecs, subln_gt)


def _bias_kernel(rb_ref, o_ref):
    h = pl.program_id(0)
    log2e = math.log2(math.e)
    far = rb_ref[h, 2 * REL_CLIP] * log2e
    shape = (Q_TILE, B_TABLE_W)
    row = lax.broadcasted_iota(jnp.int32, shape, 0)
    col = lax.broadcasted_iota(jnp.int32, shape, 1)
    kc = col // CHUNK - row // CHUNK
    band = (kc >= 0) & (kc <= B_LEFT_CHUNKS)

    win = 3 * LANES
    wrow = lax.broadcasted_iota(jnp.int32, (CHUNK, win), 0)
    wcol = lax.broadcasted_iota(jnp.int32, (CHUNK, win), 1)
    idxs = tuple(jnp.clip(wrow - wcol + REL_CLIP + shift, -REL_CLIP, REL_CLIP) + REL_CLIP for shift in (0, CHUNK))

    def step(r, ws):
        val = rb_ref[h, r] * log2e
        return tuple(jnp.where(idx == r, val, w) for idx, w in zip(idxs, ws))

    init = tuple(jnp.full((CHUNK, win), far, F32) for _ in idxs)
    ws = lax.fori_loop(0, 2 * REL_CLIP, step, init)

    win_start = (3 * LANES, 3 * LANES, 4 * LANES, 4 * LANES)
    rows = []
    for c in range(Q_TILE // CHUNK):
        pieces = []
        for j in range(B_TABLE_W // LANES):
            lo = j * LANES
            bnd = band[c * CHUNK:(c + 1) * CHUNK, lo:lo + LANES]
            if win_start[c] <= lo < win_start[c] + win:
                off = lo - win_start[c]
                pieces.append(jnp.where(bnd, ws[c % 2][:, off:off + LANES], NEG))
            else:
                pieces.append(jnp.where(bnd, far, NEG))
        rows.append(jnp.concatenate(pieces, axis=1))
    table = jnp.concatenate(rows, axis=0)
    for v in range(3):
        c0 = (2 - v) * K_TILE
        o_ref[v] = table[:, c0:c0 + B_KEYS].T


def _bias_tiles(rel_bias):
    heads = rel_bias.shape[0]
    return pl.pallas_call(
        _bias_kernel,
        out_shape=jax.ShapeDtypeStruct((3, heads, B_KEYS, Q_TILE), F32),
        grid=(heads,),
        in_specs=[pl.BlockSpec(memory_space=pltpu.SMEM)],
        out_specs=pl.BlockSpec((3, None, B_KEYS, Q_TILE), lambda h: (0, h, 0, 0)),
        compiler_params=_params(1),
        name="bias_tiles",
    )(rel_bias)


def _attn_b_kernel(q_ref, k_ref, vt_ref, bias_ref, gt_ref, o_ref, s_ref, mx_ref, res_ref, *, nq):
    tq = Q_TILE
    s_ref[1] = jnp.zeros(s_ref.shape[1:], F32)
    mx_ref[...] = jnp.zeros(mx_ref.shape, F32)
    gt = gt_ref[...]

    def step(i, slot):
        other = 1 - slot
        ia = jnp.minimum(i, nq - 1)
        first_key = pl.multiple_of(jnp.maximum(ia - 2, 0) * K_TILE, K_TILE)
        qcat = jnp.concatenate(_split_heads(q_ref[pl.ds(pl.multiple_of(ia * tq, tq), tq), :]), axis=0)
        var = jnp.minimum(ia, 2)
        s = (_dot_t(k_ref[pl.ds(first_key, B_KEYS), :], qcat)
             + jnp.concatenate([bias_ref[var, 0], bias_ref[var, 1]], axis=1))
        s_ref[slot] = s
        mx_ref[slot, 0:1, :] = jnp.max(s, axis=0, keepdims=True)
        ib = i - 1
        jb = jnp.maximum(jnp.minimum(ib, nq - 1) - 2, 0)
        p = jnp.exp2(s_ref[other] - mx_ref[other, 0:1, :]).astype(BF16)
        outs = []
        for n in range(2):
            acc = sum(_dot(vt_ref[jb + t][n * B_V_ROWS:(n + 1) * B_V_ROWS, :],
                           p[t * K_TILE:(t + 1) * K_TILE, n * tq:(n + 1) * tq]) for t in range(B_KEYS // K_TILE))
            o = acc[0:HEAD_DIM] * (1.0 / acc[HEAD_DIM:HEAD_DIM + 1])
            rs = lax.rsqrt(jnp.mean(o * o, axis=0, keepdims=True) + EPS)
            outs.append(o * rs * gt[n * HEAD_DIM:(n + 1) * HEAD_DIM, :])
        res_ref[jnp.where(ib < 0, nq, jnp.minimum(ib, nq))] = jnp.concatenate(outs, axis=0).T.astype(BF16)

    def steps(it, carry):
        for u in range(2):
            step(2 * it + u, u)
        return carry

    lax.fori_loop(0, (nq + 2) // 2, steps, 0)
    for blk in range(nq):
        o_ref[blk * tq:(blk + 1) * tq, :] = res_ref[blk]


def _attn_b(proj, vt, bias_tiles, gn_t, bsz, seq):
    nq = seq // Q_TILE
    nk = seq // K_TILE
    pairs = B_HEADS // 2
    return pl.pallas_call(
        functools.partial(_attn_b_kernel, nq=nq),
        out_shape=jax.ShapeDtypeStruct((bsz * seq, pairs * LANES), BF16),
        grid=(bsz, pairs),
        in_specs=[
            pl.BlockSpec((seq, LANES), lambda b, h: (b, B_Q_BLK + h)),
            pl.BlockSpec((seq, LANES), lambda b, h: (b, B_K_BLK + h)),
            pl.BlockSpec((nk, None, 2 * B_V_ROWS, K_TILE), lambda b, h: (b, h, 0, 0)),
            pl.BlockSpec((3, 2, B_KEYS, Q_TILE), lambda b, h: (0, h, 0, 0)),
            pl.BlockSpec((LANES, Q_TILE), lambda b, h: (h, 0)),
        ],
        out_specs=pl.BlockSpec((seq, LANES), lambda b, h: (b, h)),
        scratch_shapes=[pltpu.VMEM((2, B_KEYS, 2 * Q_TILE), F32), pltpu.VMEM((2, 8, 2 * Q_TILE), F32),
                        pltpu.VMEM((nq + 1, Q_TILE, LANES), BF16)],
        compiler_params=_params(2),
        name="attn_b",
    )(proj, proj, vt, bias_tiles, gn_t)


def _softplus(z):
    return jnp.maximum(z, 0.0) + jnp.log1p(jnp.exp(-jnp.abs(z)))


def _split_bf16(x):
    hi = x.astype(BF16)
    return hi, (x - hi.astype(F32)).astype(BF16)


def _attn_c_kernel(q_ref, k_ref, vt_ref, gt_ref, o_ref):
    qi = pl.program_id(2)
    tq, tk = Q_TILE, K_TILE
    qcat = jnp.concatenate(_split_heads(q_ref[...]), axis=0)
    tri_r = lax.broadcasted_iota(jnp.int32, (tk, tk), 0)
    tri_c = lax.broadcasted_iota(jnp.int32, (tk, tk), 1)
    later = jnp.where(tri_c > tri_r, 1.0, 0.0).astype(BF16)

    def suffix_sums(stay):
        hi, lo = _split_bf16(stay)
        return _dot(later, hi) + _dot(later, lo)

    def values(vt, a):
        return tuple(_dot(vt[n * HEAD_DIM:(n + 1) * HEAD_DIM, :], a[:, n * tq:(n + 1) * tq]) for n in range(2))

    j0 = jnp.maximum(qi - 1, 0)
    z = _dot_t(k_ref[pl.ds(pl.multiple_of(j0 * tk, tk), 2 * tk), :], qcat)
    krow = lax.broadcasted_iota(jnp.int32, z.shape, 0)
    qcol = lax.broadcasted_iota(jnp.int32, z.shape, 1) % tq
    causal = krow < qcol + (qi - j0) * tk
    sp = _softplus(z)
    stay = jnp.where(causal, -sp, 0.0)
    rem_hi = suffix_sums(stay[tk:])
    rem_lo = suffix_sums(stay[:tk]) + jnp.sum(stay[tk:], axis=0, keepdims=True)
    a = jnp.where(causal, jnp.exp((z - sp) + jnp.concatenate([rem_lo, rem_hi], axis=0)), 0.0).astype(BF16)
    c = jnp.sum(stay, axis=0, keepdims=True)
    acc = tuple(x + y for x, y in zip(values(vt_ref[j0], a[:tk]), values(vt_ref[j0 + 1], a[tk:])))

    def live(state):
        t, c, _, _ = state
        return (t < j0) & (jnp.max(c) > C_EXIT)

    def sweep(state):
        t, c, acc0, acc1 = state
        kj = j0 - 1 - t
        z = _dot_t(k_ref[pl.ds(pl.multiple_of(kj * tk, tk), tk), :], qcat)
        sp = _softplus(z)
        a = jnp.exp((z - sp) + suffix_sums(-sp) + c).astype(BF16)
        d0, d1 = values(vt_ref[kj], a)
        return t + 1, c - jnp.sum(sp, axis=0, keepdims=True), acc0 + d0, acc1 + d1

    _, _, acc0, acc1 = lax.while_loop(live, sweep, (jnp.int32(0), c, acc[0], acc[1]))
    gt = gt_ref[...]
    outs = []
    for n, o in enumerate((acc0, acc1)):
        rs = lax.rsqrt(jnp.mean(o * o, axis=0, keepdims=True) + EPS)
        outs.append(o * rs * gt[n * HEAD_DIM:(n + 1) * HEAD_DIM, :])
    o_ref[...] = jnp.concatenate(outs, axis=0).T.astype(BF16)


def _attn_c(proj, vt, gn_t, bsz, seq):
    nq = seq // Q_TILE
    nk = seq // K_TILE
    pairs = C_HEADS // 2
    return pl.pallas_call(
        _attn_c_kernel,
        out_shape=jax.ShapeDtypeStruct((bsz * seq, pairs * LANES), BF16),
        grid=(bsz, pairs, nq),
        in_specs=[
            pl.BlockSpec((Q_TILE, LANES), lambda b, h, i: (b * nq + i, C_Q_BLK + h)),
            pl.BlockSpec((seq, LANES), lambda b, h, i: (b, C_K_BLK + h)),
            pl.BlockSpec((nk, None, LANES, K_TILE), lambda b, h, i: (b, h, 0, 0)),
            pl.BlockSpec((LANES, Q_TILE), lambda b, h, i: (h, 0)),
        ],
        out_specs=pl.BlockSpec((Q_TILE, LANES), lambda b, h, i: (b * nq + i, h)),
        compiler_params=_params(3),
        name="attn_c",
    )(proj, proj, vt, gn_t)


def _outmlp_kernel(x_ref, oa_ref, ob_ref, oc_ref, wo_ref, g1_ref, g2_ref, wu_ref, wd_ref, g3_ref, o_ref, acc_ref,
                   *, f_chunk):
    wa, wb = oa_ref.shape[1], ob_ref.shape[1]
    y = (_dot(oa_ref[...], wo_ref[0:wa, :]) + _dot(ob_ref[...], wo_ref[wa:wa + wb, :])
         + _dot(oc_ref[...], wo_ref[wa + wb:, :]))
    x1 = x_ref[...] + _rms(y, g1_ref[...])
    h = _rms(x1, g2_ref[...]).astype(BF16)
    d_ff = wu_ref.shape[1]
    for f0 in range(0, d_ff, f_chunk):
        u = jnp.maximum(_dot(h, wu_ref[:, f0:f0 + f_chunk]), 0.0)
        part = _dot((u * u).astype(BF16), wd_ref[f0:f0 + f_chunk, :])
        if f0 == 0:
            acc_ref[...] = part
        else:
            acc_ref[...] += part
    o_ref[...] = x1 + _rms(acc_ref[...], g3_ref[...])


def _outmlp(x, oa, ob, oc, w_out, g_post, g_pre, w_up, w_down, g_post2, tm=512, f_chunk=1024):
    m, d = x.shape
    d_ff = w_up.shape[1]
    row = lambda w: pl.BlockSpec((tm, w), lambda i: (i, 0))
    full = lambda a, b: pl.BlockSpec((a, b), lambda i: (0, 0), pipeline_mode=pl.Buffered(1))
    return pl.pallas_call(
        functools.partial(_outmlp_kernel, f_chunk=f_chunk),
        out_shape=jax.ShapeDtypeStruct((m, d), F32),
        grid=(m // tm,),
        in_specs=[
            row(d), row(oa.shape[1]), row(ob.shape[1]), row(oc.shape[1]),
            full(d, d), full(1, d), full(1, d), full(d, d_ff), full(d_ff, d), full(1, d),
        ],
        out_specs=row(d),
        scratch_shapes=[pltpu.VMEM((tm, d), F32)],
        compiler_params=_params(1),
        name="outproj_mlp",
    )(x, oa, ob, oc, w_out, g_post, g_pre, w_up, w_down, g_post2)


def _rope_tables(seq):
    pos = jnp.arange(seq, dtype=F32)
    inv_freq = ROPE_THETA ** (-jnp.arange(0, ROPE_DIM, 2, dtype=F32) / ROPE_DIM)
    ang = pos[:, None] * inv_freq[None, :]
    cos, sin = jnp.cos(ang), jnp.sin(ang)
    half = ROPE_DIM // 2
    d = jnp.arange(LANES) % HEAD_DIM
    f = d % half
    rc = jnp.where(d[None, :] < ROPE_DIM, cos[:, f], 1.0)
    rs1 = jnp.where(d[None, :] < half, -sin[:, f], 0.0)
    rs2 = jnp.where((d[None, :] >= half) & (d[None, :] < ROPE_DIM), sin[:, f], 0.0)
    return rc.astype(F32), rs1.astype(F32), rs2.astype(F32)


def _q_scale_columns():
    blk = jnp.arange(D_IN) // LANES
    is_q = (blk < A_K_BLK) | ((blk >= B_Q_BLK) & (blk < B_K_BLK)) | ((blk >= C_Q_BLK) & (blk < C_K_BLK))
    scale = jnp.where(is_q, HEAD_DIM ** -0.5, 1.0)
    exp2_groups = (blk < A_K_BLK) | ((blk >= B_Q_BLK) & (blk < B_K_BLK))
    return jnp.where(exp2_groups, scale * math.log2(math.e), scale).astype(F32)


def _gain_t(g, n_blk, width):
    return jnp.broadcast_to(g.reshape(n_blk * LANES, 1), (n_blk * LANES, width)).astype(F32)


def kernel(x, norm_pre_mix, w_in, lam_q1, lam_k1, lam_q2, lam_k2, subln_a, rel_bias, gn_b, gn_c, w_out,
           norm_post_mix, norm_pre_mlp, w_up, w_down, norm_post_mlp):
    bsz, seq, d = x.shape
    depth = w_in.shape[0]
    xf = x.reshape(bsz * seq, d)
    rc, rs1, rs2 = _rope_tables(seq)
    q_scale = _q_scale_columns()
    row = lambda v: v.reshape(1, -1)
    for l in range(depth):
        lam_init = 0.8 - 0.6 * math.exp(-0.3 * l)
        w_in_l = (w_in[l] * q_scale[None, :]).astype(BF16)
        proj, vt_a, vt_b, vt_c = _inproj(xf, row(norm_pre_mix[l]), w_in_l, rc, rs1, rs2, seq)
        lam_vecs = jnp.stack([lam_q1[l], lam_k1[l], lam_q2[l], lam_k2[l]]).astype(F32)
        oa = _attn_a(proj, vt_a, lam_vecs, _gain_t(subln_a[l], 1, A_TILE), lam_init, bsz, seq)
        ob = _attn_b(proj, vt_b, _bias_tiles(rel_bias[l]), _gain_t(gn_b[l], B_HEADS // 2, Q_TILE), bsz, seq)
        oc = _attn_c(proj, vt_c, _gain_t(gn_c[l], C_HEADS // 2, Q_TILE), bsz, seq)
        xf = _outmlp(xf, oa, ob, oc, w_out[l].astype(BF16), row(norm_post_mix[l]), row(norm_pre_mlp[l]),
                     w_up[l].astype(BF16), w_down[l].astype(BF16), row(norm_post_mlp[l]))
    return xf.reshape(bsz, seq, d)
```

```python
import functools
import math

import jax
import jax.numpy as jnp
from jax import lax
from jax.experimental import pallas as pl
from jax.experimental.pallas import tpu as pltpu

F32 = jnp.float32
BF16 = jnp.bfloat16

EPS = 1e-6
ROPE_THETA = 500000.0
HEAD_DIM = 64
ROPE_DIM = HEAD_DIM // 4
CHUNK = 64
LANES = 128
A_HEADS = 4
B_HEADS = 4
C_HEADS = 4
B_LEFT_CHUNKS = 8
REL_CLIP = 128
NEG = -1e30
M_RESET = -5e29
C_EXIT = -160.0

A_Q_BLK, A_K_BLK, A_V_BLK = 0, 4, 8
B_Q_BLK, B_K_BLK, B_V_BLK = 12, 14, 16
C_Q_BLK, C_K_BLK, C_V_BLK = 18, 20, 22
D_IN = 24 * LANES

Q_TILE = 256
K_TILE = 256
A_TILE = 512
A_UNROLL = 4
V_ROWS = LANES + 16
B_KEYS = 3 * K_TILE
B_TABLE_W = B_KEYS + 2 * K_TILE
B_V_ROWS = HEAD_DIM + 16

VMEM_LIMIT = 56 * 1024 * 1024


def _params(n_axes, vmem=VMEM_LIMIT):
    return pltpu.CompilerParams(dimension_semantics=("arbitrary",) * n_axes, vmem_limit_bytes=vmem)


def _rms(x, g):
    return x * lax.rsqrt(jnp.mean(x * x, axis=-1, keepdims=True) + EPS) * g


def _dot_t(a, b):
    return lax.dot_general(a, b, (((1,), (1,)), ((), ())), preferred_element_type=F32)


def _dot(a, b):
    return jnp.dot(a, b, preferred_element_type=F32)


def _split_heads(q):
    lane = lax.broadcasted_iota(jnp.int32, q.shape, 1)
    zero = jnp.zeros_like(q)
    return jnp.where(lane < HEAD_DIM, q, zero), jnp.where(lane >= HEAD_DIM, q, zero)


def _inproj_kernel(x_ref, g_ref, w_ref, rc_ref, rs1_ref, rs2_ref, o_ref, vta_ref, vtb_ref, vtc_ref, *, n_chunk):
    h = _rms(x_ref[...], g_ref[...]).astype(BF16)
    rc, rs1, rs2 = rc_ref[...], rs1_ref[...], rs2_ref[...]
    tm = x_ref.shape[0]
    pad_rows = V_ROWS - LANES
    ones_rows = jnp.where(lax.broadcasted_iota(jnp.int32, (pad_rows, tm), 0) == 0, 1.0, 0.0).astype(BF16)
    ones_rows_k = jnp.where(lax.broadcasted_iota(jnp.int32, (pad_rows, K_TILE), 0) == 0, 1.0, 0.0).astype(BF16)
    kt = tm // K_TILE
    for n0 in range(0, D_IN, n_chunk):
        y = _dot(h, w_ref[:, n0:n0 + n_chunk])
        for j in range(n_chunk // LANES):
            blk = n0 // LANES + j
            yj = y[:, j * LANES:(j + 1) * LANES]
            if blk < A_V_BLK:
                yj = yj * rc + pltpu.roll(yj, LANES - ROPE_DIM // 2, 1) * rs1 + pltpu.roll(yj, ROPE_DIM // 2, 1) * rs2
            o_ref[:, n0 + j * LANES:n0 + (j + 1) * LANES] = yj.astype(BF16)
            if A_V_BLK <= blk < B_Q_BLK:
                vta_ref[blk - A_V_BLK, 0:LANES, :] = yj.T.astype(BF16)
                vta_ref[blk - A_V_BLK, LANES:V_ROWS, :] = ones_rows
            if B_V_BLK <= blk < C_Q_BLK:
                yt = yj.T.astype(BF16)
                for t in range(kt):
                    for n in range(2):
                        r0 = n * B_V_ROWS
                        vtb_ref[t, blk - B_V_BLK, r0:r0 + HEAD_DIM, :] = (
                            yt[n * HEAD_DIM:(n + 1) * HEAD_DIM, t * K_TILE:(t + 1) * K_TILE])
                        vtb_ref[t, blk - B_V_BLK, r0 + HEAD_DIM:r0 + B_V_ROWS, :] = ones_rows_k
            if blk >= C_V_BLK:
                yt = yj.T.astype(BF16)
                for t in range(kt):
                    vtc_ref[t, blk - C_V_BLK] = yt[:, t * K_TILE:(t + 1) * K_TILE]


def _inproj(x, g, w, rc, rs1, rs2, seq, tm=A_TILE, n_chunk=512):
    m, d = x.shape
    pos_blocks = seq // tm
    rope_spec = pl.BlockSpec((tm, LANES), lambda i: (i % pos_blocks, 0))
    kt = tm // K_TILE
    return pl.pallas_call(
        functools.partial(_inproj_kernel, n_chunk=n_chunk),
        out_shape=(jax.ShapeDtypeStruct((m, D_IN), BF16),
                   jax.ShapeDtypeStruct((m // tm, A_HEADS, V_ROWS, tm), BF16),
                   jax.ShapeDtypeStruct((m // K_TILE, B_HEADS // 2, 2 * B_V_ROWS, K_TILE), BF16),
                   jax.ShapeDtypeStruct((m // K_TILE, C_HEADS // 2, LANES, K_TILE), BF16)),
        grid=(m // tm,),
        in_specs=[
            pl.BlockSpec((tm, d), lambda i: (i, 0)),
            pl.BlockSpec((1, d), lambda i: (0, 0)),
            pl.BlockSpec((d, D_IN), lambda i: (0, 0)),
            rope_spec, rope_spec, rope_spec,
        ],
        out_specs=(pl.BlockSpec((tm, D_IN), lambda i: (i, 0)),
                   pl.BlockSpec((None, A_HEADS, V_ROWS, tm), lambda i: (i, 0, 0, 0)),
                   pl.BlockSpec((kt, B_HEADS // 2, 2 * B_V_ROWS, K_TILE), lambda i: (i, 0, 0, 0)),
                   pl.BlockSpec((kt, C_HEADS // 2, LANES, K_TILE), lambda i: (i, 0, 0, 0))),
        compiler_params=_params(1),
        name="inproj",
    )(x, g, w, rc, rs1, rs2)


def _attn_a_kernel(q_ref, k_ref, vt_ref, lam_ref, gt_ref, o_ref, qx_ref, m_ref, mx_ref, s_ref, acc_ref,
                   *, lam_init, nq):
    tq = A_TILE
    n_chunks = tq // CHUNK
    fr = lax.broadcasted_iota(jnp.int32, (2 * tq, LANES), 0)
    fc = lax.broadcasted_iota(jnp.int32, (2 * tq, LANES), 1)
    q_feat = jnp.where((fc < n_chunks) & ((fr % tq) // CHUNK == fc), 1.0, 0.0).astype(BF16)
    for blk in range(nq):
        q0, q1 = _split_heads(q_ref[blk * tq:(blk + 1) * tq, :])
        qx_ref[blk, 0:tq, 0:LANES] = q0
        qx_ref[blk, tq:2 * tq, 0:LANES] = q1
        qx_ref[blk, :, LANES:2 * LANES] = q_feat
    kr = lax.broadcasted_iota(jnp.int32, (tq, LANES), 0)
    kc_ = lax.broadcasted_iota(jnp.int32, (tq, LANES), 1)
    k_feat = jnp.where((kc_ < n_chunks) & (kr // CHUNK > kc_), NEG, 0.0).astype(BF16)
    k_nofeat = jnp.zeros_like(k_feat)

    m_ref[...] = jnp.full(m_ref.shape, M_RESET, F32)
    mx_ref[...] = jnp.full(mx_ref.shape, NEG, F32)
    s_ref[1] = jnp.full(s_ref.shape[1:], NEG, F32)
    acc_ref[...] = jnp.zeros(acc_ref.shape, F32)

    def step(st, slot):
        qa, ka, qb, kb = st
        other = 1 - slot
        k = k_ref[pl.ds(pl.multiple_of(jnp.minimum(ka, nq - 1) * tq, tq), tq), :]
        kx = jnp.concatenate([k, jnp.where(ka == qa, k_feat, k_nofeat)], axis=1)
        s_new = _dot_t(kx, qx_ref[jnp.minimum(qa, nq - 1)])
        s_ref[slot] = s_new
        mx_ref[slot, 0:1, :] = jnp.max(s_new, axis=0, keepdims=True)
        m_old = jnp.where(kb == 0, M_RESET, m_ref[0:1, :])
        m_new = jnp.maximum(m_old, mx_ref[other, 0:1, :])
        m_ref[0:1, :] = m_new
        p = jnp.exp2(s_ref[other] - m_new).astype(BF16)
        blk = jnp.minimum(qb, nq)
        acc_ref[blk] = jnp.exp2(m_old - m_new) * acc_ref[blk] + _dot(vt_ref[jnp.clip(kb, 0, nq - 1)], p)
        last = ka == qa
        return (jnp.where(last, qa + 1, qa), jnp.where(last, 0, ka + 1), qa, ka)

    def steps(i, st):
        for u in range(A_UNROLL):
            st = step(st, u % 2)
        return st

    n_steps = nq * (nq + 1) // 2 + 1
    zero, neg1 = jnp.int32(0), jnp.int32(-1)
    lax.fori_loop(0, -(-n_steps // A_UNROLL), steps, (zero, zero, zero, neg1))

    lv = lam_ref[...]
    lam = (jnp.exp(jnp.sum(lv[0:1] * lv[1:2], axis=-1, keepdims=True))
           - jnp.exp(jnp.sum(lv[2:3] * lv[3:4], axis=-1, keepdims=True)) + lam_init)
    for blk in range(nq):
        acc = acc_ref[blk]
        on = acc[0:LANES] * (1.0 / acc[LANES:LANES + 1])
        o = on[:, :tq] - lam * on[:, tq:]
        o = o * lax.rsqrt(jnp.mean(o * o, axis=0, keepdims=True) + EPS) * gt_ref[...] * (1.0 - lam_init)
        o_ref[blk * tq:(blk + 1) * tq, :] = o.T.astype(BF16)


def _attn_a(proj, vt, lam_vecs, subln_gt, lam_init, bsz, seq):
    nq = seq // A_TILE
    return pl.pallas_call(
        functools.partial(_attn_a_kernel, lam_init=lam_init, nq=nq),
        out_shape=jax.ShapeDtypeStruct((bsz * seq, A_HEADS * LANES), BF16),
        grid=(bsz, A_HEADS),
        in_specs=[
            pl.BlockSpec((seq, LANES), lambda b, h: (b, A_Q_BLK + h)),
            pl.BlockSpec((seq, LANES), lambda b, h: (b, A_K_BLK + h)),
            pl.BlockSpec((nq, None, V_ROWS, A_TILE), lambda b, h: (b, h, 0, 0)),
            pl.BlockSpec((4, HEAD_DIM), lambda b, h: (0, 0)),
            pl.BlockSpec((LANES, A_TILE), lambda b, h: (0, 0)),
        ],
        out_specs=pl.BlockSpec((seq, LANES), lambda b, h: (b, h)),
        scratch_shapes=[pltpu.VMEM((nq, 2 * A_TILE, 2 * LANES), BF16), pltpu.VMEM((8, 2 * A_TILE), F32),
                        pltpu.VMEM((2, 8, 2 * A_TILE), F32),
                        pltpu.VMEM((2, A_TILE, 2 * A_TILE), F32),
                        pltpu.VMEM((nq + 1, V_ROWS, 2 * A_TILE), F32)],
        compiler_params=_params(2),
        name="attn_a",
    )(proj, proj, vt, lam_vecs, subln_gt)


def _bias_kernel(rb_ref, o_ref):
    h = pl.program_id(0)
    log2e = math.log2(math.e)
    far = rb_ref[h, 2 * REL_CLIP] * log2e
    shape = (Q_TILE, B_TABLE_W)
    row = lax.broadcasted_iota(jnp.int32, shape, 0)
    col = lax.broadcasted_iota(jnp.int32, shape, 1)
    kc = col // CHUNK - row // CHUNK
    band = (kc >= 0) & (kc <= B_LEFT_CHUNKS)

    win = 3 * LANES
    wrow = lax.broadcasted_iota(jnp.int32, (CHUNK, win), 0)
    wcol = lax.broadcasted_iota(jnp.int32, (CHUNK, win), 1)
    idxs = tuple(jnp.clip(wrow - wcol + REL_CLIP + shift, -REL_CLIP, REL_CLIP) + REL_CLIP for shift in (0, CHUNK))

    def step(r, ws):
        val = rb_ref[h, r] * log2e
        return tuple(jnp.where(idx == r, val, w) for idx, w in zip(idxs, ws))

    init = tuple(jnp.full((CHUNK, win), far, F32) for _ in idxs)
    ws = lax.fori_loop(REL_CLIP - CHUNK + 1, 2 * REL_CLIP, step, init)

    win_start = (3 * LANES, 3 * LANES, 4 * LANES, 4 * LANES)
    rows = []
    for c in range(Q_TILE // CHUNK):
        pieces = []
        for j in range(B_TABLE_W // LANES):
            lo = j * LANES
            bnd = band[c * CHUNK:(c + 1) * CHUNK, lo:lo + LANES]
            if win_start[c] <= lo < win_start[c] + win:
                off = lo - win_start[c]
                pieces.append(jnp.where(bnd, ws[c % 2][:, off:off + LANES], NEG))
            else:
                pieces.append(jnp.where(bnd, far, NEG))
        rows.append(jnp.concatenate(pieces, axis=1))
    table = jnp.concatenate(rows, axis=0)
    for v in range(3):
        c0 = (2 - v) * K_TILE
        o_ref[v] = table[:, c0:c0 + B_KEYS].T


def _bias_tiles(rel_bias):
    heads = rel_bias.shape[0]
    return pl.pallas_call(
        _bias_kernel,
        out_shape=jax.ShapeDtypeStruct((3, heads, B_KEYS, Q_TILE), F32),
        grid=(heads,),
        in_specs=[pl.BlockSpec(memory_space=pltpu.SMEM)],
        out_specs=pl.BlockSpec((3, None, B_KEYS, Q_TILE), lambda h: (0, h, 0, 0)),
        compiler_params=_params(1),
        name="bias_tiles",
    )(rel_bias)


def _attn_b_kernel(q_ref, k_ref, vt_ref, bias_ref, gt_ref, o_ref, s_ref, mx_ref, res_ref, *, nq):
    tq = Q_TILE
    s_ref[1] = jnp.zeros(s_ref.shape[1:], F32)
    mx_ref[...] = jnp.zeros(mx_ref.shape, F32)
    gt = gt_ref[...]

    def step(i, slot):
        other = 1 - slot
        ia = jnp.minimum(i, nq - 1)
        first_key = pl.multiple_of(jnp.maximum(ia - 2, 0) * K_TILE, K_TILE)
        qcat = jnp.concatenate(_split_heads(q_ref[pl.ds(pl.multiple_of(ia * tq, tq), tq), :]), axis=0)
        var = jnp.minimum(ia, 2)
        s = (_dot_t(k_ref[pl.ds(first_key, B_KEYS), :], qcat)
             + jnp.concatenate([bias_ref[var, 0], bias_ref[var, 1]], axis=1))
        s_ref[slot] = s
        mx_ref[slot, 0:1, :] = jnp.max(s, axis=0, keepdims=True)
        ib = i - 1
        jb = jnp.maximum(jnp.minimum(ib, nq - 1) - 2, 0)
        p = jnp.exp2(s_ref[other] - mx_ref[other, 0:1, :]).astype(BF16)
        outs = []
        for n in range(2):
            acc = sum(_dot(vt_ref[jb + t][n * B_V_ROWS:(n + 1) * B_V_ROWS, :],
                           p[t * K_TILE:(t + 1) * K_TILE, n * tq:(n + 1) * tq]) for t in range(B_KEYS // K_TILE))
            o = acc[0:HEAD_DIM] * (1.0 / acc[HEAD_DIM:HEAD_DIM + 1])
            rs = lax.rsqrt(jnp.mean(o * o, axis=0, keepdims=True) + EPS)
            outs.append(o * rs * gt[n * HEAD_DIM:(n + 1) * HEAD_DIM, :])
        res_ref[jnp.where(ib < 0, nq, jnp.minimum(ib, nq))] = jnp.concatenate(outs, axis=0).T.astype(BF16)

    def steps(it, carry):
        for u in range(2):
            step(2 * it + u, u)
        return carry

    lax.fori_loop(0, (nq + 2) // 2, steps, 0)
    for blk in range(nq):
        o_ref[blk * tq:(blk + 1) * tq, :] = res_ref[blk]


def _attn_b(proj, vt, bias_tiles, gn_t, bsz, seq):
    nq = seq // Q_TILE
    nk = seq // K_TILE
    pairs = B_HEADS // 2
    return pl.pallas_call(
        functools.partial(_attn_b_kernel, nq=nq),
        out_shape=jax.ShapeDtypeStruct((bsz * seq, pairs * LANES), BF16),
        grid=(bsz, pairs),
        in_specs=[
            pl.BlockSpec((seq, LANES), lambda b, h: (b, B_Q_BLK + h)),
            pl.BlockSpec((seq, LANES), lambda b, h: (b, B_K_BLK + h)),
            pl.BlockSpec((nk, None, 2 * B_V_ROWS, K_TILE), lambda b, h: (b, h, 0, 0)),
            pl.BlockSpec((3, 2, B_KEYS, Q_TILE), lambda b, h: (0, h, 0, 0)),
            pl.BlockSpec((LANES, Q_TILE), lambda b, h: (h, 0)),
        ],
        out_specs=pl.BlockSpec((seq, LANES), lambda b, h: (b, h)),
        scratch_shapes=[pltpu.VMEM((2, B_KEYS, 2 * Q_TILE), F32), pltpu.VMEM((2, 8, 2 * Q_TILE), F32),
                        pltpu.VMEM((nq + 1, Q_TILE, LANES), BF16)],
        compiler_params=_params(2),
        name="attn_b",
    )(proj, proj, vt, bias_tiles, gn_t)


def _softplus2(z):
    return jnp.maximum(z, 0.0) + jnp.log2(1.0 + jnp.exp2(-jnp.abs(z)))


def _attn_c_kernel(q_ref, k_ref, vt_ref, mask_ref, gt_ref, o_ref, *, n_sub):
    tq, tk = Q_TILE, K_TILE
    tri_r = lax.broadcasted_iota(jnp.int32, (tk, tk), 0)
    tri_c = lax.broadcasted_iota(jnp.int32, (tk, tk), 1)
    later = jnp.where(tri_c > tri_r, 1.0, 0.0).astype(BF16)

    def suffix_sums(x):
        return _dot(later, x.astype(BF16))

    def values(vt, a):
        return tuple(_dot(vt[n * HEAD_DIM:(n + 1) * HEAD_DIM, :], a[:, n * tq:(n + 1) * tq]) for n in range(2))

    subs = range(n_sub)
    qis = [pl.program_id(2) * n_sub + u for u in subs]
    qcats = [jnp.concatenate(_split_heads(q_ref[u * tq:(u + 1) * tq, :]), axis=0) for u in subs]

    j0s = [jnp.maximum(qi - 1, 0) for qi in qis]
    zs = [_dot_t(k_ref[pl.ds(pl.multiple_of(j0 * tk, tk), 2 * tk), :], qcat) + mask_ref[jnp.minimum(qi, 1)]
          for j0, qi, qcat in zip(j0s, qis, qcats)]
    sps = [_softplus2(z) for z in zs]
    gone = [jnp.concatenate([suffix_sums(sp[:tk]) + jnp.sum(sp[tk:], axis=0, keepdims=True),
                             suffix_sums(sp[tk:])], axis=0) for sp in sps]
    a_s = [jnp.exp2((z - sp) - g).astype(BF16) for z, sp, g in zip(zs, sps, gone)]
    cs = [jnp.sum(sp, axis=0, keepdims=True) for sp in sps]
    accs = [tuple(x + y for x, y in zip(values(vt_ref[j0], a[:tk]), values(vt_ref[j0 + 1], a[tk:])))
            for j0, a in zip(j0s, a_s)]

    gt = gt_ref[...]
    for u in subs:
        j0, qcat = j0s[u], qcats[u]

        def live(state):
            t, c, _, _ = state
            return (t < j0) & (jnp.min(c) < -C_EXIT)

        def sweep(state):
            t, c, acc0, acc1 = state
            kj = j0 - 1 - t
            z = _dot_t(k_ref[pl.ds(pl.multiple_of(kj * tk, tk), tk), :], qcat)
            sp = _softplus2(z)
            a = jnp.exp2((z - sp) - suffix_sums(sp) - c).astype(BF16)
            d0, d1 = values(vt_ref[kj], a)
            return t + 1, c + jnp.sum(sp, axis=0, keepdims=True), acc0 + d0, acc1 + d1

        _, _, acc0, acc1 = lax.while_loop(live, sweep, (jnp.int32(0), cs[u], accs[u][0], accs[u][1]))
        outs = []
        for n, o in enumerate((acc0, acc1)):
            rs = lax.rsqrt(jnp.mean(o * o, axis=0, keepdims=True) + EPS)
            outs.append(o * rs * gt[n * HEAD_DIM:(n + 1) * HEAD_DIM, :])
        o_ref[u * tq:(u + 1) * tq, :] = jnp.concatenate(outs, axis=0).T.astype(BF16)


def _causal_bias_c():
    key = jnp.arange(2 * K_TILE)[:, None]
    query = jnp.arange(2 * Q_TILE)[None, :] % Q_TILE
    return jnp.stack([jnp.where(key < query + off, 0.0, NEG) for off in (0, K_TILE)]).astype(F32)


def _attn_c(proj, vt, gn_t, bsz, seq, n_sub=4):
    nq = seq // (Q_TILE * n_sub)
    nk = seq // K_TILE
    pairs = C_HEADS // 2
    return pl.pallas_call(
        functools.partial(_attn_c_kernel, n_sub=n_sub),
        out_shape=jax.ShapeDtypeStruct((bsz * seq, pairs * LANES), BF16),
        grid=(bsz, pairs, nq),
        in_specs=[
            pl.BlockSpec((n_sub * Q_TILE, LANES), lambda b, h, i: (b * nq + i, C_Q_BLK + h)),
            pl.BlockSpec((seq, LANES), lambda b, h, i: (b, C_K_BLK + h)),
            pl.BlockSpec((nk, None, LANES, K_TILE), lambda b, h, i: (b, h, 0, 0)),
            pl.BlockSpec((2, 2 * K_TILE, 2 * Q_TILE), lambda b, h, i: (0, 0, 0)),
            pl.BlockSpec((LANES, Q_TILE), lambda b, h, i: (h, 0)),
        ],
        out_specs=pl.BlockSpec((n_sub * Q_TILE, LANES), lambda b, h, i: (b * nq + i, h)),
        compiler_params=_params(3),
        name="attn_c",
    )(proj, proj, vt, _causal_bias_c(), gn_t)


def _outmlp_kernel(x_ref, oa_ref, ob_ref, oc_ref, wo_ref, g1_ref, g2_ref, wu_ref, wd_ref, g3_ref, o_ref, acc_ref,
                   *, f_chunk):
    wa, wb = oa_ref.shape[1], ob_ref.shape[1]
    y = (_dot(oa_ref[...], wo_ref[0:wa, :]) + _dot(ob_ref[...], wo_ref[wa:wa + wb, :])
         + _dot(oc_ref[...], wo_ref[wa + wb:, :]))
    x1 = x_ref[...] + _rms(y, g1_ref[...])
    h = _rms(x1, g2_ref[...]).astype(BF16)
    d_ff = wu_ref.shape[1]
    for f0 in range(0, d_ff, f_chunk):
        u = jnp.maximum(_dot(h, wu_ref[:, f0:f0 + f_chunk]), 0.0)
        part = _dot((u * u).astype(BF16), wd_ref[f0:f0 + f_chunk, :])
        if f0 == 0:
            acc_ref[...] = part
        else:
            acc_ref[...] += part
    o_ref[...] = x1 + _rms(acc_ref[...], g3_ref[...])


def _outmlp(x, oa, ob, oc, w_out, g_post, g_pre, w_up, w_down, g_post2, tm=512, f_chunk=1024):
    m, d = x.shape
    d_ff = w_up.shape[1]
    row = lambda w: pl.BlockSpec((tm, w), lambda i: (i, 0))
    full = lambda a, b: pl.BlockSpec((a, b), lambda i: (0, 0), pipeline_mode=pl.Buffered(1))
    return pl.pallas_call(
        functools.partial(_outmlp_kernel, f_chunk=f_chunk),
        out_shape=jax.ShapeDtypeStruct((m, d), F32),
        grid=(m // tm,),
        in_specs=[
            row(d), row(oa.shape[1]), row(ob.shape[1]), row(oc.shape[1]),
            full(d, d), full(1, d), full(1, d), full(d, d_ff), full(d_ff, d), full(1, d),
        ],
        out_specs=row(d),
        scratch_shapes=[pltpu.VMEM((tm, d), F32)],
        compiler_params=_params(1),
        name="outproj_mlp",
    )(x, oa, ob, oc, w_out, g_post, g_pre, w_up, w_down, g_post2)


def _rope_tables(seq):
    pos = jnp.arange(seq, dtype=F32)
    inv_freq = ROPE_THETA ** (-jnp.arange(0, ROPE_DIM, 2, dtype=F32) / ROPE_DIM)
    ang = pos[:, None] * inv_freq[None, :]
    cos, sin = jnp.cos(ang), jnp.sin(ang)
    half = ROPE_DIM // 2
    d = jnp.arange(LANES) % HEAD_DIM
    f = d % half
    rc = jnp.where(d[None, :] < ROPE_DIM, cos[:, f], 1.0)
    rs1 = jnp.where(d[None, :] < half, -sin[:, f], 0.0)
    rs2 = jnp.where((d[None, :] >= half) & (d[None, :] < ROPE_DIM), sin[:, f], 0.0)
    return rc.astype(F32), rs1.astype(F32), rs2.astype(F32)


def _q_scale_columns():
    blk = jnp.arange(D_IN) // LANES
    is_q = (blk < A_K_BLK) | ((blk >= B_Q_BLK) & (blk < B_K_BLK)) | ((blk >= C_Q_BLK) & (blk < C_K_BLK))
    return jnp.where(is_q, HEAD_DIM ** -0.5 * math.log2(math.e), 1.0).astype(F32)


def _gain_t(g, n_blk, width):
    return jnp.broadcast_to(g.reshape(n_blk * LANES, 1), (n_blk * LANES, width)).astype(F32)


def kernel(x, norm_pre_mix, w_in, lam_q1, lam_k1, lam_q2, lam_k2, subln_a, rel_bias, gn_b, gn_c, w_out,
           norm_post_mix, norm_pre_mlp, w_up, w_down, norm_post_mlp):
    bsz, seq, d = x.shape
    depth = w_in.shape[0]
    xf = x.reshape(bsz * seq, d)
    rc, rs1, rs2 = _rope_tables(seq)
    q_scale = _q_scale_columns()
    row = lambda v: v.reshape(1, -1)
    for l in range(depth):
        lam_init = 0.8 - 0.6 * math.exp(-0.3 * l)
        w_in_l = (w_in[l] * q_scale[None, :]).astype(BF16)
        proj, vt_a, vt_b, vt_c = _inproj(xf, row(norm_pre_mix[l]), w_in_l, rc, rs1, rs2, seq)
        lam_vecs = jnp.stack([lam_q1[l], lam_k1[l], lam_q2[l], lam_k2[l]]).astype(F32)
        oa = _attn_a(proj, vt_a, lam_vecs, _gain_t(subln_a[l], 1, A_TILE), lam_init, bsz, seq)
        ob = _attn_b(proj, vt_b, _bias_tiles(rel_bias[l]), _gain_t(gn_b[l], B_HEADS // 2, Q_TILE), bsz, seq)
        oc = _attn_c(proj, vt_c, _gain_t(gn_c[l], C_HEADS // 2, Q_TILE), bsz, seq)
        xf = _outmlp(xf, oa, ob, oc, w_out[l].astype(BF16), row(norm_post_mix[l]), row(norm_pre_mlp[l]),
                     w_up[l].astype(BF16), w_down[l].astype(BF16), row(norm_post_mlp[l]))
    return xf.reshape(bsz, seq, d)
```

```python
import functools
import math

import jax
import jax.numpy as jnp
from jax import lax
from jax.experimental import pallas as pl
from jax.experimental.pallas import tpu as pltpu

F32 = jnp.float32
BF16 = jnp.bfloat16

EPS = 1e-6
ROPE_THETA = 500000.0
HEAD_DIM = 64
ROPE_DIM = HEAD_DIM // 4
CHUNK = 64
LANES = 128
A_HEADS = 4
B_HEADS = 4
C_HEADS = 4
B_LEFT_CHUNKS = 8
REL_CLIP = 128
NEG = -1e30
M_RESET = -5e29
C_EXIT = -160.0

A_Q_BLK, A_K_BLK, A_V_BLK = 0, 4, 8
B_Q_BLK, B_K_BLK, B_V_BLK = 12, 14, 16
C_Q_BLK, C_K_BLK, C_V_BLK = 18, 20, 22
D_IN = 24 * LANES

Q_TILE = 256
K_TILE = 256
A_TILE = 512
A_UNROLL = 4
B_UNROLL = 4
V_ROWS = LANES + 16
B_KEYS = 3 * K_TILE
B_TABLE_W = B_KEYS + 2 * K_TILE
B_V_ROWS = HEAD_DIM + 16

VMEM_LIMIT = 56 * 1024 * 1024


def _params(n_axes, vmem=VMEM_LIMIT):
    return pltpu.CompilerParams(dimension_semantics=("arbitrary",) * n_axes, vmem_limit_bytes=vmem)


def _rms(x, g):
    return x * lax.rsqrt(jnp.mean(x * x, axis=-1, keepdims=True) + EPS) * g


def _dot_t(a, b):
    return lax.dot_general(a, b, (((1,), (1,)), ((), ())), preferred_element_type=F32)


def _dot(a, b):
    return jnp.dot(a, b, preferred_element_type=F32)


def _split_heads(q):
    lane = lax.broadcasted_iota(jnp.int32, q.shape, 1)
    zero = jnp.zeros_like(q)
    return jnp.where(lane < HEAD_DIM, q, zero), jnp.where(lane >= HEAD_DIM, q, zero)


def _inproj_kernel(x_ref, g_ref, w_ref, rc_ref, rs1_ref, rs2_ref, o_ref, vta_ref, vtb_ref, vtc_ref, *, n_chunk):
    h = _rms(x_ref[...], g_ref[...]).astype(BF16)
    rc, rs1, rs2 = rc_ref[...], rs1_ref[...], rs2_ref[...]
    tm = x_ref.shape[0]
    pad_rows = V_ROWS - LANES
    ones_rows = jnp.where(lax.broadcasted_iota(jnp.int32, (pad_rows, tm), 0) == 0, 1.0, 0.0).astype(BF16)
    ones_rows_k = jnp.where(lax.broadcasted_iota(jnp.int32, (pad_rows, K_TILE), 0) == 0, 1.0, 0.0).astype(BF16)
    kt = tm // K_TILE
    for n0 in range(0, D_IN, n_chunk):
        y = _dot(h, w_ref[:, n0:n0 + n_chunk])
        for j in range(n_chunk // LANES):
            blk = n0 // LANES + j
            yj = y[:, j * LANES:(j + 1) * LANES]
            if blk < A_V_BLK:
                yj = yj * rc + pltpu.roll(yj, LANES - ROPE_DIM // 2, 1) * rs1 + pltpu.roll(yj, ROPE_DIM // 2, 1) * rs2
            o_ref[:, n0 + j * LANES:n0 + (j + 1) * LANES] = yj.astype(BF16)
            if A_V_BLK <= blk < B_Q_BLK:
                vta_ref[blk - A_V_BLK, 0:LANES, :] = yj.T.astype(BF16)
                vta_ref[blk - A_V_BLK, LANES:V_ROWS, :] = ones_rows
            if B_V_BLK <= blk < C_Q_BLK:
                yt = yj.T.astype(BF16)
                for t in range(kt):
                    for n in range(2):
                        r0 = n * B_V_ROWS
                        vtb_ref[t, blk - B_V_BLK, r0:r0 + HEAD_DIM, :] = (
                            yt[n * HEAD_DIM:(n + 1) * HEAD_DIM, t * K_TILE:(t + 1) * K_TILE])
                        vtb_ref[t, blk - B_V_BLK, r0 + HEAD_DIM:r0 + B_V_ROWS, :] = ones_rows_k
            if blk >= C_V_BLK:
                yt = yj.T.astype(BF16)
                for t in range(kt):
                    vtc_ref[t, blk - C_V_BLK] = yt[:, t * K_TILE:(t + 1) * K_TILE]


def _inproj(x, g, w, rc, rs1, rs2, seq, tm=A_TILE, n_chunk=512):
    m, d = x.shape
    pos_blocks = seq // tm
    rope_spec = pl.BlockSpec((tm, LANES), lambda i: (i % pos_blocks, 0))
    kt = tm // K_TILE
    return pl.pallas_call(
        functools.partial(_inproj_kernel, n_chunk=n_chunk),
        out_shape=(jax.ShapeDtypeStruct((m, D_IN), BF16),
                   jax.ShapeDtypeStruct((m // tm, A_HEADS, V_ROWS, tm), BF16),
                   jax.ShapeDtypeStruct((m // K_TILE, B_HEADS // 2, 2 * B_V_ROWS, K_TILE), BF16),
                   jax.ShapeDtypeStruct((m // K_TILE, C_HEADS // 2, LANES, K_TILE), BF16)),
        grid=(m // tm,),
        in_specs=[
            pl.BlockSpec((tm, d), lambda i: (i, 0)),
            pl.BlockSpec((1, d), lambda i: (0, 0)),
            pl.BlockSpec((d, D_IN), lambda i: (0, 0)),
            rope_spec, rope_spec, rope_spec,
        ],
        out_specs=(pl.BlockSpec((tm, D_IN), lambda i: (i, 0)),
                   pl.BlockSpec((None, A_HEADS, V_ROWS, tm), lambda i: (i, 0, 0, 0)),
                   pl.BlockSpec((kt, B_HEADS // 2, 2 * B_V_ROWS, K_TILE), lambda i: (i, 0, 0, 0)),
                   pl.BlockSpec((kt, C_HEADS // 2, LANES, K_TILE), lambda i: (i, 0, 0, 0))),
        compiler_params=_params(1),
        name="inproj",
    )(x, g, w, rc, rs1, rs2)


def _attn_a_kernel(q_ref, k_ref, vt_ref, lam_ref, gt_ref, o_ref, qx_ref, m_ref, mx_ref, s_ref, acc_ref,
                   *, lam_init, nq):
    tq = A_TILE
    n_chunks = tq // CHUNK
    fr = lax.broadcasted_iota(jnp.int32, (2 * tq, LANES), 0)
    fc = lax.broadcasted_iota(jnp.int32, (2 * tq, LANES), 1)
    q_feat = jnp.where((fc < n_chunks) & ((fr % tq) // CHUNK == fc), 1.0, 0.0).astype(BF16)
    q_feat_t = q_feat.astype(F32).T.astype(BF16)
    for blk in range(nq):
        q0, q1 = _split_heads(q_ref[blk * tq:(blk + 1) * tq, :])
        qx_ref[blk, 0:LANES, 0:tq] = q0.astype(F32).T.astype(BF16)
        qx_ref[blk, 0:LANES, tq:2 * tq] = q1.astype(F32).T.astype(BF16)
        qx_ref[blk, LANES:2 * LANES, :] = q_feat_t
    kr = lax.broadcasted_iota(jnp.int32, (tq, LANES), 0)
    kc_ = lax.broadcasted_iota(jnp.int32, (tq, LANES), 1)
    k_feat = jnp.where((kc_ < n_chunks) & (kr // CHUNK > kc_), NEG, 0.0).astype(BF16)
    k_nofeat = jnp.zeros_like(k_feat)

    m_ref[...] = jnp.full(m_ref.shape, M_RESET, F32)
    mx_ref[...] = jnp.full(mx_ref.shape, NEG, F32)
    s_ref[1] = jnp.full(s_ref.shape[1:], NEG, F32)

    @pl.when((pl.program_id(0) == 0) & (pl.program_id(1) == 0))
    def _():
        acc_ref[...] = jnp.zeros(acc_ref.shape, F32)

    def score_stage(qa, ka, slot):
        k = k_ref[pl.ds(pl.multiple_of(ka * tq, tq), tq), :]
        kx = jnp.concatenate([k, jnp.where(ka == qa, k_feat, k_nofeat)], axis=1)
        s_new = _dot(kx, qx_ref[qa])
        s_ref[slot] = s_new
        mx_ref[slot, 0:1, :] = jnp.max(s_new, axis=0, keepdims=True)

    def value_stage(qb, kb, slot):
        m_old = jnp.where(kb == 0, M_RESET, m_ref[0:1, :])
        m_new = jnp.maximum(m_old, mx_ref[slot, 0:1, :])
        m_ref[0:1, :] = m_new
        p = jnp.exp2(s_ref[slot] - m_new).astype(BF16)
        acc_ref[qb] = jnp.exp2(m_old - m_new) * acc_ref[qb] + _dot(vt_ref[jnp.maximum(kb, 0)], p)

    def step(st, slot):
        qa, ka, qb, kb = st
        score_stage(qa, ka, slot)
        value_stage(qb, kb, 1 - slot)
        last = ka == qa
        return (jnp.where(last, qa + 1, qa), jnp.where(last, 0, ka + 1), qa, ka)

    def steps(i, st):
        for u in range(A_UNROLL):
            st = step(st, u % 2)
        return st

    n_tiles = nq * (nq + 1) // 2
    zero, neg1 = jnp.int32(0), jnp.int32(-1)
    st = lax.fori_loop(0, n_tiles // A_UNROLL, steps, (zero, zero, zero, neg1))
    for f in range(n_tiles // A_UNROLL * A_UNROLL, n_tiles):
        st = step(st, f % 2)
    value_stage(st[2], st[3], (n_tiles - 1) % 2)

    lv = lam_ref[...]
    lam = (jnp.exp(jnp.sum(lv[0:1] * lv[1:2], axis=-1, keepdims=True))
           - jnp.exp(jnp.sum(lv[2:3] * lv[3:4], axis=-1, keepdims=True)) + lam_init)
    for blk in range(nq):
        acc = acc_ref[blk]
        on = acc[0:LANES] * (1.0 / acc[LANES:LANES + 1])
        o = on[:, :tq] - lam * on[:, tq:]
        o = o * lax.rsqrt(jnp.mean(o * o, axis=0, keepdims=True) + EPS) * gt_ref[...] * (1.0 - lam_init)
        o_ref[blk * tq:(blk + 1) * tq, :] = o.T.astype(BF16)


def _attn_a(proj, vt, lam_vecs, subln_gt, lam_init, bsz, seq):
    nq = seq // A_TILE
    return pl.pallas_call(
        functools.partial(_attn_a_kernel, lam_init=lam_init, nq=nq),
        out_shape=jax.ShapeDtypeStruct((bsz * seq, A_HEADS * LANES), BF16),
        grid=(bsz, A_HEADS),
        in_specs=[
            pl.BlockSpec((seq, LANES), lambda b, h: (b, A_Q_BLK + h)),
            pl.BlockSpec((seq, LANES), lambda b, h: (b, A_K_BLK + h)),
            pl.BlockSpec((nq, None, V_ROWS, A_TILE), lambda b, h: (b, h, 0, 0)),
            pl.BlockSpec((4, HEAD_DIM), lambda b, h: (0, 0)),
            pl.BlockSpec((LANES, A_TILE), lambda b, h: (0, 0)),
        ],
        out_specs=pl.BlockSpec((seq, LANES), lambda b, h: (b, h)),
        scratch_shapes=[pltpu.VMEM((nq, 2 * LANES, 2 * A_TILE), BF16), pltpu.VMEM((8, 2 * A_TILE), F32),
                        pltpu.VMEM((2, 8, 2 * A_TILE), F32),
                        pltpu.VMEM((2, A_TILE, 2 * A_TILE), F32),
                        pltpu.VMEM((nq, V_ROWS, 2 * A_TILE), F32)],
        compiler_params=_params(2),
        name="attn_a",
    )(proj, proj, vt, lam_vecs, subln_gt)


def _bias_kernel(rb_ref, o_ref):
    h = pl.program_id(0)
    log2e = math.log2(math.e)
    far = rb_ref[h, 2 * REL_CLIP] * log2e
    shape = (Q_TILE, B_TABLE_W)
    row = lax.broadcasted_iota(jnp.int32, shape, 0)
    col = lax.broadcasted_iota(jnp.int32, shape, 1)
    kc = col // CHUNK - row // CHUNK
    band = (kc >= 0) & (kc <= B_LEFT_CHUNKS)

    win = 3 * LANES
    wrow = lax.broadcasted_iota(jnp.int32, (CHUNK, win), 0)
    wcol = lax.broadcasted_iota(jnp.int32, (CHUNK, win), 1)
    idxs = tuple(jnp.clip(wrow - wcol + REL_CLIP + shift, -REL_CLIP, REL_CLIP) + REL_CLIP for shift in (0, CHUNK))

    def step(r, ws):
        val = rb_ref[h, r] * log2e
        return tuple(jnp.where(idx == r, val, w) for idx, w in zip(idxs, ws))

    init = tuple(jnp.full((CHUNK, win), far, F32) for _ in idxs)
    ws = lax.fori_loop(REL_CLIP - CHUNK + 1, 2 * REL_CLIP, step, init)

    win_start = (3 * LANES, 3 * LANES, 4 * LANES, 4 * LANES)
    rows = []
    for c in range(Q_TILE // CHUNK):
        pieces = []
        for j in range(B_TABLE_W // LANES):
            lo = j * LANES
            bnd = band[c * CHUNK:(c + 1) * CHUNK, lo:lo + LANES]
            if win_start[c] <= lo < win_start[c] + win:
                off = lo - win_start[c]
                pieces.append(jnp.where(bnd, ws[c % 2][:, off:off + LANES], NEG))
            else:
                pieces.append(jnp.where(bnd, far, NEG))
        rows.append(jnp.concatenate(pieces, axis=1))
    table = jnp.concatenate(rows, axis=0)
    for v in range(3):
        c0 = (2 - v) * K_TILE
        o_ref[v] = table[:, c0:c0 + B_KEYS].T


def _bias_tiles(rel_bias):
    heads = rel_bias.shape[0]
    return pl.pallas_call(
        _bias_kernel,
        out_shape=jax.ShapeDtypeStruct((3, heads, B_KEYS, Q_TILE), F32),
        grid=(heads,),
        in_specs=[pl.BlockSpec(memory_space=pltpu.SMEM)],
        out_specs=pl.BlockSpec((3, None, B_KEYS, Q_TILE), lambda h: (0, h, 0, 0)),
        compiler_params=_params(1),
        name="bias_tiles",
    )(rel_bias)


def _attn_b_kernel(q_ref, k_ref, vt_ref, bias_ref, gt_ref, o_ref, s_ref, mx_ref, res_ref, *, nq):
    tq = Q_TILE
    s_ref[1] = jnp.zeros(s_ref.shape[1:], F32)
    mx_ref[...] = jnp.zeros(mx_ref.shape, F32)
    gt = gt_ref[...]

    def score_stage(i, slot):
        first_key = pl.multiple_of(jnp.maximum(i - 2, 0) * K_TILE, K_TILE)
        qcat = jnp.concatenate(_split_heads(q_ref[pl.ds(pl.multiple_of(i * tq, tq), tq), :]), axis=0)
        var = jnp.minimum(i, 2)
        s = (_dot_t(k_ref[pl.ds(first_key, B_KEYS), :], qcat)
             + jnp.concatenate([bias_ref[var, 0], bias_ref[var, 1]], axis=1))
        s_ref[slot] = s
        mx_ref[slot, 0:1, :] = jnp.max(s, axis=0, keepdims=True)

    def value_stage(i, slot):
        jb = jnp.maximum(i - 2, 0)
        p = jnp.exp2(s_ref[slot] - mx_ref[slot, 0:1, :]).astype(BF16)
        outs = []
        for n in range(2):
            acc = sum(_dot(vt_ref[jb + t][n * B_V_ROWS:(n + 1) * B_V_ROWS, :],
                           p[t * K_TILE:(t + 1) * K_TILE, n * tq:(n + 1) * tq]) for t in range(B_KEYS // K_TILE))
            o = acc[0:HEAD_DIM] * (1.0 / acc[HEAD_DIM:HEAD_DIM + 1])
            rs = lax.rsqrt(jnp.mean(o * o, axis=0, keepdims=True) + EPS)
            outs.append(o * rs * gt[n * HEAD_DIM:(n + 1) * HEAD_DIM, :])
        res_ref[jnp.where(i < 0, nq, i)] = jnp.concatenate(outs, axis=0).T.astype(BF16)

    def step(i, slot):
        score_stage(i, slot)
        value_stage(i - 1, 1 - slot)

    def steps(it, carry):
        for u in range(B_UNROLL):
            step(B_UNROLL * it + u, u % 2)
        return carry

    lax.fori_loop(0, nq // B_UNROLL, steps, 0)
    for i in range(nq // B_UNROLL * B_UNROLL, nq):
        step(i, i % 2)
    value_stage(nq - 1, (nq - 1) % 2)
    for blk in range(nq):
        o_ref[blk * tq:(blk + 1) * tq, :] = res_ref[blk]


def _attn_b(proj, vt, bias_tiles, gn_t, bsz, seq):
    nq = seq // Q_TILE
    nk = seq // K_TILE
    pairs = B_HEADS // 2
    return pl.pallas_call(
        functools.partial(_attn_b_kernel, nq=nq),
        out_shape=jax.ShapeDtypeStruct((bsz * seq, pairs * LANES), BF16),
        grid=(bsz, pairs),
        in_specs=[
            pl.BlockSpec((seq, LANES), lambda b, h: (b, B_Q_BLK + h)),
            pl.BlockSpec((seq, LANES), lambda b, h: (b, B_K_BLK + h)),
            pl.BlockSpec((nk, None, 2 * B_V_ROWS, K_TILE), lambda b, h: (b, h, 0, 0)),
            pl.BlockSpec((3, 2, B_KEYS, Q_TILE), lambda b, h: (0, h, 0, 0)),
            pl.BlockSpec((LANES, Q_TILE), lambda b, h: (h, 0)),
        ],
        out_specs=pl.BlockSpec((seq, LANES), lambda b, h: (b, h)),
        scratch_shapes=[pltpu.VMEM((2, B_KEYS, 2 * Q_TILE), F32), pltpu.VMEM((2, 8, 2 * Q_TILE), F32),
                        pltpu.VMEM((nq + 1, Q_TILE, LANES), BF16)],
        compiler_params=_params(2),
        name="attn_b",
    )(proj, proj, vt, bias_tiles, gn_t)


def _softplus2(z):
    return jnp.maximum(z, 0.0) + jnp.log2(1.0 + jnp.exp2(-jnp.abs(z)))


def _attn_c_kernel(q_ref, k_ref, vt_ref, mask_ref, gt_ref, o_ref, *, n_sub):
    tq, tk = Q_TILE, K_TILE
    tri_r = lax.broadcasted_iota(jnp.int32, (tk, tk), 0)
    tri_c = lax.broadcasted_iota(jnp.int32, (tk, tk), 1)
    later = jnp.where(tri_c > tri_r, 1.0, 0.0).astype(BF16)

    def suffix_sums(x):
        return _dot(later, x.astype(BF16))

    def values(vt, a):
        return tuple(_dot(vt[n * HEAD_DIM:(n + 1) * HEAD_DIM, :], a[:, n * tq:(n + 1) * tq]) for n in range(2))

    subs = range(n_sub)
    qis = [pl.program_id(2) * n_sub + u for u in subs]
    qcats = [jnp.concatenate(_split_heads(q_ref[u * tq:(u + 1) * tq, :]), axis=0) for u in subs]

    j0s = [jnp.maximum(qi - 1, 0) for qi in qis]
    zs = [_dot_t(k_ref[pl.ds(pl.multiple_of(j0 * tk, tk), 2 * tk), :], qcat) + mask_ref[jnp.minimum(qi, 1)]
          for j0, qi, qcat in zip(j0s, qis, qcats)]
    sps = [_softplus2(z) for z in zs]
    gone = [jnp.concatenate([suffix_sums(sp[:tk]) + jnp.sum(sp[tk:], axis=0, keepdims=True),
                             suffix_sums(sp[tk:])], axis=0) for sp in sps]
    a_s = [jnp.exp2((z - sp) - g).astype(BF16) for z, sp, g in zip(zs, sps, gone)]
    cs = [jnp.sum(sp, axis=0, keepdims=True) for sp in sps]
    accs = [tuple(x + y for x, y in zip(values(vt_ref[j0], a[:tk]), values(vt_ref[j0 + 1], a[tk:])))
            for j0, a in zip(j0s, a_s)]

    gt = gt_ref[...]
    for u in subs:
        j0, qcat = j0s[u], qcats[u]

        def live(state):
            t, c, _, _ = state
            return (t < j0) & (jnp.min(c) < -C_EXIT)

        def sweep(state):
            t, c, acc0, acc1 = state
            kj = j0 - 1 - t
            z = _dot_t(k_ref[pl.ds(pl.multiple_of(kj * tk, tk), tk), :], qcat)
            sp = _softplus2(z)
            a = jnp.exp2((z - sp) - suffix_sums(sp) - c).astype(BF16)
            d0, d1 = values(vt_ref[kj], a)
            return t + 1, c + jnp.sum(sp, axis=0, keepdims=True), acc0 + d0, acc1 + d1

        _, _, acc0, acc1 = lax.while_loop(live, sweep, (jnp.int32(0), cs[u], accs[u][0], accs[u][1]))
        outs = []
        for n, o in enumerate((acc0, acc1)):
            rs = lax.rsqrt(jnp.mean(o * o, axis=0, keepdims=True) + EPS)
            outs.append(o * rs * gt[n * HEAD_DIM:(n + 1) * HEAD_DIM, :])
        o_ref[u * tq:(u + 1) * tq, :] = jnp.concatenate(outs, axis=0).T.astype(BF16)


def _causal_bias_c():
    key = jnp.arange(2 * K_TILE)[:, None]
    query = jnp.arange(2 * Q_TILE)[None, :] % Q_TILE
    return jnp.stack([jnp.where(key < query + off, 0.0, NEG) for off in (0, K_TILE)]).astype(F32)


def _attn_c(proj, vt, gn_t, bsz, seq, n_sub=4):
    nq = seq // (Q_TILE * n_sub)
    nk = seq // K_TILE
    pairs = C_HEADS // 2
    return pl.pallas_call(
        functools.partial(_attn_c_kernel, n_sub=n_sub),
        out_shape=jax.ShapeDtypeStruct((bsz * seq, pairs * LANES), BF16),
        grid=(bsz, pairs, nq),
        in_specs=[
            pl.BlockSpec((n_sub * Q_TILE, LANES), lambda b, h, i: (b * nq + i, C_Q_BLK + h)),
            pl.BlockSpec((seq, LANES), lambda b, h, i: (b, C_K_BLK + h)),
            pl.BlockSpec((nk, None, LANES, K_TILE), lambda b, h, i: (b, h, 0, 0)),
            pl.BlockSpec((2, 2 * K_TILE, 2 * Q_TILE), lambda b, h, i: (0, 0, 0)),
            pl.BlockSpec((LANES, Q_TILE), lambda b, h, i: (h, 0)),
        ],
        out_specs=pl.BlockSpec((n_sub * Q_TILE, LANES), lambda b, h, i: (b * nq + i, h)),
        compiler_params=_params(3),
        name="attn_c",
    )(proj, proj, vt, _causal_bias_c(), gn_t)


def _outmlp_kernel(x_ref, oa_ref, ob_ref, oc_ref, wo_ref, g1_ref, g2_ref, wu_ref, wd_ref, g3_ref, o_ref, acc_ref,
                   *, f_chunk):
    wa, wb = oa_ref.shape[1], ob_ref.shape[1]
    y = (_dot(oa_ref[...], wo_ref[0:wa, :]) + _dot(ob_ref[...], wo_ref[wa:wa + wb, :])
         + _dot(oc_ref[...], wo_ref[wa + wb:, :]))
    x1 = x_ref[...] + _rms(y, g1_ref[...])
    h = _rms(x1, g2_ref[...]).astype(BF16)
    d_ff = wu_ref.shape[1]
    for f0 in range(0, d_ff, f_chunk):
        u = jnp.maximum(_dot(h, wu_ref[:, f0:f0 + f_chunk]), 0.0)
        part = _dot((u * u).astype(BF16), wd_ref[f0:f0 + f_chunk, :])
        if f0 == 0:
            acc_ref[...] = part
        else:
            acc_ref[...] += part
    o_ref[...] = x1 + _rms(acc_ref[...], g3_ref[...])


def _outmlp(x, oa, ob, oc, w_out, g_post, g_pre, w_up, w_down, g_post2, tm=1024, f_chunk=1024):
    m, d = x.shape
    d_ff = w_up.shape[1]
    row = lambda w: pl.BlockSpec((tm, w), lambda i: (i, 0))
    full = lambda a, b: pl.BlockSpec((a, b), lambda i: (0, 0), pipeline_mode=pl.Buffered(1))
    return pl.pallas_call(
        functools.partial(_outmlp_kernel, f_chunk=f_chunk),
        out_shape=jax.ShapeDtypeStruct((m, d), F32),
        grid=(m // tm,),
        in_specs=[
            row(d), row(oa.shape[1]), row(ob.shape[1]), row(oc.shape[1]),
            full(d, d), full(1, d), full(1, d), full(d, d_ff), full(d_ff, d), full(1, d),
        ],
        out_specs=row(d),
        scratch_shapes=[pltpu.VMEM((tm, d), F32)],
        compiler_params=_params(1),
        name="outproj_mlp",
    )(x, oa, ob, oc, w_out, g_post, g_pre, w_up, w_down, g_post2)


def _rope_tables(seq):
    pos = jnp.arange(seq, dtype=F32)
    inv_freq = ROPE_THETA ** (-jnp.arange(0, ROPE_DIM, 2, dtype=F32) / ROPE_DIM)
    ang = pos[:, None] * inv_freq[None, :]
    cos, sin = jnp.cos(ang), jnp.sin(ang)
    half = ROPE_DIM // 2
    d = jnp.arange(LANES) % HEAD_DIM
    f = d % half
    rc = jnp.where(d[None, :] < ROPE_DIM, cos[:, f], 1.0)
    rs1 = jnp.where(d[None, :] < half, -sin[:, f], 0.0)
    rs2 = jnp.where((d[None, :] >= half) & (d[None, :] < ROPE_DIM), sin[:, f], 0.0)
    return rc.astype(F32), rs1.astype(F32), rs2.astype(F32)


def _q_scale_columns():
    blk = jnp.arange(D_IN) // LANES
    is_q = (blk < A_K_BLK) | ((blk >= B_Q_BLK) & (blk < B_K_BLK)) | ((blk >= C_Q_BLK) & (blk < C_K_BLK))
    return jnp.where(is_q, HEAD_DIM ** -0.5 * math.log2(math.e), 1.0).astype(F32)


def _gain_t(g, n_blk, width):
    return jnp.broadcast_to(g.reshape(n_blk * LANES, 1), (n_blk * LANES, width)).astype(F32)


def kernel(x, norm_pre_mix, w_in, lam_q1, lam_k1, lam_q2, lam_k2, subln_a, rel_bias, gn_b, gn_c, w_out,
           norm_post_mix, norm_pre_mlp, w_up, w_down, norm_post_mlp):
    bsz, seq, d = x.shape
    depth = w_in.shape[0]
    xf = x.reshape(bsz * seq, d)
    rc, rs1, rs2 = _rope_tables(seq)
    q_scale = _q_scale_columns()
    row = lambda v: v.reshape(1, -1)
    for l in range(depth):
        lam_init = 0.8 - 0.6 * math.exp(-0.3 * l)
        w_in_l = (w_in[l] * q_scale[None, :]).astype(BF16)
        proj, vt_a, vt_b, vt_c = _inproj(xf, row(norm_pre_mix[l]), w_in_l, rc, rs1, rs2, seq)
        lam_vecs = jnp.stack([lam_q1[l], lam_k1[l], lam_q2[l], lam_k2[l]]).astype(F32)
        oa = _attn_a(proj, vt_a, lam_vecs, _gain_t(subln_a[l], 1, A_TILE), lam_init, bsz, seq)
        ob = _attn_b(proj, vt_b, _bias_tiles(rel_bias[l]), _gain_t(gn_b[l], B_HEADS // 2, Q_TILE), bsz, seq)
        oc = _attn_c(proj, vt_c, _gain_t(gn_c[l], C_HEADS // 2, Q_TILE), bsz, seq)
        xf = _outmlp(xf, oa, ob, oc, w_out[l].astype(BF16), row(norm_post_mix[l]), row(norm_pre_mlp[l]),
                     w_up[l].astype(BF16), w_down[l].astype(BF16), row(norm_post_mlp[l]))
    return xf.reshape(bsz, seq, d)
```

```python
import functools
import math

import jax
import jax.numpy as jnp
from jax import lax
from jax.experimental import pallas as pl
from jax.experimental.pallas import tpu as pltpu

F32 = jnp.float32
BF16 = jnp.bfloat16

EPS = 1e-6
ROPE_THETA = 500000.0
HEAD_DIM = 64
ROPE_DIM = HEAD_DIM // 4
CHUNK = 64
LANES = 128
A_HEADS = 4
B_HEADS = 4
C_HEADS = 4
B_LEFT_CHUNKS = 8
REL_CLIP = 128
NEG = -1e30
M_RESET = -5e29
C_EXIT = -160.0

A_Q_BLK, A_K_BLK, A_V_BLK = 0, 4, 8
B_Q_BLK, B_K_BLK, B_V_BLK = 12, 14, 16
C_Q_BLK, C_K_BLK, C_V_BLK = 18, 20, 22
D_IN = 24 * LANES

Q_TILE = 256
K_TILE = 256
A_TILE = 512
A_UNROLL = 4
B_UNROLL = 4
V_ROWS = LANES + 16
B_KEYS = 3 * K_TILE
B_TABLE_W = B_KEYS + 2 * K_TILE
B_V_ROWS = HEAD_DIM + 16

VMEM_LIMIT = 56 * 1024 * 1024


def _params(n_axes, vmem=VMEM_LIMIT):
    return pltpu.CompilerParams(dimension_semantics=("arbitrary",) * n_axes, vmem_limit_bytes=vmem)


def _rms(x, g):
    return x * lax.rsqrt(jnp.mean(x * x, axis=-1, keepdims=True) + EPS) * g


def _dot_t(a, b):
    return lax.dot_general(a, b, (((1,), (1,)), ((), ())), preferred_element_type=F32)


def _dot(a, b):
    return jnp.dot(a, b, preferred_element_type=F32)


def _split_heads(q):
    lane = lax.broadcasted_iota(jnp.int32, q.shape, 1)
    zero = jnp.zeros_like(q)
    return jnp.where(lane < HEAD_DIM, q, zero), jnp.where(lane >= HEAD_DIM, q, zero)


def _inproj_kernel(x_ref, g_ref, w_ref, rc_ref, rs1_ref, rs2_ref, o_ref, vta_ref, vtb_ref, vtc_ref, *, n_chunk):
    h = _rms(x_ref[...], g_ref[...]).astype(BF16)
    rc, rs1, rs2 = rc_ref[...], rs1_ref[...], rs2_ref[...]
    tm = x_ref.shape[0]
    pad_rows = V_ROWS - LANES
    ones_rows = jnp.where(lax.broadcasted_iota(jnp.int32, (pad_rows, A_TILE), 0) == 0, 1.0, 0.0).astype(BF16)
    ones_rows_k = jnp.where(lax.broadcasted_iota(jnp.int32, (pad_rows, K_TILE), 0) == 0, 1.0, 0.0).astype(BF16)
    kt = tm // K_TILE
    for n0 in range(0, D_IN, n_chunk):
        y = _dot(h, w_ref[:, n0:n0 + n_chunk])
        for j in range(n_chunk // LANES):
            blk = n0 // LANES + j
            yj = y[:, j * LANES:(j + 1) * LANES]
            if blk < A_V_BLK:
                yj = yj * rc + pltpu.roll(yj, LANES - ROPE_DIM // 2, 1) * rs1 + pltpu.roll(yj, ROPE_DIM // 2, 1) * rs2
            o_ref[:, n0 + j * LANES:n0 + (j + 1) * LANES] = yj.astype(BF16)
            if A_V_BLK <= blk < B_Q_BLK:
                yt = yj.T.astype(BF16)
                for t in range(tm // A_TILE):
                    vta_ref[t, blk - A_V_BLK, 0:LANES, :] = yt[:, t * A_TILE:(t + 1) * A_TILE]
                    vta_ref[t, blk - A_V_BLK, LANES:V_ROWS, :] = ones_rows
            if B_V_BLK <= blk < C_Q_BLK:
                yt = yj.T.astype(BF16)
                for t in range(kt):
                    for n in range(2):
                        r0 = n * B_V_ROWS
                        vtb_ref[t, blk - B_V_BLK, r0:r0 + HEAD_DIM, :] = (
                            yt[n * HEAD_DIM:(n + 1) * HEAD_DIM, t * K_TILE:(t + 1) * K_TILE])
                        vtb_ref[t, blk - B_V_BLK, r0 + HEAD_DIM:r0 + B_V_ROWS, :] = ones_rows_k
            if blk >= C_V_BLK:
                yt = yj.T.astype(BF16)
                for t in range(kt):
                    vtc_ref[t, blk - C_V_BLK] = yt[:, t * K_TILE:(t + 1) * K_TILE]


def _inproj(x, g, w, rc, rs1, rs2, seq, tm=2 * A_TILE, n_chunk=512):
    m, d = x.shape
    pos_blocks = seq // tm
    rope_spec = pl.BlockSpec((tm, LANES), lambda i: (i % pos_blocks, 0))
    kt = tm // K_TILE
    return pl.pallas_call(
        functools.partial(_inproj_kernel, n_chunk=n_chunk),
        out_shape=(jax.ShapeDtypeStruct((m, D_IN), BF16),
                   jax.ShapeDtypeStruct((m // A_TILE, A_HEADS, V_ROWS, A_TILE), BF16),
                   jax.ShapeDtypeStruct((m // K_TILE, B_HEADS // 2, 2 * B_V_ROWS, K_TILE), BF16),
                   jax.ShapeDtypeStruct((m // K_TILE, C_HEADS // 2, LANES, K_TILE), BF16)),
        grid=(m // tm,),
        in_specs=[
            pl.BlockSpec((tm, d), lambda i: (i, 0)),
            pl.BlockSpec((1, d), lambda i: (0, 0)),
            pl.BlockSpec((d, D_IN), lambda i: (0, 0), pipeline_mode=pl.Buffered(1)),
            rope_spec, rope_spec, rope_spec,
        ],
        out_specs=(pl.BlockSpec((tm, D_IN), lambda i: (i, 0)),
                   pl.BlockSpec((tm // A_TILE, A_HEADS, V_ROWS, A_TILE), lambda i: (i, 0, 0, 0)),
                   pl.BlockSpec((kt, B_HEADS // 2, 2 * B_V_ROWS, K_TILE), lambda i: (i, 0, 0, 0)),
                   pl.BlockSpec((kt, C_HEADS // 2, LANES, K_TILE), lambda i: (i, 0, 0, 0))),
        compiler_params=_params(1),
        name="inproj",
    )(x, g, w, rc, rs1, rs2)


def _attn_a_kernel(q_ref, k_ref, vt_ref, lam_ref, gt_ref, o_ref, qx_ref, m_ref, mx_ref, s_ref, acc_ref,
                   *, lam_init, nq):
    tq = A_TILE
    n_chunks = tq // CHUNK
    fr = lax.broadcasted_iota(jnp.int32, (2 * tq, LANES), 0)
    fc = lax.broadcasted_iota(jnp.int32, (2 * tq, LANES), 1)
    q_feat = jnp.where((fc < n_chunks) & ((fr % tq) // CHUNK == fc), 1.0, 0.0).astype(BF16)
    q_feat_t = q_feat.astype(F32).T.astype(BF16)
    for blk in range(nq):
        q0, q1 = _split_heads(q_ref[blk * tq:(blk + 1) * tq, :])
        qx_ref[blk, 0:LANES, 0:tq] = q0.astype(F32).T.astype(BF16)
        qx_ref[blk, 0:LANES, tq:2 * tq] = q1.astype(F32).T.astype(BF16)
        qx_ref[blk, LANES:2 * LANES, :] = q_feat_t
    kr = lax.broadcasted_iota(jnp.int32, (tq, LANES), 0)
    kc_ = lax.broadcasted_iota(jnp.int32, (tq, LANES), 1)
    k_feat = jnp.where((kc_ < n_chunks) & (kr // CHUNK > kc_), NEG, 0.0).astype(BF16)
    k_nofeat = jnp.zeros_like(k_feat)

    m_ref[...] = jnp.full(m_ref.shape, M_RESET, F32)
    mx_ref[...] = jnp.full(mx_ref.shape, NEG, F32)
    s_ref[1] = jnp.full(s_ref.shape[1:], NEG, F32)

    @pl.when((pl.program_id(0) == 0) & (pl.program_id(1) == 0))
    def _():
        acc_ref[...] = jnp.zeros(acc_ref.shape, F32)

    def score_stage(qa, ka, slot):
        k = k_ref[pl.ds(pl.multiple_of(ka * tq, tq), tq), :]
        kx = jnp.concatenate([k, jnp.where(ka == qa, k_feat, k_nofeat)], axis=1)
        s_new = _dot(kx, qx_ref[qa])
        s_ref[slot] = s_new
        mx_ref[slot, 0:1, :] = jnp.max(s_new, axis=0, keepdims=True)

    def value_stage(qb, kb, slot):
        m_old = jnp.where(kb == 0, M_RESET, m_ref[0:1, :])
        m_new = jnp.maximum(m_old, mx_ref[slot, 0:1, :])
        m_ref[0:1, :] = m_new
        p = jnp.exp2(s_ref[slot] - m_new).astype(BF16)
        acc_ref[qb] = jnp.exp2(m_old - m_new) * acc_ref[qb] + _dot(vt_ref[jnp.maximum(kb, 0)], p)

    def step(st, slot):
        qa, ka, qb, kb = st
        score_stage(qa, ka, slot)
        value_stage(qb, kb, 1 - slot)
        last = ka == qa
        return (jnp.where(last, qa + 1, qa), jnp.where(last, 0, ka + 1), qa, ka)

    def steps(i, st):
        for u in range(A_UNROLL):
            st = step(st, u % 2)
        return st

    n_tiles = nq * (nq + 1) // 2
    zero, neg1 = jnp.int32(0), jnp.int32(-1)
    st = lax.fori_loop(0, n_tiles // A_UNROLL, steps, (zero, zero, zero, neg1))
    for f in range(n_tiles // A_UNROLL * A_UNROLL, n_tiles):
        st = step(st, f % 2)
    value_stage(st[2], st[3], (n_tiles - 1) % 2)

    lv = lam_ref[...]
    lam = (jnp.exp(jnp.sum(lv[0:1] * lv[1:2], axis=-1, keepdims=True))
           - jnp.exp(jnp.sum(lv[2:3] * lv[3:4], axis=-1, keepdims=True)) + lam_init)
    for blk in range(nq):
        acc = acc_ref[blk]
        on = acc[0:LANES] * (1.0 / acc[LANES:LANES + 1])
        o = on[:, :tq] - lam * on[:, tq:]
        o = o * lax.rsqrt(jnp.mean(o * o, axis=0, keepdims=True) + EPS) * gt_ref[...] * (1.0 - lam_init)
        o_ref[blk * tq:(blk + 1) * tq, :] = o.T.astype(BF16)


def _attn_a(proj, vt, lam_vecs, subln_gt, lam_init, bsz, seq):
    nq = seq // A_TILE
    return pl.pallas_call(
        functools.partial(_attn_a_kernel, lam_init=lam_init, nq=nq),
        out_shape=jax.ShapeDtypeStruct((bsz * seq, A_HEADS * LANES), BF16),
        grid=(bsz, A_HEADS),
        in_specs=[
            pl.BlockSpec((seq, LANES), lambda b, h: (b, A_Q_BLK + h)),
            pl.BlockSpec((seq, LANES), lambda b, h: (b, A_K_BLK + h)),
            pl.BlockSpec((nq, None, V_ROWS, A_TILE), lambda b, h: (b, h, 0, 0)),
            pl.BlockSpec((4, HEAD_DIM), lambda b, h: (0, 0)),
            pl.BlockSpec((LANES, A_TILE), lambda b, h: (0, 0)),
        ],
        out_specs=pl.BlockSpec((seq, LANES), lambda b, h: (b, h)),
        scratch_shapes=[pltpu.VMEM((nq, 2 * LANES, 2 * A_TILE), BF16), pltpu.VMEM((8, 2 * A_TILE), F32),
                        pltpu.VMEM((2, 8, 2 * A_TILE), F32),
                        pltpu.VMEM((2, A_TILE, 2 * A_TILE), F32),
                        pltpu.VMEM((nq, V_ROWS, 2 * A_TILE), F32)],
        compiler_params=_params(2),
        name="attn_a",
    )(proj, proj, vt, lam_vecs, subln_gt)


def _bias_kernel(base_ref, o_ref):
    win = 3 * LANES
    base = base_ref[...] * math.log2(math.e)
    spread = jnp.broadcast_to(base, (CHUNK, win + LANES))
    ws = tuple(pltpu.roll(spread, shift, 1, stride=1, stride_axis=0)[:, 0:win] for shift in (0, CHUNK))
    far = base[:, win:win + 1]

    shape = (Q_TILE, B_TABLE_W)
    row = lax.broadcasted_iota(jnp.int32, shape, 0)
    col = lax.broadcasted_iota(jnp.int32, shape, 1)
    kc = col // CHUNK - row // CHUNK
    band = (kc >= 0) & (kc <= B_LEFT_CHUNKS)

    win_start = (3 * LANES, 3 * LANES, 4 * LANES, 4 * LANES)
    rows = []
    for c in range(Q_TILE // CHUNK):
        pieces = []
        for j in range(B_TABLE_W // LANES):
            lo = j * LANES
            bnd = band[c * CHUNK:(c + 1) * CHUNK, lo:lo + LANES]
            if win_start[c] <= lo < win_start[c] + win:
                off = lo - win_start[c]
                pieces.append(jnp.where(bnd, ws[c % 2][:, off:off + LANES], NEG))
            else:
                pieces.append(jnp.where(bnd, far, NEG))
        rows.append(jnp.concatenate(pieces, axis=1))
    table = jnp.concatenate(rows, axis=0)
    for v in range(3):
        c0 = (2 - v) * K_TILE
        o_ref[v] = table[:, c0:c0 + B_KEYS].T


def _bias_tiles(rel_bias):
    heads = rel_bias.shape[0]
    win = 3 * LANES
    base = jnp.concatenate([rel_bias[:, ::-1],
                            jnp.broadcast_to(rel_bias[:, :1], (heads, win - 2 * REL_CLIP - 1)),
                            jnp.broadcast_to(rel_bias[:, -1:], (heads, LANES))], axis=1).astype(F32)
    return pl.pallas_call(
        _bias_kernel,
        out_shape=jax.ShapeDtypeStruct((3, heads, B_KEYS, Q_TILE), F32),
        grid=(heads,),
        in_specs=[pl.BlockSpec((None, 1, win + LANES), lambda h: (h, 0, 0))],
        out_specs=pl.BlockSpec((3, None, B_KEYS, Q_TILE), lambda h: (0, h, 0, 0)),
        compiler_params=_params(1),
        name="bias_tiles",
    )(base.reshape(heads, 1, win + LANES))


def _attn_b_kernel(q_ref, k_ref, vt_ref, bias_ref, gt_ref, o_ref, s_ref, mx_ref, res_ref, *, nq):
    tq = Q_TILE
    s_ref[1] = jnp.zeros(s_ref.shape[1:], F32)
    mx_ref[...] = jnp.zeros(mx_ref.shape, F32)
    gt = gt_ref[...]

    def score_stage(i, slot):
        first_key = pl.multiple_of(jnp.maximum(i - 2, 0) * K_TILE, K_TILE)
        qcat = jnp.concatenate(_split_heads(q_ref[pl.ds(pl.multiple_of(i * tq, tq), tq), :]), axis=0)
        var = jnp.minimum(i, 2)
        s = (_dot_t(k_ref[pl.ds(first_key, B_KEYS), :], qcat)
             + jnp.concatenate([bias_ref[var, 0], bias_ref[var, 1]], axis=1))
        s_ref[slot] = s
        mx_ref[slot, 0:1, :] = jnp.max(s, axis=0, keepdims=True)

    def value_stage(i, slot):
        jb = jnp.maximum(i - 2, 0)
        p = jnp.exp2(s_ref[slot] - mx_ref[slot, 0:1, :]).astype(BF16)
        outs = []
        for n in range(2):
            acc = sum(_dot(vt_ref[jb + t][n * B_V_ROWS:(n + 1) * B_V_ROWS, :],
                           p[t * K_TILE:(t + 1) * K_TILE, n * tq:(n + 1) * tq]) for t in range(B_KEYS // K_TILE))
            o = acc[0:HEAD_DIM] * (1.0 / acc[HEAD_DIM:HEAD_DIM + 1])
            rs = lax.rsqrt(jnp.mean(o * o, axis=0, keepdims=True) + EPS)
            outs.append(o * rs * gt[n * HEAD_DIM:(n + 1) * HEAD_DIM, :])
        res_ref[jnp.where(i < 0, nq, i)] = jnp.concatenate(outs, axis=0).T.astype(BF16)

    def step(i, slot):
        score_stage(i, slot)
        value_stage(i - 1, 1 - slot)

    def steps(it, carry):
        for u in range(B_UNROLL):
            step(B_UNROLL * it + u, u % 2)
        return carry

    lax.fori_loop(0, nq // B_UNROLL, steps, 0)
    for i in range(nq // B_UNROLL * B_UNROLL, nq):
        step(i, i % 2)
    value_stage(nq - 1, (nq - 1) % 2)
    for blk in range(nq):
        o_ref[blk * tq:(blk + 1) * tq, :] = res_ref[blk]


def _attn_b(proj, vt, bias_tiles, gn_t, bsz, seq):
    nq = seq // Q_TILE
    nk = seq // K_TILE
    pairs = B_HEADS // 2
    return pl.pallas_call(
        functools.partial(_attn_b_kernel, nq=nq),
        out_shape=jax.ShapeDtypeStruct((bsz * seq, pairs * LANES), BF16),
        grid=(bsz, pairs),
        in_specs=[
            pl.BlockSpec((seq, LANES), lambda b, h: (b, B_Q_BLK + h)),
            pl.BlockSpec((seq, LANES), lambda b, h: (b, B_K_BLK + h)),
            pl.BlockSpec((nk, None, 2 * B_V_ROWS, K_TILE), lambda b, h: (b, h, 0, 0)),
            pl.BlockSpec((3, 2, B_KEYS, Q_TILE), lambda b, h: (0, h, 0, 0)),
            pl.BlockSpec((LANES, Q_TILE), lambda b, h: (h, 0)),
        ],
        out_specs=pl.BlockSpec((seq, LANES), lambda b, h: (b, h)),
        scratch_shapes=[pltpu.VMEM((2, B_KEYS, 2 * Q_TILE), F32), pltpu.VMEM((2, 8, 2 * Q_TILE), F32),
                        pltpu.VMEM((nq + 1, Q_TILE, LANES), BF16)],
        compiler_params=_params(2),
        name="attn_b",
    )(proj, proj, vt, bias_tiles, gn_t)


def _softplus2(z):
    return jnp.maximum(z, 0.0) + jnp.log2(1.0 + jnp.exp2(-jnp.abs(z)))


def _attn_c_kernel(q_ref, k_ref, vt_ref, mask_ref, gt_ref, o_ref, *, n_sub):
    tq, tk = Q_TILE, K_TILE
    tri_r = lax.broadcasted_iota(jnp.int32, (tk, tk), 0)
    tri_c = lax.broadcasted_iota(jnp.int32, (tk, tk), 1)
    later = jnp.where(tri_c > tri_r, 1.0, 0.0).astype(BF16)

    def suffix_sums(x):
        return _dot(later, x.astype(BF16))

    def values(vt, a):
        return tuple(_dot(vt[n * HEAD_DIM:(n + 1) * HEAD_DIM, :], a[:, n * tq:(n + 1) * tq]) for n in range(2))

    subs = range(n_sub)
    qis = [pl.program_id(2) * n_sub + u for u in subs]
    qcats = [jnp.concatenate(_split_heads(q_ref[u * tq:(u + 1) * tq, :]), axis=0) for u in subs]

    j0s = [jnp.maximum(qi - 1, 0) for qi in qis]
    zs = [_dot_t(k_ref[pl.ds(pl.multiple_of(j0 * tk, tk), 2 * tk), :], qcat) + mask_ref[jnp.minimum(qi, 1)]
          for j0, qi, qcat in zip(j0s, qis, qcats)]
    sps = [_softplus2(z) for z in zs]
    gone = [jnp.concatenate([suffix_sums(sp[:tk]) + jnp.sum(sp[tk:], axis=0, keepdims=True),
                             suffix_sums(sp[tk:])], axis=0) for sp in sps]
    a_s = [jnp.exp2((z - sp) - g).astype(BF16) for z, sp, g in zip(zs, sps, gone)]
    cs = [jnp.sum(sp, axis=0, keepdims=True) for sp in sps]
    accs = [tuple(x + y for x, y in zip(values(vt_ref[j0], a[:tk]), values(vt_ref[j0 + 1], a[tk:])))
            for j0, a in zip(j0s, a_s)]

    gt = gt_ref[...]
    for u in subs:
        j0, qcat = j0s[u], qcats[u]

        def live(state):
            t, c, _, _ = state
            return (t < j0) & (jnp.min(c) < -C_EXIT)

        def sweep(state):
            t, c, acc0, acc1 = state
            kj = j0 - 1 - t
            z = _dot_t(k_ref[pl.ds(pl.multiple_of(kj * tk, tk), tk), :], qcat)
            sp = _softplus2(z)
            a = jnp.exp2((z - sp) - suffix_sums(sp) - c).astype(BF16)
            d0, d1 = values(vt_ref[kj], a)
            return t + 1, c + jnp.sum(sp, axis=0, keepdims=True), acc0 + d0, acc1 + d1

        _, _, acc0, acc1 = lax.while_loop(live, sweep, (jnp.int32(0), cs[u], accs[u][0], accs[u][1]))
        outs = []
        for n, o in enumerate((acc0, acc1)):
            rs = lax.rsqrt(jnp.mean(o * o, axis=0, keepdims=True) + EPS)
            outs.append(o * rs * gt[n * HEAD_DIM:(n + 1) * HEAD_DIM, :])
        o_ref[u * tq:(u + 1) * tq, :] = jnp.concatenate(outs, axis=0).T.astype(BF16)


def _causal_bias_c():
    key = jnp.arange(2 * K_TILE)[:, None]
    query = jnp.arange(2 * Q_TILE)[None, :] % Q_TILE
    return jnp.stack([jnp.where(key < query + off, 0.0, NEG) for off in (0, K_TILE)]).astype(F32)


def _attn_c(proj, vt, gn_t, bsz, seq, n_sub=4):
    nq = seq // (Q_TILE * n_sub)
    nk = seq // K_TILE
    pairs = C_HEADS // 2
    return pl.pallas_call(
        functools.partial(_attn_c_kernel, n_sub=n_sub),
        out_shape=jax.ShapeDtypeStruct((bsz * seq, pairs * LANES), BF16),
        grid=(bsz, pairs, nq),
        in_specs=[
            pl.BlockSpec((n_sub * Q_TILE, LANES), lambda b, h, i: (b * nq + i, C_Q_BLK + h)),
            pl.BlockSpec((seq, LANES), lambda b, h, i: (b, C_K_BLK + h)),
            pl.BlockSpec((nk, None, LANES, K_TILE), lambda b, h, i: (b, h, 0, 0)),
            pl.BlockSpec((2, 2 * K_TILE, 2 * Q_TILE), lambda b, h, i: (0, 0, 0)),
            pl.BlockSpec((LANES, Q_TILE), lambda b, h, i: (h, 0)),
        ],
        out_specs=pl.BlockSpec((n_sub * Q_TILE, LANES), lambda b, h, i: (b * nq + i, h)),
        compiler_params=_params(3),
        name="attn_c",
    )(proj, proj, vt, _causal_bias_c(), gn_t)


def _outmlp_kernel(x_ref, oa_ref, ob_ref, oc_ref, wo_ref, g1_ref, g2_ref, wu_ref, wd_ref, g3_ref, o_ref, acc_ref,
                   *, f_chunk):
    wa, wb = oa_ref.shape[1], ob_ref.shape[1]
    y = (_dot(oa_ref[...], wo_ref[0:wa, :]) + _dot(ob_ref[...], wo_ref[wa:wa + wb, :])
         + _dot(oc_ref[...], wo_ref[wa + wb:, :]))
    x1 = x_ref[...] + _rms(y, g1_ref[...])
    h = _rms(x1, g2_ref[...]).astype(BF16)
    d_ff = wu_ref.shape[1]
    for f0 in range(0, d_ff, f_chunk):
        u = jnp.maximum(_dot(h, wu_ref[:, f0:f0 + f_chunk]), 0.0)
        part = _dot((u * u).astype(BF16), wd_ref[f0:f0 + f_chunk, :])
        if f0 == 0:
            acc_ref[...] = part
        else:
            acc_ref[...] += part
    o_ref[...] = x1 + _rms(acc_ref[...], g3_ref[...])


def _outmlp(x, oa, ob, oc, w_out, g_post, g_pre, w_up, w_down, g_post2, tm=1024, f_chunk=1024):
    m, d = x.shape
    d_ff = w_up.shape[1]
    row = lambda w: pl.BlockSpec((tm, w), lambda i: (i, 0))
    full = lambda a, b: pl.BlockSpec((a, b), lambda i: (0, 0), pipeline_mode=pl.Buffered(1))
    return pl.pallas_call(
        functools.partial(_outmlp_kernel, f_chunk=f_chunk),
        out_shape=jax.ShapeDtypeStruct((m, d), F32),
        grid=(m // tm,),
        in_specs=[
            row(d), row(oa.shape[1]), row(ob.shape[1]), row(oc.shape[1]),
            full(d, d), full(1, d), full(1, d), full(d, d_ff), full(d_ff, d), full(1, d),
        ],
        out_specs=row(d),
        scratch_shapes=[pltpu.VMEM((tm, d), F32)],
        compiler_params=_params(1),
        name="outproj_mlp",
    )(x, oa, ob, oc, w_out, g_post, g_pre, w_up, w_down, g_post2)


def _rope_tables(seq):
    pos = jnp.arange(seq, dtype=F32)
    inv_freq = ROPE_THETA ** (-jnp.arange(0, ROPE_DIM, 2, dtype=F32) / ROPE_DIM)
    ang = pos[:, None] * inv_freq[None, :]
    cos, sin = jnp.cos(ang), jnp.sin(ang)
    half = ROPE_DIM // 2
    d = jnp.arange(LANES) % HEAD_DIM
    f = d % half
    rc = jnp.where(d[None, :] < ROPE_DIM, cos[:, f], 1.0)
    rs1 = jnp.where(d[None, :] < half, -sin[:, f], 0.0)
    rs2 = jnp.where((d[None, :] >= half) & (d[None, :] < ROPE_DIM), sin[:, f], 0.0)
    return rc.astype(F32), rs1.astype(F32), rs2.astype(F32)


def _q_scale_columns():
    blk = jnp.arange(D_IN) // LANES
    is_q = (blk < A_K_BLK) | ((blk >= B_Q_BLK) & (blk < B_K_BLK)) | ((blk >= C_Q_BLK) & (blk < C_K_BLK))
    return jnp.where(is_q, HEAD_DIM ** -0.5 * math.log2(math.e), 1.0).astype(F32)


def _gain_t(g, n_blk, width):
    return jnp.broadcast_to(g.reshape(n_blk * LANES, 1), (n_blk * LANES, width)).astype(F32)


def kernel(x, norm_pre_mix, w_in, lam_q1, lam_k1, lam_q2, lam_k2, subln_a, rel_bias, gn_b, gn_c, w_out,
           norm_post_mix, norm_pre_mlp, w_up, w_down, norm_post_mlp):
    bsz, seq, d = x.shape
    depth = w_in.shape[0]
    xf = x.reshape(bsz * seq, d)
    rc, rs1, rs2 = _rope_tables(seq)
    q_scale = _q_scale_columns()
    row = lambda v: v.reshape(1, -1)
    for l in range(depth):
        lam_init = 0.8 - 0.6 * math.exp(-0.3 * l)
        w_in_l = (w_in[l] * q_scale[None, :]).astype(BF16)
        proj, vt_a, vt_b, vt_c = _inproj(xf, row(norm_pre_mix[l]), w_in_l, rc, rs1, rs2, seq)
        lam_vecs = jnp.stack([lam_q1[l], lam_k1[l], lam_q2[l], lam_k2[l]]).astype(F32)
        oa = _attn_a(proj, vt_a, lam_vecs, _gain_t(subln_a[l], 1, A_TILE), lam_init, bsz, seq)
        ob = _attn_b(proj, vt_b, _bias_tiles(rel_bias[l]), _gain_t(gn_b[l], B_HEADS // 2, Q_TILE), bsz, seq)
        oc = _attn_c(proj, vt_c, _gain_t(gn_c[l], C_HEADS // 2, Q_TILE), bsz, seq)
        xf = _outmlp(xf, oa, ob, oc, w_out[l].astype(BF16), row(norm_post_mix[l]), row(norm_pre_mlp[l]),
                     w_up[l].astype(BF16), w_down[l].astype(BF16), row(norm_post_mlp[l]))
    return xf.reshape(bsz, seq, d)
```

```python
import functools
import math

import jax
import jax.numpy as jnp
from jax import lax
from jax.experimental import pallas as pl
from jax.experimental.pallas import tpu as pltpu

F32 = jnp.float32
BF16 = jnp.bfloat16

EPS = 1e-6
ROPE_THETA = 500000.0
HEAD_DIM = 64
ROPE_DIM = HEAD_DIM // 4
CHUNK = 64
LANES = 128
A_HEADS = 4
B_HEADS = 4
C_HEADS = 4
B_LEFT_CHUNKS = 8
REL_CLIP = 128
NEG = -1e30
M_RESET = -5e29
C_EXIT = -160.0

A_Q_BLK, A_K_BLK, A_V_BLK = 0, 4, 8
B_Q_BLK, B_K_BLK, B_V_BLK = 12, 14, 16
C_Q_BLK, C_K_BLK, C_V_BLK = 18, 20, 22
D_IN = 24 * LANES

Q_TILE = 256
K_TILE = 256
A_TILE = 512
A_UNROLL = 4
B_UNROLL = 4
V_ROWS = LANES + 16
B_KEYS = 3 * K_TILE
B_TABLE_W = B_KEYS + 2 * K_TILE
B_V_ROWS = HEAD_DIM + 16

VMEM_LIMIT = 56 * 1024 * 1024


def _params(n_axes, vmem=VMEM_LIMIT):
    return pltpu.CompilerParams(dimension_semantics=("arbitrary",) * n_axes, vmem_limit_bytes=vmem)


def _rms(x, g):
    return x * lax.rsqrt(jnp.mean(x * x, axis=-1, keepdims=True) + EPS) * g


def _dot_t(a, b):
    return lax.dot_general(a, b, (((1,), (1,)), ((), ())), preferred_element_type=F32)


def _dot(a, b):
    return jnp.dot(a, b, preferred_element_type=F32)


def _split_heads(q):
    lane = lax.broadcasted_iota(jnp.int32, q.shape, 1)
    zero = jnp.zeros_like(q)
    return jnp.where(lane < HEAD_DIM, q, zero), jnp.where(lane >= HEAD_DIM, q, zero)


def _inproj_kernel(x_ref, g_ref, w_ref, rc_ref, rs1_ref, rs2_ref, o_ref, vta_ref, vtb_ref, vtc_ref, *, n_chunk):
    h = _rms(x_ref[...], g_ref[...]).astype(BF16)
    rc, rs1, rs2 = rc_ref[...], rs1_ref[...], rs2_ref[...]
    tm = x_ref.shape[0]
    pad_rows = V_ROWS - LANES
    ones_rows = jnp.where(lax.broadcasted_iota(jnp.int32, (pad_rows, A_TILE), 0) == 0, 1.0, 0.0).astype(BF16)
    ones_rows_k = jnp.where(lax.broadcasted_iota(jnp.int32, (pad_rows, K_TILE), 0) == 0, 1.0, 0.0).astype(BF16)
    kt = tm // K_TILE
    for n0 in range(0, D_IN, n_chunk):
        y = _dot(h, w_ref[:, n0:n0 + n_chunk])
        for j in range(n_chunk // LANES):
            blk = n0 // LANES + j
            yj = y[:, j * LANES:(j + 1) * LANES]
            if blk < A_V_BLK:
                yj = yj * rc + pltpu.roll(yj, LANES - ROPE_DIM // 2, 1) * rs1 + pltpu.roll(yj, ROPE_DIM // 2, 1) * rs2
            o_ref[:, n0 + j * LANES:n0 + (j + 1) * LANES] = yj.astype(BF16)
            if A_V_BLK <= blk < B_Q_BLK:
                yt = yj.T.astype(BF16)
                for t in range(tm // A_TILE):
                    vta_ref[t, blk - A_V_BLK, 0:LANES, :] = yt[:, t * A_TILE:(t + 1) * A_TILE]
                    vta_ref[t, blk - A_V_BLK, LANES:V_ROWS, :] = ones_rows
            if B_V_BLK <= blk < C_Q_BLK:
                yt = yj.T.astype(BF16)
                for t in range(kt):
                    for n in range(2):
                        r0 = n * B_V_ROWS
                        vtb_ref[t, blk - B_V_BLK, r0:r0 + HEAD_DIM, :] = (
                            yt[n * HEAD_DIM:(n + 1) * HEAD_DIM, t * K_TILE:(t + 1) * K_TILE])
                        vtb_ref[t, blk - B_V_BLK, r0 + HEAD_DIM:r0 + B_V_ROWS, :] = ones_rows_k
            if blk >= C_V_BLK:
                yt = yj.T.astype(BF16)
                for t in range(kt):
                    vtc_ref[t, blk - C_V_BLK] = yt[:, t * K_TILE:(t + 1) * K_TILE]


def _inproj(x, g, w, rc, rs1, rs2, seq, tm=2 * A_TILE, n_chunk=512):
    m, d = x.shape
    pos_blocks = seq // tm
    rope_spec = pl.BlockSpec((tm, LANES), lambda i: (i % pos_blocks, 0))
    kt = tm // K_TILE
    return pl.pallas_call(
        functools.partial(_inproj_kernel, n_chunk=n_chunk),
        out_shape=(jax.ShapeDtypeStruct((m, D_IN), BF16),
                   jax.ShapeDtypeStruct((m // A_TILE, A_HEADS, V_ROWS, A_TILE), BF16),
                   jax.ShapeDtypeStruct((m // K_TILE, B_HEADS // 2, 2 * B_V_ROWS, K_TILE), BF16),
                   jax.ShapeDtypeStruct((m // K_TILE, C_HEADS // 2, LANES, K_TILE), BF16)),
        grid=(m // tm,),
        in_specs=[
            pl.BlockSpec((tm, d), lambda i: (i, 0)),
            pl.BlockSpec((1, d), lambda i: (0, 0)),
            pl.BlockSpec((d, D_IN), lambda i: (0, 0), pipeline_mode=pl.Buffered(1)),
            rope_spec, rope_spec, rope_spec,
        ],
        out_specs=(pl.BlockSpec((tm, D_IN), lambda i: (i, 0)),
                   pl.BlockSpec((tm // A_TILE, A_HEADS, V_ROWS, A_TILE), lambda i: (i, 0, 0, 0)),
                   pl.BlockSpec((kt, B_HEADS // 2, 2 * B_V_ROWS, K_TILE), lambda i: (i, 0, 0, 0)),
                   pl.BlockSpec((kt, C_HEADS // 2, LANES, K_TILE), lambda i: (i, 0, 0, 0))),
        compiler_params=_params(1),
        name="inproj",
    )(x, g, w, rc, rs1, rs2)


def _attn_a_kernel(q_ref, k_ref, vt_ref, lam_ref, gt_ref, o_ref, qx_ref, m_ref, mx_ref, s_ref, acc_ref,
                   *, lam_init, nq):
    tq = A_TILE
    n_chunks = tq // CHUNK
    fr = lax.broadcasted_iota(jnp.int32, (2 * tq, LANES), 0)
    fc = lax.broadcasted_iota(jnp.int32, (2 * tq, LANES), 1)
    q_feat = jnp.where((fc < n_chunks) & ((fr % tq) // CHUNK == fc), 1.0, 0.0).astype(BF16)
    q_feat_t = q_feat.astype(F32).T.astype(BF16)
    for blk in range(nq):
        q0, q1 = _split_heads(q_ref[blk * tq:(blk + 1) * tq, :])
        qx_ref[blk, 0:LANES, 0:tq] = q0.astype(F32).T.astype(BF16)
        qx_ref[blk, 0:LANES, tq:2 * tq] = q1.astype(F32).T.astype(BF16)
        qx_ref[blk, LANES:2 * LANES, :] = q_feat_t
    kr = lax.broadcasted_iota(jnp.int32, (tq, LANES), 0)
    kc_ = lax.broadcasted_iota(jnp.int32, (tq, LANES), 1)
    k_feat = jnp.where((kc_ < n_chunks) & (kr // CHUNK > kc_), NEG, 0.0).astype(BF16)
    k_nofeat = jnp.zeros_like(k_feat)

    m_ref[...] = jnp.full(m_ref.shape, M_RESET, F32)
    mx_ref[...] = jnp.full(mx_ref.shape, NEG, F32)
    s_ref[1] = jnp.full(s_ref.shape[1:], NEG, F32)

    @pl.when((pl.program_id(0) == 0) & (pl.program_id(1) == 0))
    def _():
        acc_ref[...] = jnp.zeros(acc_ref.shape, F32)

    def score_stage(qa, ka, slot):
        k = k_ref[pl.ds(pl.multiple_of(ka * tq, tq), tq), :]
        kx = jnp.concatenate([k, jnp.where(ka == qa, k_feat, k_nofeat)], axis=1)
        s_new = _dot(kx, qx_ref[qa])
        s_ref[slot] = s_new
        mx_ref[slot, 0:1, :] = jnp.max(s_new, axis=0, keepdims=True)

    def value_stage(qb, kb, slot):
        m_old = jnp.where(kb == 0, M_RESET, m_ref[0:1, :])
        m_new = jnp.maximum(m_old, mx_ref[slot, 0:1, :])
        m_ref[0:1, :] = m_new
        p = jnp.exp2(s_ref[slot] - m_new).astype(BF16)
        acc_ref[qb] = jnp.exp2(m_old - m_new) * acc_ref[qb] + _dot(vt_ref[jnp.maximum(kb, 0)], p)

    def step(st, slot):
        qa, ka, qb, kb = st
        score_stage(qa, ka, slot)
        value_stage(qb, kb, 1 - slot)
        last = ka == qa
        return (jnp.where(last, qa + 1, qa), jnp.where(last, 0, ka + 1), qa, ka)

    def steps(i, st):
        for u in range(A_UNROLL):
            st = step(st, u % 2)
        return st

    n_tiles = nq * (nq + 1) // 2
    zero, neg1 = jnp.int32(0), jnp.int32(-1)
    st = lax.fori_loop(0, n_tiles // A_UNROLL, steps, (zero, zero, zero, neg1))
    for f in range(n_tiles // A_UNROLL * A_UNROLL, n_tiles):
        st = step(st, f % 2)
    value_stage(st[2], st[3], (n_tiles - 1) % 2)

    lv = lam_ref[...]
    lam = (jnp.exp(jnp.sum(lv[0:1] * lv[1:2], axis=-1, keepdims=True))
           - jnp.exp(jnp.sum(lv[2:3] * lv[3:4], axis=-1, keepdims=True)) + lam_init)
    for blk in range(nq):
        acc = acc_ref[blk]
        on = acc[0:LANES] * (1.0 / acc[LANES:LANES + 1])
        o = on[:, :tq] - lam * on[:, tq:]
        o = o * lax.rsqrt(jnp.mean(o * o, axis=0, keepdims=True) + EPS) * gt_ref[...] * (1.0 - lam_init)
        o_ref[blk * tq:(blk + 1) * tq, :] = o.T.astype(BF16)


def _attn_a(proj, vt, lam_vecs, subln_gt, lam_init, bsz, seq):
    nq = seq // A_TILE
    return pl.pallas_call(
        functools.partial(_attn_a_kernel, lam_init=lam_init, nq=nq),
        out_shape=jax.ShapeDtypeStruct((bsz * seq, A_HEADS * LANES), BF16),
        grid=(bsz, A_HEADS),
        in_specs=[
            pl.BlockSpec((seq, LANES), lambda b, h: (b, A_Q_BLK + h)),
            pl.BlockSpec((seq, LANES), lambda b, h: (b, A_K_BLK + h)),
            pl.BlockSpec((nq, None, V_ROWS, A_TILE), lambda b, h: (b, h, 0, 0)),
            pl.BlockSpec((4, HEAD_DIM), lambda b, h: (0, 0)),
            pl.BlockSpec((LANES, A_TILE), lambda b, h: (0, 0)),
        ],
        out_specs=pl.BlockSpec((seq, LANES), lambda b, h: (b, h)),
        scratch_shapes=[pltpu.VMEM((nq, 2 * LANES, 2 * A_TILE), BF16), pltpu.VMEM((8, 2 * A_TILE), F32),
                        pltpu.VMEM((2, 8, 2 * A_TILE), F32),
                        pltpu.VMEM((2, A_TILE, 2 * A_TILE), F32),
                        pltpu.VMEM((nq, V_ROWS, 2 * A_TILE), F32)],
        compiler_params=_params(2),
        name="attn_a",
    )(proj, proj, vt, lam_vecs, subln_gt)


def _bias_kernel(base_ref, o_ref):
    win = 3 * LANES
    base = base_ref[...] * math.log2(math.e)
    spread = jnp.broadcast_to(base, (CHUNK, win + LANES))
    ws = tuple(pltpu.roll(spread, shift, 1, stride=1, stride_axis=0)[:, 0:win] for shift in (0, CHUNK))
    far = base[:, win:win + 1]

    shape = (Q_TILE, B_TABLE_W)
    row = lax.broadcasted_iota(jnp.int32, shape, 0)
    col = lax.broadcasted_iota(jnp.int32, shape, 1)
    kc = col // CHUNK - row // CHUNK
    band = (kc >= 0) & (kc <= B_LEFT_CHUNKS)

    win_start = (3 * LANES, 3 * LANES, 4 * LANES, 4 * LANES)
    rows = []
    for c in range(Q_TILE // CHUNK):
        pieces = []
        for j in range(B_TABLE_W // LANES):
            lo = j * LANES
            bnd = band[c * CHUNK:(c + 1) * CHUNK, lo:lo + LANES]
            if win_start[c] <= lo < win_start[c] + win:
                off = lo - win_start[c]
                pieces.append(jnp.where(bnd, ws[c % 2][:, off:off + LANES], NEG))
            else:
                pieces.append(jnp.where(bnd, far, NEG))
        rows.append(jnp.concatenate(pieces, axis=1))
    table = jnp.concatenate(rows, axis=0)
    for v in range(3):
        c0 = (2 - v) * K_TILE
        o_ref[v] = table[:, c0:c0 + B_KEYS].T


def _bias_tiles(rel_bias):
    heads = rel_bias.shape[0]
    win = 3 * LANES
    base = jnp.concatenate([rel_bias[:, ::-1],
                            jnp.broadcast_to(rel_bias[:, :1], (heads, win - 2 * REL_CLIP - 1)),
                            jnp.broadcast_to(rel_bias[:, -1:], (heads, LANES))], axis=1).astype(F32)
    return pl.pallas_call(
        _bias_kernel,
        out_shape=jax.ShapeDtypeStruct((3, heads, B_KEYS, Q_TILE), F32),
        grid=(heads,),
        in_specs=[pl.BlockSpec((None, 1, win + LANES), lambda h: (h, 0, 0))],
        out_specs=pl.BlockSpec((3, None, B_KEYS, Q_TILE), lambda h: (0, h, 0, 0)),
        compiler_params=_params(1),
        name="bias_tiles",
    )(base.reshape(heads, 1, win + LANES))


def _attn_b_kernel(q_ref, k_ref, vt_ref, bias_ref, gt_ref, o_ref, s_ref, mx_ref, res_ref, *, nq):
    tq = Q_TILE
    s_ref[1] = jnp.zeros(s_ref.shape[1:], F32)
    mx_ref[...] = jnp.zeros(mx_ref.shape, F32)
    gt = gt_ref[...]

    def score_stage(i, slot):
        first_key = pl.multiple_of(jnp.maximum(i - 2, 0) * K_TILE, K_TILE)
        qcat = jnp.concatenate(_split_heads(q_ref[pl.ds(pl.multiple_of(i * tq, tq), tq), :]), axis=0)
        var = jnp.minimum(i, 2)
        s = (_dot_t(k_ref[pl.ds(first_key, B_KEYS), :], qcat)
             + jnp.concatenate([bias_ref[var, 0], bias_ref[var, 1]], axis=1))
        s_ref[slot] = s
        mx_ref[slot, 0:1, :] = jnp.max(s, axis=0, keepdims=True)

    def value_stage(i, slot):
        jb = jnp.maximum(i - 2, 0)
        p = jnp.exp2(s_ref[slot] - mx_ref[slot, 0:1, :]).astype(BF16)
        outs = []
        for n in range(2):
            acc = sum(_dot(vt_ref[jb + t][n * B_V_ROWS:(n + 1) * B_V_ROWS, :],
                           p[t * K_TILE:(t + 1) * K_TILE, n * tq:(n + 1) * tq]) for t in range(B_KEYS // K_TILE))
            o = acc[0:HEAD_DIM] * (1.0 / acc[HEAD_DIM:HEAD_DIM + 1])
            rs = lax.rsqrt(jnp.mean(o * o, axis=0, keepdims=True) + EPS)
            outs.append(o * rs * gt[n * HEAD_DIM:(n + 1) * HEAD_DIM, :])
        res_ref[jnp.where(i < 0, nq, i)] = jnp.concatenate(outs, axis=0).T.astype(BF16)

    def step(i, slot):
        score_stage(i, slot)
        value_stage(i - 1, 1 - slot)

    def steps(it, carry):
        for u in range(B_UNROLL):
            step(B_UNROLL * it + u, u % 2)
        return carry

    lax.fori_loop(0, nq // B_UNROLL, steps, 0)
    for i in range(nq // B_UNROLL * B_UNROLL, nq):
        step(i, i % 2)
    value_stage(nq - 1, (nq - 1) % 2)
    for blk in range(nq):
        o_ref[blk * tq:(blk + 1) * tq, :] = res_ref[blk]


def _attn_b(proj, vt, bias_tiles, gn_t, bsz, seq):
    nq = seq // Q_TILE
    nk = seq // K_TILE
    pairs = B_HEADS // 2
    return pl.pallas_call(
        functools.partial(_attn_b_kernel, nq=nq),
        out_shape=jax.ShapeDtypeStruct((bsz * seq, pairs * LANES), BF16),
        grid=(bsz, pairs),
        in_specs=[
            pl.BlockSpec((seq, LANES), lambda b, h: (b, B_Q_BLK + h)),
            pl.BlockSpec((seq, LANES), lambda b, h: (b, B_K_BLK + h)),
            pl.BlockSpec((nk, None, 2 * B_V_ROWS, K_TILE), lambda b, h: (b, h, 0, 0)),
            pl.BlockSpec((3, 2, B_KEYS, Q_TILE), lambda b, h: (0, h, 0, 0)),
            pl.BlockSpec((LANES, Q_TILE), lambda b, h: (h, 0)),
        ],
        out_specs=pl.BlockSpec((seq, LANES), lambda b, h: (b, h)),
        scratch_shapes=[pltpu.VMEM((2, B_KEYS, 2 * Q_TILE), F32), pltpu.VMEM((2, 8, 2 * Q_TILE), F32),
                        pltpu.VMEM((nq + 1, Q_TILE, LANES), BF16)],
        compiler_params=_params(2),
        name="attn_b",
    )(proj, proj, vt, bias_tiles, gn_t)


def _softplus2(z):
    return jnp.maximum(z, 0.0) + jnp.log2(1.0 + jnp.exp2(-jnp.abs(z)))


def _attn_c_kernel(q_ref, k_ref, vt_ref, mask_ref, gt_ref, o_ref, *, n_sub):
    tq, tk = Q_TILE, K_TILE
    tri_r = lax.broadcasted_iota(jnp.int32, (tk, tk), 0)
    tri_c = lax.broadcasted_iota(jnp.int32, (tk, tk), 1)
    later = jnp.where(tri_c > tri_r, 1.0, 0.0).astype(BF16)

    def suffix_sums(x):
        return _dot(later, x.astype(BF16))

    def values(vt, a):
        return tuple(_dot(vt[n * HEAD_DIM:(n + 1) * HEAD_DIM, :], a[:, n * tq:(n + 1) * tq]) for n in range(2))

    subs = range(n_sub)
    qis = [pl.program_id(2) * n_sub + u for u in subs]
    qcats = [jnp.concatenate(_split_heads(q_ref[u * tq:(u + 1) * tq, :]), axis=0) for u in subs]

    j0s = [jnp.maximum(qi - 1, 0) for qi in qis]
    zs = [_dot_t(k_ref[pl.ds(pl.multiple_of(j0 * tk, tk), 2 * tk), :], qcat) + mask_ref[jnp.minimum(qi, 1)]
          for j0, qi, qcat in zip(j0s, qis, qcats)]
    sps = [_softplus2(z) for z in zs]
    gone = [jnp.concatenate([suffix_sums(sp[:tk]) + jnp.sum(sp[tk:], axis=0, keepdims=True),
                             suffix_sums(sp[tk:])], axis=0) for sp in sps]
    a_s = [jnp.exp2((z - sp) - g).astype(BF16) for z, sp, g in zip(zs, sps, gone)]
    cs = [jnp.sum(sp, axis=0, keepdims=True) for sp in sps]
    accs = [tuple(x + y for x, y in zip(values(vt_ref[j0], a[:tk]), values(vt_ref[j0 + 1], a[tk:])))
            for j0, a in zip(j0s, a_s)]

    gt = gt_ref[...]
    for u in subs:
        j0, qcat = j0s[u], qcats[u]

        def live(state):
            t, c, _, _ = state
            return (t < j0) & (jnp.min(c) < -C_EXIT)

        def sweep(state):
            t, c, acc0, acc1 = state
            kj = j0 - 1 - t
            z = _dot_t(k_ref[pl.ds(pl.multiple_of(kj * tk, tk), tk), :], qcat)
            sp = _softplus2(z)
            a = jnp.exp2((z - sp) - suffix_sums(sp) - c).astype(BF16)
            d0, d1 = values(vt_ref[kj], a)
            return t + 1, c + jnp.sum(sp, axis=0, keepdims=True), acc0 + d0, acc1 + d1

        _, _, acc0, acc1 = lax.while_loop(live, sweep, (jnp.int32(0), cs[u], accs[u][0], accs[u][1]))
        outs = []
        for n, o in enumerate((acc0, acc1)):
            rs = lax.rsqrt(jnp.mean(o * o, axis=0, keepdims=True) + EPS)
            outs.append(o * rs * gt[n * HEAD_DIM:(n + 1) * HEAD_DIM, :])
        o_ref[u * tq:(u + 1) * tq, :] = jnp.concatenate(outs, axis=0).T.astype(BF16)


def _causal_bias_c():
    key = jnp.arange(2 * K_TILE)[:, None]
    query = jnp.arange(2 * Q_TILE)[None, :] % Q_TILE
    return jnp.stack([jnp.where(key < query + off, 0.0, NEG) for off in (0, K_TILE)]).astype(F32)


def _attn_c(proj, vt, gn_t, bsz, seq, n_sub=4):
    nq = seq // (Q_TILE * n_sub)
    nk = seq // K_TILE
    pairs = C_HEADS // 2
    return pl.pallas_call(
        functools.partial(_attn_c_kernel, n_sub=n_sub),
        out_shape=jax.ShapeDtypeStruct((bsz * seq, pairs * LANES), BF16),
        grid=(bsz, pairs, nq),
        in_specs=[
            pl.BlockSpec((n_sub * Q_TILE, LANES), lambda b, h, i: (b * nq + i, C_Q_BLK + h)),
            pl.BlockSpec((seq, LANES), lambda b, h, i: (b, C_K_BLK + h)),
            pl.BlockSpec((nk, None, LANES, K_TILE), lambda b, h, i: (b, h, 0, 0)),
            pl.BlockSpec((2, 2 * K_TILE, 2 * Q_TILE), lambda b, h, i: (0, 0, 0)),
            pl.BlockSpec((LANES, Q_TILE), lambda b, h, i: (h, 0)),
        ],
        out_specs=pl.BlockSpec((n_sub * Q_TILE, LANES), lambda b, h, i: (b * nq + i, h)),
        compiler_params=_params(3),
        name="attn_c",
    )(proj, proj, vt, _causal_bias_c(), gn_t)


def _outmlp_kernel(x_ref, oa_ref, ob_ref, oc_ref, wo_ref, g1_ref, g2_ref, wu_ref, wd_ref, g3_ref, o_ref, acc_ref,
                   *, f_chunk, n_part):
    wa, wb = oa_ref.shape[1], ob_ref.shape[1]
    rows = x_ref.shape[0] // n_part
    parts = [slice(r * rows, (r + 1) * rows) for r in range(n_part)]
    d_ff = wu_ref.shape[1]
    ys = [_dot(oa_ref[p, :], wo_ref[0:wa, :]) + _dot(ob_ref[p, :], wo_ref[wa:wa + wb, :])
          + _dot(oc_ref[p, :], wo_ref[wa + wb:, :]) for p in parts]
    x1s = [x_ref[p, :] + _rms(y, g1_ref[...]) for p, y in zip(parts, ys)]
    hs = [_rms(x1, g2_ref[...]).astype(BF16) for x1 in x1s]
    for p, x1, h in zip(parts, x1s, hs):
        for f0 in range(0, d_ff, f_chunk):
            u = jnp.maximum(_dot(h, wu_ref[:, f0:f0 + f_chunk]), 0.0)
            part = _dot((u * u).astype(BF16), wd_ref[f0:f0 + f_chunk, :])
            if f0 == 0:
                acc_ref[p, :] = part
            else:
                acc_ref[p, :] += part
        o_ref[p, :] = x1 + _rms(acc_ref[p, :], g3_ref[...])


def _outmlp(x, oa, ob, oc, w_out, g_post, g_pre, w_up, w_down, g_post2, tm=1024, f_chunk=1024, n_part=4):
    m, d = x.shape
    d_ff = w_up.shape[1]
    row = lambda w: pl.BlockSpec((tm, w), lambda i: (i, 0))
    full = lambda a, b: pl.BlockSpec((a, b), lambda i: (0, 0), pipeline_mode=pl.Buffered(1))
    return pl.pallas_call(
        functools.partial(_outmlp_kernel, f_chunk=f_chunk, n_part=n_part),
        out_shape=jax.ShapeDtypeStruct((m, d), F32),
        grid=(m // tm,),
        in_specs=[
            row(d), row(oa.shape[1]), row(ob.shape[1]), row(oc.shape[1]),
            full(d, d), full(1, d), full(1, d), full(d, d_ff), full(d_ff, d), full(1, d),
        ],
        out_specs=row(d),
        scratch_shapes=[pltpu.VMEM((tm, d), F32)],
        compiler_params=_params(1),
        name="outproj_mlp",
    )(x, oa, ob, oc, w_out, g_post, g_pre, w_up, w_down, g_post2)


def _rope_tables(seq):
    pos = jnp.arange(seq, dtype=F32)
    inv_freq = ROPE_THETA ** (-jnp.arange(0, ROPE_DIM, 2, dtype=F32) / ROPE_DIM)
    ang = pos[:, None] * inv_freq[None, :]
    cos, sin = jnp.cos(ang), jnp.sin(ang)
    half = ROPE_DIM // 2
    d = jnp.arange(LANES) % HEAD_DIM
    f = d % half
    rc = jnp.where(d[None, :] < ROPE_DIM, cos[:, f], 1.0)
    rs1 = jnp.where(d[None, :] < half, -sin[:, f], 0.0)
    rs2 = jnp.where((d[None, :] >= half) & (d[None, :] < ROPE_DIM), sin[:, f], 0.0)
    return rc.astype(F32), rs1.astype(F32), rs2.astype(F32)


def _q_scale_columns():
    blk = jnp.arange(D_IN) // LANES
    is_q = (blk < A_K_BLK) | ((blk >= B_Q_BLK) & (blk < B_K_BLK)) | ((blk >= C_Q_BLK) & (blk < C_K_BLK))
    return jnp.where(is_q, HEAD_DIM ** -0.5 * math.log2(math.e), 1.0).astype(F32)


def _gain_t(g, n_blk, width):
    return jnp.broadcast_to(g.reshape(n_blk * LANES, 1), (n_blk * LANES, width)).astype(F32)


def kernel(x, norm_pre_mix, w_in, lam_q1, lam_k1, lam_q2, lam_k2, subln_a, rel_bias, gn_b, gn_c, w_out,
           norm_post_mix, norm_pre_mlp, w_up, w_down, norm_post_mlp):
    bsz, seq, d = x.shape
    depth = w_in.shape[0]
    xf = x.reshape(bsz * seq, d)
    rc, rs1, rs2 = _rope_tables(seq)
    q_scale = _q_scale_columns()
    row = lambda v: v.reshape(1, -1)
    for l in range(depth):
        lam_init = 0.8 - 0.6 * math.exp(-0.3 * l)
        w_in_l = (w_in[l] * q_scale[None, :]).astype(BF16)
        proj, vt_a, vt_b, vt_c = _inproj(xf, row(norm_pre_mix[l]), w_in_l, rc, rs1, rs2, seq)
        lam_vecs = jnp.stack([lam_q1[l], lam_k1[l], lam_q2[l], lam_k2[l]]).astype(F32)
        oa = _attn_a(proj, vt_a, lam_vecs, _gain_t(subln_a[l], 1, A_TILE), lam_init, bsz, seq)
        ob = _attn_b(proj, vt_b, _bias_tiles(rel_bias[l]), _gain_t(gn_b[l], B_HEADS // 2, Q_TILE), bsz, seq)
        oc = _attn_c(proj, vt_c, _gain_t(gn_c[l], C_HEADS // 2, Q_TILE), bsz, seq)
        xf = _outmlp(xf, oa, ob, oc, w_out[l].astype(BF16), row(norm_post_mix[l]), row(norm_pre_mlp[l]),
                     w_up[l].astype(BF16), w_down[l].astype(BF16), row(norm_post_mlp[l]))
    return xf.reshape(bsz, seq, d)
```

```python
import functools
import math

import jax
import jax.numpy as jnp
from jax import lax
from jax.experimental import pallas as pl
from jax.experimental.pallas import tpu as pltpu

F32 = jnp.float32
BF16 = jnp.bfloat16

EPS = 1e-6
ROPE_THETA = 500000.0
HEAD_DIM = 64
ROPE_DIM = HEAD_DIM // 4
CHUNK = 64
LANES = 128
A_HEADS = 4
B_HEADS = 4
C_HEADS = 4
B_LEFT_CHUNKS = 8
REL_CLIP = 128
NEG = -1e30
M_RESET = -5e29
SP_SWITCH = 64.0
C_EXIT = -160.0

A_Q_BLK, A_K_BLK, A_V_BLK = 0, 4, 8
B_Q_BLK, B_K_BLK, B_V_BLK = 12, 14, 16
C_Q_BLK, C_K_BLK, C_V_BLK = 18, 20, 22
D_IN = 24 * LANES

Q_TILE = 256
K_TILE = 256
A_TILE = 512
A_UNROLL = 4
B_UNROLL = 4
V_ROWS = LANES + 16
B_KEYS = 3 * K_TILE
B_TABLE_W = B_KEYS + 2 * K_TILE
B_V_ROWS = HEAD_DIM + 16

VMEM_LIMIT = 56 * 1024 * 1024


def _params(n_axes, vmem=VMEM_LIMIT):
    return pltpu.CompilerParams(dimension_semantics=("arbitrary",) * n_axes, vmem_limit_bytes=vmem)


def _rms(x, g):
    return x * lax.rsqrt(jnp.mean(x * x, axis=-1, keepdims=True) + EPS) * g


def _dot_t(a, b):
    return lax.dot_general(a, b, (((1,), (1,)), ((), ())), preferred_element_type=F32)


def _dot(a, b):
    return jnp.dot(a, b, preferred_element_type=F32)


def _split_heads(q):
    lane = lax.broadcasted_iota(jnp.int32, q.shape, 1)
    zero = jnp.zeros_like(q)
    return jnp.where(lane < HEAD_DIM, q, zero), jnp.where(lane >= HEAD_DIM, q, zero)


def _inproj_kernel(x_ref, g_ref, w_ref, qs_ref, rc_ref, rs1_ref, rs2_ref, o_ref, vta_ref, vtb_ref, vtc_ref, *, n_chunk):
    h = _rms(x_ref[...], g_ref[...]).astype(BF16)
    rc, rs1, rs2 = rc_ref[...], rs1_ref[...], rs2_ref[...]
    tm = x_ref.shape[0]
    pad_rows = V_ROWS - LANES
    ones_rows = jnp.where(lax.broadcasted_iota(jnp.int32, (pad_rows, A_TILE), 0) == 0, 1.0, 0.0).astype(BF16)
    ones_rows_k = jnp.where(lax.broadcasted_iota(jnp.int32, (pad_rows, K_TILE), 0) == 0, 1.0, 0.0).astype(BF16)
    kt = tm // K_TILE
    for n0 in range(0, D_IN, n_chunk):
        y = _dot(h, w_ref[:, n0:n0 + n_chunk].astype(BF16))
        for j in range(n_chunk // LANES):
            blk = n0 // LANES + j
            yj = y[:, j * LANES:(j + 1) * LANES]
            if blk < A_K_BLK or B_Q_BLK <= blk < B_K_BLK or C_Q_BLK <= blk < C_K_BLK:
                yj = yj * qs_ref[:, blk * LANES:(blk + 1) * LANES]
            if blk < A_V_BLK:
                yj = yj * rc + pltpu.roll(yj, LANES - ROPE_DIM // 2, 1) * rs1 + pltpu.roll(yj, ROPE_DIM // 2, 1) * rs2
            o_ref[:, n0 + j * LANES:n0 + (j + 1) * LANES] = yj.astype(BF16)
            if A_V_BLK <= blk < B_Q_BLK:
                yt = yj.T.astype(BF16)
                for t in range(tm // A_TILE):
                    vta_ref[t, blk - A_V_BLK, 0:LANES, :] = yt[:, t * A_TILE:(t + 1) * A_TILE]
                    vta_ref[t, blk - A_V_BLK, LANES:V_ROWS, :] = ones_rows
            if B_V_BLK <= blk < C_Q_BLK:
                yt = yj.T.astype(BF16)
                for t in range(kt):
                    for n in range(2):
                        r0 = n * B_V_ROWS
                        vtb_ref[t, blk - B_V_BLK, r0:r0 + HEAD_DIM, :] = (
                            yt[n * HEAD_DIM:(n + 1) * HEAD_DIM, t * K_TILE:(t + 1) * K_TILE])
                        vtb_ref[t, blk - B_V_BLK, r0 + HEAD_DIM:r0 + B_V_ROWS, :] = ones_rows_k
            if blk >= C_V_BLK:
                yt = yj.T.astype(BF16)
                for t in range(kt):
                    vtc_ref[t, blk - C_V_BLK] = yt[:, t * K_TILE:(t + 1) * K_TILE]


def _inproj(x, g, w, q_scale, rc, rs1, rs2, seq, tm=2 * A_TILE, n_chunk=512):
    m, d = x.shape
    pos_blocks = seq // tm
    rope_spec = pl.BlockSpec((tm, LANES), lambda i: (i % pos_blocks, 0))
    kt = tm // K_TILE
    return pl.pallas_call(
        functools.partial(_inproj_kernel, n_chunk=n_chunk),
        out_shape=(jax.ShapeDtypeStruct((m, D_IN), BF16),
                   jax.ShapeDtypeStruct((m // A_TILE, A_HEADS, V_ROWS, A_TILE), BF16),
                   jax.ShapeDtypeStruct((m // K_TILE, B_HEADS // 2, 2 * B_V_ROWS, K_TILE), BF16),
                   jax.ShapeDtypeStruct((m // K_TILE, C_HEADS // 2, LANES, K_TILE), BF16)),
        grid=(m // tm,),
        in_specs=[
            pl.BlockSpec((tm, d), lambda i: (i, 0)),
            pl.BlockSpec((1, d), lambda i: (0, 0)),
            pl.BlockSpec((d, D_IN), lambda i: (0, 0), pipeline_mode=pl.Buffered(1)),
            pl.BlockSpec((1, D_IN), lambda i: (0, 0)),
            rope_spec, rope_spec, rope_spec,
        ],
        out_specs=(pl.BlockSpec((tm, D_IN), lambda i: (i, 0)),
                   pl.BlockSpec((tm // A_TILE, A_HEADS, V_ROWS, A_TILE), lambda i: (i, 0, 0, 0)),
                   pl.BlockSpec((kt, B_HEADS // 2, 2 * B_V_ROWS, K_TILE), lambda i: (i, 0, 0, 0)),
                   pl.BlockSpec((kt, C_HEADS // 2, LANES, K_TILE), lambda i: (i, 0, 0, 0))),
        compiler_params=_params(1),
        name="inproj",
    )(x, g, w, q_scale, rc, rs1, rs2)


def _attn_a_kernel(q_ref, k_ref, vt_ref, lam_ref, gt_ref, o_ref, qx_ref, m_ref, mx_ref, s_ref, acc_ref,
                   *, lam_init, nq):
    tq = A_TILE
    n_chunks = tq // CHUNK
    fr = lax.broadcasted_iota(jnp.int32, (2 * tq, LANES), 0)
    fc = lax.broadcasted_iota(jnp.int32, (2 * tq, LANES), 1)
    q_feat = jnp.where((fc < n_chunks) & ((fr % tq) // CHUNK == fc), 1.0, 0.0).astype(BF16)
    q_feat_t = q_feat.astype(F32).T.astype(BF16)
    for blk in range(nq):
        q0, q1 = _split_heads(q_ref[blk * tq:(blk + 1) * tq, :])
        qx_ref[blk, 0:LANES, 0:tq] = q0.astype(F32).T.astype(BF16)
        qx_ref[blk, 0:LANES, tq:2 * tq] = q1.astype(F32).T.astype(BF16)
        qx_ref[blk, LANES:2 * LANES, :] = q_feat_t
    kr = lax.broadcasted_iota(jnp.int32, (tq, LANES), 0)
    kc_ = lax.broadcasted_iota(jnp.int32, (tq, LANES), 1)
    k_feat = jnp.where((kc_ < n_chunks) & (kr // CHUNK > kc_), NEG, 0.0).astype(BF16)
    k_nofeat = jnp.zeros_like(k_feat)

    m_ref[...] = jnp.full(m_ref.shape, M_RESET, F32)
    mx_ref[...] = jnp.full(mx_ref.shape, NEG, F32)
    s_ref[1] = jnp.full(s_ref.shape[1:], NEG, F32)

    @pl.when((pl.program_id(0) == 0) & (pl.program_id(1) == 0))
    def _():
        acc_ref[...] = jnp.zeros(acc_ref.shape, F32)

    def score_stage(qa, ka, slot):
        k = k_ref[pl.ds(pl.multiple_of(ka * tq, tq), tq), :]
        kx = jnp.concatenate([k, jnp.where(ka == qa, k_feat, k_nofeat)], axis=1)
        s_new = _dot(kx, qx_ref[qa])
        s_ref[slot] = s_new
        mx_ref[slot, 0:1, :] = jnp.max(s_new, axis=0, keepdims=True)

    def value_stage(qb, kb, slot):
        m_old = jnp.where(kb == 0, M_RESET, m_ref[0:1, :])
        m_new = jnp.maximum(m_old, mx_ref[slot, 0:1, :])
        m_ref[0:1, :] = m_new
        p = jnp.exp2(s_ref[slot] - m_new).astype(BF16)
        acc_ref[qb] = jnp.exp2(m_old - m_new) * acc_ref[qb] + _dot(vt_ref[jnp.maximum(kb, 0)], p)

    def step(st, slot):
        qa, ka, qb, kb = st
        score_stage(qa, ka, slot)
        value_stage(qb, kb, 1 - slot)
        last = ka == qa
        return (jnp.where(last, qa + 1, qa), jnp.where(last, 0, ka + 1), qa, ka)

    def steps(i, st):
        for u in range(A_UNROLL):
            st = step(st, u % 2)
        return st

    n_tiles = nq * (nq + 1) // 2
    zero, neg1 = jnp.int32(0), jnp.int32(-1)
    st = lax.fori_loop(0, n_tiles // A_UNROLL, steps, (zero, zero, zero, neg1))
    for f in range(n_tiles // A_UNROLL * A_UNROLL, n_tiles):
        st = step(st, f % 2)
    value_stage(st[2], st[3], (n_tiles - 1) % 2)

    lv = lam_ref[...]
    lam = (jnp.exp(jnp.sum(lv[0:1] * lv[1:2], axis=-1, keepdims=True))
           - jnp.exp(jnp.sum(lv[2:3] * lv[3:4], axis=-1, keepdims=True)) + lam_init)
    for blk in range(nq):
        acc = acc_ref[blk]
        on = acc[0:LANES] * (1.0 / acc[LANES:LANES + 1])
        o = on[:, :tq] - lam * on[:, tq:]
        o = o * lax.rsqrt(jnp.mean(o * o, axis=0, keepdims=True) + EPS) * gt_ref[...] * (1.0 - lam_init)
        o_ref[blk * tq:(blk + 1) * tq, :] = o.T.astype(BF16)


def _attn_a(proj, vt, lam_vecs, subln_gt, lam_init, bsz, seq):
    nq = seq // A_TILE
    return pl.pallas_call(
        functools.partial(_attn_a_kernel, lam_init=lam_init, nq=nq),
        out_shape=jax.ShapeDtypeStruct((bsz * seq, A_HEADS * LANES), BF16),
        grid=(bsz, A_HEADS),
        in_specs=[
            pl.BlockSpec((seq, LANES), lambda b, h: (b, A_Q_BLK + h)),
            pl.BlockSpec((seq, LANES), lambda b, h: (b, A_K_BLK + h)),
            pl.BlockSpec((nq, None, V_ROWS, A_TILE), lambda b, h: (b, h, 0, 0)),
            pl.BlockSpec((4, HEAD_DIM), lambda b, h: (0, 0)),
            pl.BlockSpec((LANES, A_TILE), lambda b, h: (0, 0)),
        ],
        out_specs=pl.BlockSpec((seq, LANES), lambda b, h: (b, h)),
        scratch_shapes=[pltpu.VMEM((nq, 2 * LANES, 2 * A_TILE), BF16), pltpu.VMEM((8, 2 * A_TILE), F32),
                        pltpu.VMEM((2, 8, 2 * A_TILE), F32),
                        pltpu.VMEM((2, A_TILE, 2 * A_TILE), F32),
                        pltpu.VMEM((nq, V_ROWS, 2 * A_TILE), F32)],
        compiler_params=_params(2),
        name="attn_a",
    )(proj, proj, vt, lam_vecs, subln_gt)


def _bias_kernel(base_ref, o_ref):
    win = 3 * LANES
    base = base_ref[...] * math.log2(math.e)
    spread = jnp.broadcast_to(base, (CHUNK, win + LANES))
    ws = tuple(pltpu.roll(spread, shift, 1, stride=1, stride_axis=0)[:, 0:win] for shift in (0, CHUNK))
    far = base[:, win:win + 1]

    shape = (Q_TILE, B_TABLE_W)
    row = lax.broadcasted_iota(jnp.int32, shape, 0)
    col = lax.broadcasted_iota(jnp.int32, shape, 1)
    kc = col // CHUNK - row // CHUNK
    band = (kc >= 0) & (kc <= B_LEFT_CHUNKS)

    win_start = (3 * LANES, 3 * LANES, 4 * LANES, 4 * LANES)
    rows = []
    for c in range(Q_TILE // CHUNK):
        pieces = []
        for j in range(B_TABLE_W // LANES):
            lo = j * LANES
            bnd = band[c * CHUNK:(c + 1) * CHUNK, lo:lo + LANES]
            if win_start[c] <= lo < win_start[c] + win:
                off = lo - win_start[c]
                pieces.append(jnp.where(bnd, ws[c % 2][:, off:off + LANES], NEG))
            else:
                pieces.append(jnp.where(bnd, far, NEG))
        rows.append(jnp.concatenate(pieces, axis=1))
    table = jnp.concatenate(rows, axis=0)
    for v in range(3):
        c0 = (2 - v) * K_TILE
        o_ref[v] = table[:, c0:c0 + B_KEYS].T


def _bias_tiles(rel_bias):
    heads = rel_bias.shape[0]
    win = 3 * LANES
    base = jnp.concatenate([rel_bias[:, ::-1],
                            jnp.broadcast_to(rel_bias[:, :1], (heads, win - 2 * REL_CLIP - 1)),
                            jnp.broadcast_to(rel_bias[:, -1:], (heads, LANES))], axis=1).astype(F32)
    return pl.pallas_call(
        _bias_kernel,
        out_shape=jax.ShapeDtypeStruct((3, heads, B_KEYS, Q_TILE), F32),
        grid=(heads,),
        in_specs=[pl.BlockSpec((None, 1, win + LANES), lambda h: (h, 0, 0))],
        out_specs=pl.BlockSpec((3, None, B_KEYS, Q_TILE), lambda h: (0, h, 0, 0)),
        compiler_params=_params(1),
        name="bias_tiles",
    )(base.reshape(heads, 1, win + LANES))


def _attn_b_kernel(q_ref, k_ref, vt_ref, bias_ref, gt_ref, o_ref, s_ref, mx_ref, res_ref, *, nq):
    tq = Q_TILE
    s_ref[1] = jnp.zeros(s_ref.shape[1:], F32)
    mx_ref[...] = jnp.zeros(mx_ref.shape, F32)
    gt = gt_ref[...]

    def score_stage(i, slot):
        first_key = pl.multiple_of(jnp.maximum(i - 2, 0) * K_TILE, K_TILE)
        qcat = jnp.concatenate(_split_heads(q_ref[pl.ds(pl.multiple_of(i * tq, tq), tq), :]), axis=0)
        var = jnp.minimum(i, 2)
        s = (_dot_t(k_ref[pl.ds(first_key, B_KEYS), :], qcat)
             + jnp.concatenate([bias_ref[var, 0], bias_ref[var, 1]], axis=1))
        s_ref[slot] = s
        mx_ref[slot, 0:1, :] = jnp.max(s, axis=0, keepdims=True)

    def value_stage(i, slot):
        jb = jnp.maximum(i - 2, 0)
        p = jnp.exp2(s_ref[slot] - mx_ref[slot, 0:1, :]).astype(BF16)
        outs = []
        for n in range(2):
            acc = sum(_dot(vt_ref[jb + t][n * B_V_ROWS:(n + 1) * B_V_ROWS, :],
                           p[t * K_TILE:(t + 1) * K_TILE, n * tq:(n + 1) * tq]) for t in range(B_KEYS // K_TILE))
            o = acc[0:HEAD_DIM] * (1.0 / acc[HEAD_DIM:HEAD_DIM + 1])
            rs = lax.rsqrt(jnp.mean(o * o, axis=0, keepdims=True) + EPS)
            outs.append(o * rs * gt[n * HEAD_DIM:(n + 1) * HEAD_DIM, :])
        res_ref[jnp.where(i < 0, nq, i)] = jnp.concatenate(outs, axis=0).T.astype(BF16)

    def step(i, slot):
        score_stage(i, slot)
        value_stage(i - 1, 1 - slot)

    def steps(it, carry):
        for u in range(B_UNROLL):
            step(B_UNROLL * it + u, u % 2)
        return carry

    lax.fori_loop(0, nq // B_UNROLL, steps, 0)
    for i in range(nq // B_UNROLL * B_UNROLL, nq):
        step(i, i % 2)
    value_stage(nq - 1, (nq - 1) % 2)
    for blk in range(nq):
        o_ref[blk * tq:(blk + 1) * tq, :] = res_ref[blk]


def _attn_b(proj, vt, bias_tiles, gn_t, bsz, seq):
    nq = seq // Q_TILE
    nk = seq // K_TILE
    pairs = B_HEADS // 2
    return pl.pallas_call(
        functools.partial(_attn_b_kernel, nq=nq),
        out_shape=jax.ShapeDtypeStruct((bsz * seq, pairs * LANES), BF16),
        grid=(bsz, pairs),
        in_specs=[
            pl.BlockSpec((seq, LANES), lambda b, h: (b, B_Q_BLK + h)),
            pl.BlockSpec((seq, LANES), lambda b, h: (b, B_K_BLK + h)),
            pl.BlockSpec((nk, None, 2 * B_V_ROWS, K_TILE), lambda b, h: (b, h, 0, 0)),
            pl.BlockSpec((3, 2, B_KEYS, Q_TILE), lambda b, h: (0, h, 0, 0)),
            pl.BlockSpec((LANES, Q_TILE), lambda b, h: (h, 0)),
        ],
        out_specs=pl.BlockSpec((seq, LANES), lambda b, h: (b, h)),
        scratch_shapes=[pltpu.VMEM((2, B_KEYS, 2 * Q_TILE), F32), pltpu.VMEM((2, 8, 2 * Q_TILE), F32),
                        pltpu.VMEM((nq + 1, Q_TILE, LANES), BF16)],
        compiler_params=_params(2),
        name="attn_b",
    )(proj, proj, vt, bias_tiles, gn_t)


def _softplus2(z):
    return jnp.maximum(z, jnp.log2(1.0 + jnp.exp2(jnp.minimum(z, SP_SWITCH))))


def _attn_c_kernel(q_ref, k_ref, vt_ref, mask_ref, gt_ref, o_ref, *, n_sub):
    tq, tk = Q_TILE, K_TILE
    tri_r = lax.broadcasted_iota(jnp.int32, (tk, tk), 0)
    tri_c = lax.broadcasted_iota(jnp.int32, (tk, tk), 1)
    later = jnp.where(tri_c > tri_r, 1.0, 0.0).astype(BF16)

    def suffix_sums(x):
        return _dot(later, x.astype(BF16))

    def values(vt, a):
        return tuple(_dot(vt[n * HEAD_DIM:(n + 1) * HEAD_DIM, :], a[:, n * tq:(n + 1) * tq]) for n in range(2))

    subs = range(n_sub)
    qis = [pl.program_id(2) * n_sub + u for u in subs]
    qcats = [jnp.concatenate(_split_heads(q_ref[u * tq:(u + 1) * tq, :]), axis=0) for u in subs]

    j0s = [jnp.maximum(qi - 1, 0) for qi in qis]
    zs = [_dot_t(k_ref[pl.ds(pl.multiple_of(j0 * tk, tk), 2 * tk), :], qcat) + mask_ref[jnp.minimum(qi, 1)]
          for j0, qi, qcat in zip(j0s, qis, qcats)]
    sps = [_softplus2(z) for z in zs]
    gone = [jnp.concatenate([suffix_sums(sp[:tk]) + jnp.sum(sp[tk:], axis=0, keepdims=True),
                             suffix_sums(sp[tk:])], axis=0) for sp in sps]
    a_s = [jnp.exp2((z - sp) - g).astype(BF16) for z, sp, g in zip(zs, sps, gone)]
    cs = [jnp.sum(sp, axis=0, keepdims=True) for sp in sps]
    accs = [tuple(x + y for x, y in zip(values(vt_ref[j0], a[:tk]), values(vt_ref[j0 + 1], a[tk:])))
            for j0, a in zip(j0s, a_s)]

    gt = gt_ref[...]
    for u in subs:
        j0, qcat = j0s[u], qcats[u]

        def live(state):
            t, c, _, _ = state
            return (t < j0) & (jnp.min(c) < -C_EXIT)

        def sweep(state):
            t, c, acc0, acc1 = state
            kj = j0 - 1 - t
            z = _dot_t(k_ref[pl.ds(pl.multiple_of(kj * tk, tk), tk), :], qcat)
            sp = _softplus2(z)
            a = jnp.exp2((z - sp) - suffix_sums(sp) - c).astype(BF16)
            d0, d1 = values(vt_ref[kj], a)
            return t + 1, c + jnp.sum(sp, axis=0, keepdims=True), acc0 + d0, acc1 + d1

        _, _, acc0, acc1 = lax.while_loop(live, sweep, (jnp.int32(0), cs[u], accs[u][0], accs[u][1]))
        outs = []
        for n, o in enumerate((acc0, acc1)):
            rs = lax.rsqrt(jnp.mean(o * o, axis=0, keepdims=True) + EPS)
            outs.append(o * rs * gt[n * HEAD_DIM:(n + 1) * HEAD_DIM, :])
        o_ref[u * tq:(u + 1) * tq, :] = jnp.concatenate(outs, axis=0).T.astype(BF16)


def _causal_bias_c():
    key = jnp.arange(2 * K_TILE)[:, None]
    query = jnp.arange(2 * Q_TILE)[None, :] % Q_TILE
    return jnp.stack([jnp.where(key < query + off, 0.0, NEG) for off in (0, K_TILE)]).astype(F32)


def _attn_c(proj, vt, gn_t, bsz, seq, n_sub=4):
    nq = seq // (Q_TILE * n_sub)
    nk = seq // K_TILE
    pairs = C_HEADS // 2
    return pl.pallas_call(
        functools.partial(_attn_c_kernel, n_sub=n_sub),
        out_shape=jax.ShapeDtypeStruct((bsz * seq, pairs * LANES), BF16),
        grid=(bsz, pairs, nq),
        in_specs=[
            pl.BlockSpec((n_sub * Q_TILE, LANES), lambda b, h, i: (b * nq + i, C_Q_BLK + h)),
            pl.BlockSpec((seq, LANES), lambda b, h, i: (b, C_K_BLK + h)),
            pl.BlockSpec((nk, None, LANES, K_TILE), lambda b, h, i: (b, h, 0, 0)),
            pl.BlockSpec((2, 2 * K_TILE, 2 * Q_TILE), lambda b, h, i: (0, 0, 0)),
            pl.BlockSpec((LANES, Q_TILE), lambda b, h, i: (h, 0)),
        ],
        out_specs=pl.BlockSpec((n_sub * Q_TILE, LANES), lambda b, h, i: (b * nq + i, h)),
        compiler_params=_params(3),
        name="attn_c",
    )(proj, proj, vt, _causal_bias_c(), gn_t)


def _outmlp_kernel(x_ref, oa_ref, ob_ref, oc_ref, wo_ref, g1_ref, g2_ref, wu_ref, wd_ref, g3_ref, o_ref, acc_ref,
                   *, f_chunk, n_part):
    wa, wb = oa_ref.shape[1], ob_ref.shape[1]
    rows = x_ref.shape[0] // n_part
    parts = [slice(r * rows, (r + 1) * rows) for r in range(n_part)]
    d_ff = wu_ref.shape[1]
    ys = [_dot(oa_ref[p, :], wo_ref[0:wa, :]) + _dot(ob_ref[p, :], wo_ref[wa:wa + wb, :])
          + _dot(oc_ref[p, :], wo_ref[wa + wb:, :]) for p in parts]
    x1s = [x_ref[p, :] + _rms(y, g1_ref[...]) for p, y in zip(parts, ys)]
    hs = [_rms(x1, g2_ref[...]).astype(BF16) for x1 in x1s]
    for p, x1, h in zip(parts, x1s, hs):
        for f0 in range(0, d_ff, f_chunk):
            u = jnp.maximum(_dot(h, wu_ref[:, f0:f0 + f_chunk]), 0.0)
            part = _dot((u * u).astype(BF16), wd_ref[f0:f0 + f_chunk, :])
            if f0 == 0:
                acc_ref[p, :] = part
            else:
                acc_ref[p, :] += part
        o_ref[p, :] = x1 + _rms(acc_ref[p, :], g3_ref[...])


def _outmlp(x, oa, ob, oc, w_out, g_post, g_pre, w_up, w_down, g_post2, tm=1024, f_chunk=1024, n_part=4):
    m, d = x.shape
    d_ff = w_up.shape[1]
    row = lambda w: pl.BlockSpec((tm, w), lambda i: (i, 0))
    full = lambda a, b: pl.BlockSpec((a, b), lambda i: (0, 0), pipeline_mode=pl.Buffered(1))
    return pl.pallas_call(
        functools.partial(_outmlp_kernel, f_chunk=f_chunk, n_part=n_part),
        out_shape=jax.ShapeDtypeStruct((m, d), F32),
        grid=(m // tm,),
        in_specs=[
            row(d), row(oa.shape[1]), row(ob.shape[1]), row(oc.shape[1]),
            full(d, d), full(1, d), full(1, d), full(d, d_ff), full(d_ff, d), full(1, d),
        ],
        out_specs=row(d),
        scratch_shapes=[pltpu.VMEM((tm, d), F32)],
        compiler_params=_params(1),
        name="outproj_mlp",
    )(x, oa, ob, oc, w_out, g_post, g_pre, w_up, w_down, g_post2)


def _rope_tables(seq):
    pos = jnp.arange(seq, dtype=F32)
    inv_freq = ROPE_THETA ** (-jnp.arange(0, ROPE_DIM, 2, dtype=F32) / ROPE_DIM)
    ang = pos[:, None] * inv_freq[None, :]
    cos, sin = jnp.cos(ang), jnp.sin(ang)
    half = ROPE_DIM // 2
    d = jnp.arange(LANES) % HEAD_DIM
    f = d % half
    rc = jnp.where(d[None, :] < ROPE_DIM, cos[:, f], 1.0)
    rs1 = jnp.where(d[None, :] < half, -sin[:, f], 0.0)
    rs2 = jnp.where((d[None, :] >= half) & (d[None, :] < ROPE_DIM), sin[:, f], 0.0)
    return rc.astype(F32), rs1.astype(F32), rs2.astype(F32)


def _q_scale_columns():
    blk = jnp.arange(D_IN) // LANES
    is_q = (blk < A_K_BLK) | ((blk >= B_Q_BLK) & (blk < B_K_BLK)) | ((blk >= C_Q_BLK) & (blk < C_K_BLK))
    return jnp.where(is_q, HEAD_DIM ** -0.5 * math.log2(math.e), 1.0).astype(F32)


def _gain_t(g, n_blk, width):
    return jnp.broadcast_to(g.reshape(n_blk * LANES, 1), (n_blk * LANES, width)).astype(F32)


def kernel(x, norm_pre_mix, w_in, lam_q1, lam_k1, lam_q2, lam_k2, subln_a, rel_bias, gn_b, gn_c, w_out,
           norm_post_mix, norm_pre_mlp, w_up, w_down, norm_post_mlp):
    bsz, seq, d = x.shape
    depth = w_in.shape[0]
    xf = x.reshape(bsz * seq, d)
    rc, rs1, rs2 = _rope_tables(seq)
    q_scale = _q_scale_columns()
    row = lambda v: v.reshape(1, -1)
    for l in range(depth):
        lam_init = 0.8 - 0.6 * math.exp(-0.3 * l)
        proj, vt_a, vt_b, vt_c = _inproj(xf, row(norm_pre_mix[l]), w_in[l], row(q_scale), rc, rs1, rs2, seq)
        lam_vecs = jnp.stack([lam_q1[l], lam_k1[l], lam_q2[l], lam_k2[l]]).astype(F32)
        oa = _attn_a(proj, vt_a, lam_vecs, _gain_t(subln_a[l], 1, A_TILE), lam_init, bsz, seq)
        ob = _attn_b(proj, vt_b, _bias_tiles(rel_bias[l]), _gain_t(gn_b[l], B_HEADS // 2, Q_TILE), bsz, seq)
        oc = _attn_c(proj, vt_c, _gain_t(gn_c[l], C_HEADS // 2, Q_TILE), bsz, seq)
        xf = _outmlp(xf, oa, ob, oc, w_out[l].astype(BF16), row(norm_post_mix[l]), row(norm_pre_mlp[l]),
                     w_up[l].astype(BF16), w_down[l].astype(BF16), row(norm_post_mlp[l]))
    return xf.reshape(bsz, seq, d)
```

```python
import functools
import math

import jax
import jax.numpy as jnp
from jax import lax
from jax.experimental import pallas as pl
from jax.experimental.pallas import tpu as pltpu

F32 = jnp.float32
BF16 = jnp.bfloat16

EPS = 1e-6
ROPE_THETA = 500000.0
HEAD_DIM = 64
ROPE_DIM = HEAD_DIM // 4
CHUNK = 64
LANES = 128
A_HEADS = 4
B_HEADS = 4
C_HEADS = 4
B_LEFT_CHUNKS = 8
REL_CLIP = 128
NEG = -1e30
M_RESET = -5e29
SP_SWITCH = 64.0
C_EXIT = -160.0

A_Q_BLK, A_K_BLK, A_V_BLK = 0, 4, 8
B_Q_BLK, B_K_BLK, B_V_BLK = 12, 14, 16
C_Q_BLK, C_K_BLK, C_V_BLK = 18, 20, 22
D_IN = 24 * LANES

Q_TILE = 256
K_TILE = 256
A_TILE = 512
A_UNROLL = 4
B_UNROLL = 4
V_ROWS = LANES + 16
B_KEYS = 3 * K_TILE
B_TABLE_W = B_KEYS + 2 * K_TILE
B_V_ROWS = HEAD_DIM + 16

VMEM_LIMIT = 56 * 1024 * 1024


def _params(n_axes, vmem=VMEM_LIMIT):
    return pltpu.CompilerParams(dimension_semantics=("arbitrary",) * n_axes, vmem_limit_bytes=vmem)


def _rms(x, g):
    return x * lax.rsqrt(jnp.mean(x * x, axis=-1, keepdims=True) + EPS) * g


def _dot_t(a, b):
    return lax.dot_general(a, b, (((1,), (1,)), ((), ())), preferred_element_type=F32)


def _dot(a, b):
    return jnp.dot(a, b, preferred_element_type=F32)


def _split_heads(q):
    lane = lax.broadcasted_iota(jnp.int32, q.shape, 1)
    zero = jnp.zeros_like(q)
    return jnp.where(lane < HEAD_DIM, q, zero), jnp.where(lane >= HEAD_DIM, q, zero)


def _inproj_kernel(x_ref, g_ref, w_ref, qs_ref, rc_ref, rs1_ref, rs2_ref, o_ref, vta_ref, vtb_ref, vtc_ref, *, n_chunk):
    h = _rms(x_ref[...], g_ref[...]).astype(BF16)
    rc, rs1, rs2 = rc_ref[...], rs1_ref[...], rs2_ref[...]
    tm = x_ref.shape[0]
    pad_rows = V_ROWS - LANES
    ones_rows = jnp.where(lax.broadcasted_iota(jnp.int32, (pad_rows, A_TILE), 0) == 0, 1.0, 0.0).astype(BF16)
    ones_rows_k = jnp.where(lax.broadcasted_iota(jnp.int32, (pad_rows, K_TILE), 0) == 0, 1.0, 0.0).astype(BF16)
    kt = tm // K_TILE
    for n0 in range(0, D_IN, n_chunk):
        y = _dot(h, w_ref[:, n0:n0 + n_chunk].astype(BF16))
        for j in range(n_chunk // LANES):
            blk = n0 // LANES + j
            yj = y[:, j * LANES:(j + 1) * LANES]
            if blk < A_K_BLK or B_Q_BLK <= blk < B_K_BLK or C_Q_BLK <= blk < C_K_BLK:
                yj = yj * qs_ref[:, blk * LANES:(blk + 1) * LANES]
            if blk < A_V_BLK:
                yj = yj * rc + pltpu.roll(yj, LANES - ROPE_DIM // 2, 1) * rs1 + pltpu.roll(yj, ROPE_DIM // 2, 1) * rs2
            o_ref[:, n0 + j * LANES:n0 + (j + 1) * LANES] = yj.astype(BF16)
            if A_V_BLK <= blk < B_Q_BLK:
                yt = yj.T.astype(BF16)
                for t in range(tm // A_TILE):
                    vta_ref[t, blk - A_V_BLK, 0:LANES, :] = yt[:, t * A_TILE:(t + 1) * A_TILE]
                    vta_ref[t, blk - A_V_BLK, LANES:V_ROWS, :] = ones_rows
            if B_V_BLK <= blk < C_Q_BLK:
                yt = yj.T.astype(BF16)
                for t in range(kt):
                    for n in range(2):
                        r0 = n * B_V_ROWS
                        vtb_ref[t, blk - B_V_BLK, r0:r0 + HEAD_DIM, :] = (
                            yt[n * HEAD_DIM:(n + 1) * HEAD_DIM, t * K_TILE:(t + 1) * K_TILE])
                        vtb_ref[t, blk - B_V_BLK, r0 + HEAD_DIM:r0 + B_V_ROWS, :] = ones_rows_k
            if blk >= C_V_BLK:
                yt = yj.T.astype(BF16)
                for t in range(kt):
                    vtc_ref[t, blk - C_V_BLK] = yt[:, t * K_TILE:(t + 1) * K_TILE]


def _inproj(x, g, w_all, layer, q_scale, rc, rs1, rs2, seq, tm=2 * A_TILE, n_chunk=512):
    m, d = x.shape
    pos_blocks = seq // tm
    rope_spec = pl.BlockSpec((tm, LANES), lambda i: (i % pos_blocks, 0))
    kt = tm // K_TILE
    return pl.pallas_call(
        functools.partial(_inproj_kernel, n_chunk=n_chunk),
        out_shape=(jax.ShapeDtypeStruct((m, D_IN), BF16),
                   jax.ShapeDtypeStruct((m // A_TILE, A_HEADS, V_ROWS, A_TILE), BF16),
                   jax.ShapeDtypeStruct((m // K_TILE, B_HEADS // 2, 2 * B_V_ROWS, K_TILE), BF16),
                   jax.ShapeDtypeStruct((m // K_TILE, C_HEADS // 2, LANES, K_TILE), BF16)),
        grid=(m // tm,),
        in_specs=[
            pl.BlockSpec((tm, d), lambda i: (i, 0)),
            pl.BlockSpec((1, d), lambda i: (0, 0)),
            pl.BlockSpec((None, d, D_IN), lambda i: (layer, 0, 0), pipeline_mode=pl.Buffered(1)),
            pl.BlockSpec((1, D_IN), lambda i: (0, 0)),
            rope_spec, rope_spec, rope_spec,
        ],
        out_specs=(pl.BlockSpec((tm, D_IN), lambda i: (i, 0)),
                   pl.BlockSpec((tm // A_TILE, A_HEADS, V_ROWS, A_TILE), lambda i: (i, 0, 0, 0)),
                   pl.BlockSpec((kt, B_HEADS // 2, 2 * B_V_ROWS, K_TILE), lambda i: (i, 0, 0, 0)),
                   pl.BlockSpec((kt, C_HEADS // 2, LANES, K_TILE), lambda i: (i, 0, 0, 0))),
        compiler_params=_params(1),
        name="inproj",
    )(x, g, w_all, q_scale, rc, rs1, rs2)


def _attn_a_kernel(q_ref, k_ref, vt_ref, lam_ref, gt_ref, o_ref, qx_ref, m_ref, mx_ref, s_ref, acc_ref,
                   *, lam_init, nq):
    tq = A_TILE
    n_chunks = tq // CHUNK
    fr = lax.broadcasted_iota(jnp.int32, (2 * tq, LANES), 0)
    fc = lax.broadcasted_iota(jnp.int32, (2 * tq, LANES), 1)
    q_feat = jnp.where((fc < n_chunks) & ((fr % tq) // CHUNK == fc), 1.0, 0.0).astype(BF16)
    q_feat_t = q_feat.astype(F32).T.astype(BF16)
    for blk in range(nq):
        q0, q1 = _split_heads(q_ref[blk * tq:(blk + 1) * tq, :])
        qx_ref[blk, 0:LANES, 0:tq] = q0.astype(F32).T.astype(BF16)
        qx_ref[blk, 0:LANES, tq:2 * tq] = q1.astype(F32).T.astype(BF16)
        qx_ref[blk, LANES:2 * LANES, :] = q_feat_t
    kr = lax.broadcasted_iota(jnp.int32, (tq, LANES), 0)
    kc_ = lax.broadcasted_iota(jnp.int32, (tq, LANES), 1)
    k_feat = jnp.where((kc_ < n_chunks) & (kr // CHUNK > kc_), NEG, 0.0).astype(BF16)
    k_nofeat = jnp.zeros_like(k_feat)

    m_ref[...] = jnp.full(m_ref.shape, M_RESET, F32)
    mx_ref[...] = jnp.full(mx_ref.shape, NEG, F32)
    s_ref[1] = jnp.full(s_ref.shape[1:], NEG, F32)

    @pl.when((pl.program_id(0) == 0) & (pl.program_id(1) == 0))
    def _():
        acc_ref[...] = jnp.zeros(acc_ref.shape, F32)

    def score_stage(qa, ka, slot):
        k = k_ref[pl.ds(pl.multiple_of(ka * tq, tq), tq), :]
        kx = jnp.concatenate([k, jnp.where(ka == qa, k_feat, k_nofeat)], axis=1)
        s_new = _dot(kx, qx_ref[qa])
        s_ref[slot] = s_new
        mx_ref[slot, 0:1, :] = jnp.max(s_new, axis=0, keepdims=True)

    def value_stage(qb, kb, slot):
        m_old = jnp.where(kb == 0, M_RESET, m_ref[0:1, :])
        m_new = jnp.maximum(m_old, mx_ref[slot, 0:1, :])
        m_ref[0:1, :] = m_new
        p = jnp.exp2(s_ref[slot] - m_new).astype(BF16)
        acc_ref[qb] = jnp.exp2(m_old - m_new) * acc_ref[qb] + _dot(vt_ref[jnp.maximum(kb, 0)], p)

    def step(st, slot):
        qa, ka, qb, kb = st
        score_stage(qa, ka, slot)
        value_stage(qb, kb, 1 - slot)
        last = ka == qa
        return (jnp.where(last, qa + 1, qa), jnp.where(last, 0, ka + 1), qa, ka)

    def steps(i, st):
        for u in range(A_UNROLL):
            st = step(st, u % 2)
        return st

    n_tiles = nq * (nq + 1) // 2
    zero, neg1 = jnp.int32(0), jnp.int32(-1)
    st = lax.fori_loop(0, n_tiles // A_UNROLL, steps, (zero, zero, zero, neg1))
    for f in range(n_tiles // A_UNROLL * A_UNROLL, n_tiles):
        st = step(st, f % 2)
    value_stage(st[2], st[3], (n_tiles - 1) % 2)

    lv = lam_ref[...]
    lam = (jnp.exp(jnp.sum(lv[0:1] * lv[1:2], axis=-1, keepdims=True))
           - jnp.exp(jnp.sum(lv[2:3] * lv[3:4], axis=-1, keepdims=True)) + lam_init)
    for blk in range(nq):
        acc = acc_ref[blk]
        on = acc[0:LANES] * (1.0 / acc[LANES:LANES + 1])
        o = on[:, :tq] - lam * on[:, tq:]
        o = o * lax.rsqrt(jnp.mean(o * o, axis=0, keepdims=True) + EPS) * gt_ref[...] * (1.0 - lam_init)
        o_ref[blk * tq:(blk + 1) * tq, :] = o.T.astype(BF16)


def _attn_a(proj, vt, lam_vecs, subln_gt, lam_init, bsz, seq):
    nq = seq // A_TILE
    return pl.pallas_call(
        functools.partial(_attn_a_kernel, lam_init=lam_init, nq=nq),
        out_shape=jax.ShapeDtypeStruct((bsz * seq, A_HEADS * LANES), BF16),
        grid=(bsz, A_HEADS),
        in_specs=[
            pl.BlockSpec((seq, LANES), lambda b, h: (b, A_Q_BLK + h)),
            pl.BlockSpec((seq, LANES), lambda b, h: (b, A_K_BLK + h)),
            pl.BlockSpec((nq, None, V_ROWS, A_TILE), lambda b, h: (b, h, 0, 0)),
            pl.BlockSpec((4, HEAD_DIM), lambda b, h: (0, 0)),
            pl.BlockSpec((LANES, A_TILE), lambda b, h: (0, 0)),
        ],
        out_specs=pl.BlockSpec((seq, LANES), lambda b, h: (b, h)),
        scratch_shapes=[pltpu.VMEM((nq, 2 * LANES, 2 * A_TILE), BF16), pltpu.VMEM((8, 2 * A_TILE), F32),
                        pltpu.VMEM((2, 8, 2 * A_TILE), F32),
                        pltpu.VMEM((2, A_TILE, 2 * A_TILE), F32),
                        pltpu.VMEM((nq, V_ROWS, 2 * A_TILE), F32)],
        compiler_params=_params(2),
        name="attn_a",
    )(proj, proj, vt, lam_vecs, subln_gt)


def _bias_kernel(base_ref, o_ref):
    win = 3 * LANES
    base = base_ref[...] * math.log2(math.e)
    spread = jnp.broadcast_to(base, (CHUNK, win + LANES))
    ws = tuple(pltpu.roll(spread, shift, 1, stride=1, stride_axis=0)[:, 0:win] for shift in (0, CHUNK))
    far = base[:, win:win + 1]

    shape = (Q_TILE, B_TABLE_W)
    row = lax.broadcasted_iota(jnp.int32, shape, 0)
    col = lax.broadcasted_iota(jnp.int32, shape, 1)
    kc = col // CHUNK - row // CHUNK
    band = (kc >= 0) & (kc <= B_LEFT_CHUNKS)

    win_start = (3 * LANES, 3 * LANES, 4 * LANES, 4 * LANES)
    rows = []
    for c in range(Q_TILE // CHUNK):
        pieces = []
        for j in range(B_TABLE_W // LANES):
            lo = j * LANES
            bnd = band[c * CHUNK:(c + 1) * CHUNK, lo:lo + LANES]
            if win_start[c] <= lo < win_start[c] + win:
                off = lo - win_start[c]
                pieces.append(jnp.where(bnd, ws[c % 2][:, off:off + LANES], NEG))
            else:
                pieces.append(jnp.where(bnd, far, NEG))
        rows.append(jnp.concatenate(pieces, axis=1))
    table = jnp.concatenate(rows, axis=0)
    for v in range(3):
        c0 = (2 - v) * K_TILE
        o_ref[v] = table[:, c0:c0 + B_KEYS].T


def _bias_tiles(rel_bias):
    heads = rel_bias.shape[0]
    win = 3 * LANES
    base = jnp.concatenate([rel_bias[:, ::-1],
                            jnp.broadcast_to(rel_bias[:, :1], (heads, win - 2 * REL_CLIP - 1)),
                            jnp.broadcast_to(rel_bias[:, -1:], (heads, LANES))], axis=1).astype(F32)
    return pl.pallas_call(
        _bias_kernel,
        out_shape=jax.ShapeDtypeStruct((3, heads, B_KEYS, Q_TILE), F32),
        grid=(heads,),
        in_specs=[pl.BlockSpec((None, 1, win + LANES), lambda h: (h, 0, 0))],
        out_specs=pl.BlockSpec((3, None, B_KEYS, Q_TILE), lambda h: (0, h, 0, 0)),
        compiler_params=_params(1),
        name="bias_tiles",
    )(base.reshape(heads, 1, win + LANES))


def _attn_b_kernel(q_ref, k_ref, vt_ref, bias_ref, gt_ref, o_ref, s_ref, mx_ref, res_ref, *, nq):
    tq = Q_TILE
    s_ref[1] = jnp.zeros(s_ref.shape[1:], F32)
    mx_ref[...] = jnp.zeros(mx_ref.shape, F32)
    gt = gt_ref[...]

    def score_stage(i, slot):
        first_key = pl.multiple_of(jnp.maximum(i - 2, 0) * K_TILE, K_TILE)
        qcat = jnp.concatenate(_split_heads(q_ref[pl.ds(pl.multiple_of(i * tq, tq), tq), :]), axis=0)
        var = jnp.minimum(i, 2)
        s = (_dot_t(k_ref[pl.ds(first_key, B_KEYS), :], qcat)
             + jnp.concatenate([bias_ref[var, 0], bias_ref[var, 1]], axis=1))
        s_ref[slot] = s
        mx_ref[slot, 0:1, :] = jnp.max(s, axis=0, keepdims=True)

    def value_stage(i, slot):
        jb = jnp.maximum(i - 2, 0)
        p = jnp.exp2(s_ref[slot] - mx_ref[slot, 0:1, :]).astype(BF16)
        outs = []
        for n in range(2):
            acc = sum(_dot(vt_ref[jb + t][n * B_V_ROWS:(n + 1) * B_V_ROWS, :],
                           p[t * K_TILE:(t + 1) * K_TILE, n * tq:(n + 1) * tq]) for t in range(B_KEYS // K_TILE))
            o = acc[0:HEAD_DIM] * (1.0 / acc[HEAD_DIM:HEAD_DIM + 1])
            rs = lax.rsqrt(jnp.mean(o * o, axis=0, keepdims=True) + EPS)
            outs.append(o * rs * gt[n * HEAD_DIM:(n + 1) * HEAD_DIM, :])
        res_ref[jnp.where(i < 0, nq, i)] = jnp.concatenate(outs, axis=0).T.astype(BF16)

    def step(i, slot):
        score_stage(i, slot)
        value_stage(i - 1, 1 - slot)

    def steps(it, carry):
        for u in range(B_UNROLL):
            step(B_UNROLL * it + u, u % 2)
        return carry

    lax.fori_loop(0, nq // B_UNROLL, steps, 0)
    for i in range(nq // B_UNROLL * B_UNROLL, nq):
        step(i, i % 2)
    value_stage(nq - 1, (nq - 1) % 2)
    for blk in range(nq):
        o_ref[blk * tq:(blk + 1) * tq, :] = res_ref[blk]


def _attn_b(proj, vt, bias_tiles, gn_t, bsz, seq):
    nq = seq // Q_TILE
    nk = seq // K_TILE
    pairs = B_HEADS // 2
    return pl.pallas_call(
        functools.partial(_attn_b_kernel, nq=nq),
        out_shape=jax.ShapeDtypeStruct((bsz * seq, pairs * LANES), BF16),
        grid=(bsz, pairs),
        in_specs=[
            pl.BlockSpec((seq, LANES), lambda b, h: (b, B_Q_BLK + h)),
            pl.BlockSpec((seq, LANES), lambda b, h: (b, B_K_BLK + h)),
            pl.BlockSpec((nk, None, 2 * B_V_ROWS, K_TILE), lambda b, h: (b, h, 0, 0)),
            pl.BlockSpec((3, 2, B_KEYS, Q_TILE), lambda b, h: (0, h, 0, 0)),
            pl.BlockSpec((LANES, Q_TILE), lambda b, h: (h, 0)),
        ],
        out_specs=pl.BlockSpec((seq, LANES), lambda b, h: (b, h)),
        scratch_shapes=[pltpu.VMEM((2, B_KEYS, 2 * Q_TILE), F32), pltpu.VMEM((2, 8, 2 * Q_TILE), F32),
                        pltpu.VMEM((nq + 1, Q_TILE, LANES), BF16)],
        compiler_params=_params(2),
        name="attn_b",
    )(proj, proj, vt, bias_tiles, gn_t)


def _softplus2(z):
    return jnp.maximum(z, jnp.log2(1.0 + jnp.exp2(jnp.minimum(z, SP_SWITCH))))


def _attn_c_kernel(q_ref, k_ref, vt_ref, mask_ref, gt_ref, o_ref, *, n_sub):
    tq, tk = Q_TILE, K_TILE
    tri_r = lax.broadcasted_iota(jnp.int32, (tk, tk), 0)
    tri_c = lax.broadcasted_iota(jnp.int32, (tk, tk), 1)
    later = jnp.where(tri_c > tri_r, 1.0, 0.0).astype(BF16)

    def suffix_sums(x):
        return _dot(later, x.astype(BF16))

    def values(vt, a):
        return tuple(_dot(vt[n * HEAD_DIM:(n + 1) * HEAD_DIM, :], a[:, n * tq:(n + 1) * tq]) for n in range(2))

    subs = range(n_sub)
    qis = [pl.program_id(2) * n_sub + u for u in subs]
    qcats = [jnp.concatenate(_split_heads(q_ref[u * tq:(u + 1) * tq, :]), axis=0) for u in subs]

    j0s = [jnp.maximum(qi - 1, 0) for qi in qis]
    zs = [_dot_t(k_ref[pl.ds(pl.multiple_of(j0 * tk, tk), 2 * tk), :], qcat) + mask_ref[jnp.minimum(qi, 1)]
          for j0, qi, qcat in zip(j0s, qis, qcats)]
    sps = [_softplus2(z) for z in zs]
    gone = [jnp.concatenate([suffix_sums(sp[:tk]) + jnp.sum(sp[tk:], axis=0, keepdims=True),
                             suffix_sums(sp[tk:])], axis=0) for sp in sps]
    a_s = [jnp.exp2((z - sp) - g).astype(BF16) for z, sp, g in zip(zs, sps, gone)]
    cs = [jnp.sum(sp, axis=0, keepdims=True) for sp in sps]
    accs = [tuple(x + y for x, y in zip(values(vt_ref[j0], a[:tk]), values(vt_ref[j0 + 1], a[tk:])))
            for j0, a in zip(j0s, a_s)]

    gt = gt_ref[...]
    for u in subs:
        j0, qcat = j0s[u], qcats[u]

        def live(state):
            t, c, _, _ = state
            return (t < j0) & (jnp.min(c) < -C_EXIT)

        def sweep(state):
            t, c, acc0, acc1 = state
            kj = j0 - 1 - t
            z = _dot_t(k_ref[pl.ds(pl.multiple_of(kj * tk, tk), tk), :], qcat)
            sp = _softplus2(z)
            a = jnp.exp2((z - sp) - suffix_sums(sp) - c).astype(BF16)
            d0, d1 = values(vt_ref[kj], a)
            return t + 1, c + jnp.sum(sp, axis=0, keepdims=True), acc0 + d0, acc1 + d1

        _, _, acc0, acc1 = lax.while_loop(live, sweep, (jnp.int32(0), cs[u], accs[u][0], accs[u][1]))
        outs = []
        for n, o in enumerate((acc0, acc1)):
            rs = lax.rsqrt(jnp.mean(o * o, axis=0, keepdims=True) + EPS)
            outs.append(o * rs * gt[n * HEAD_DIM:(n + 1) * HEAD_DIM, :])
        o_ref[u * tq:(u + 1) * tq, :] = jnp.concatenate(outs, axis=0).T.astype(BF16)


def _causal_bias_c():
    key = jnp.arange(2 * K_TILE)[:, None]
    query = jnp.arange(2 * Q_TILE)[None, :] % Q_TILE
    return jnp.stack([jnp.where(key < query + off, 0.0, NEG) for off in (0, K_TILE)]).astype(F32)


def _attn_c(proj, vt, gn_t, bsz, seq, n_sub=4):
    nq = seq // (Q_TILE * n_sub)
    nk = seq // K_TILE
    pairs = C_HEADS // 2
    return pl.pallas_call(
        functools.partial(_attn_c_kernel, n_sub=n_sub),
        out_shape=jax.ShapeDtypeStruct((bsz * seq, pairs * LANES), BF16),
        grid=(bsz, pairs, nq),
        in_specs=[
            pl.BlockSpec((n_sub * Q_TILE, LANES), lambda b, h, i: (b * nq + i, C_Q_BLK + h)),
            pl.BlockSpec((seq, LANES), lambda b, h, i: (b, C_K_BLK + h)),
            pl.BlockSpec((nk, None, LANES, K_TILE), lambda b, h, i: (b, h, 0, 0)),
            pl.BlockSpec((2, 2 * K_TILE, 2 * Q_TILE), lambda b, h, i: (0, 0, 0)),
            pl.BlockSpec((LANES, Q_TILE), lambda b, h, i: (h, 0)),
        ],
        out_specs=pl.BlockSpec((n_sub * Q_TILE, LANES), lambda b, h, i: (b * nq + i, h)),
        compiler_params=_params(3),
        name="attn_c",
    )(proj, proj, vt, _causal_bias_c(), gn_t)


def _outmlp_kernel(x_ref, oa_ref, ob_ref, oc_ref, wo_ref, g1_ref, g2_ref, wu_ref, wd_ref, g3_ref, o_ref, acc_ref,
                   *, f_chunk, n_part):
    wa, wb = oa_ref.shape[1], ob_ref.shape[1]
    rows = x_ref.shape[0] // n_part
    parts = [slice(r * rows, (r + 1) * rows) for r in range(n_part)]
    d_ff = wu_ref.shape[1]
    ys = [_dot(oa_ref[p, :], wo_ref[0:wa, :]) + _dot(ob_ref[p, :], wo_ref[wa:wa + wb, :])
          + _dot(oc_ref[p, :], wo_ref[wa + wb:, :]) for p in parts]
    x1s = [x_ref[p, :] + _rms(y, g1_ref[...]) for p, y in zip(parts, ys)]
    hs = [_rms(x1, g2_ref[...]).astype(BF16) for x1 in x1s]
    for p, x1, h in zip(parts, x1s, hs):
        for f0 in range(0, d_ff, f_chunk):
            u = jnp.maximum(_dot(h, wu_ref[:, f0:f0 + f_chunk]), 0.0)
            part = _dot((u * u).astype(BF16), wd_ref[f0:f0 + f_chunk, :])
            if f0 == 0:
                acc_ref[p, :] = part
            else:
                acc_ref[p, :] += part
        o_ref[p, :] = x1 + _rms(acc_ref[p, :], g3_ref[...])


def _outmlp(x, oa, ob, oc, w_out, g_post, g_pre, w_up, w_down, g_post2, tm=1024, f_chunk=1024, n_part=4):
    m, d = x.shape
    d_ff = w_up.shape[1]
    row = lambda w: pl.BlockSpec((tm, w), lambda i: (i, 0))
    full = lambda a, b: pl.BlockSpec((a, b), lambda i: (0, 0), pipeline_mode=pl.Buffered(1))
    return pl.pallas_call(
        functools.partial(_outmlp_kernel, f_chunk=f_chunk, n_part=n_part),
        out_shape=jax.ShapeDtypeStruct((m, d), F32),
        grid=(m // tm,),
        in_specs=[
            row(d), row(oa.shape[1]), row(ob.shape[1]), row(oc.shape[1]),
            full(d, d), full(1, d), full(1, d), full(d, d_ff), full(d_ff, d), full(1, d),
        ],
        out_specs=row(d),
        scratch_shapes=[pltpu.VMEM((tm, d), F32)],
        compiler_params=_params(1),
        name="outproj_mlp",
    )(x, oa, ob, oc, w_out, g_post, g_pre, w_up, w_down, g_post2)


def _rope_tables(seq):
    pos = jnp.arange(seq, dtype=F32)
    inv_freq = ROPE_THETA ** (-jnp.arange(0, ROPE_DIM, 2, dtype=F32) / ROPE_DIM)
    ang = pos[:, None] * inv_freq[None, :]
    cos, sin = jnp.cos(ang), jnp.sin(ang)
    half = ROPE_DIM // 2
    d = jnp.arange(LANES) % HEAD_DIM
    f = d % half
    rc = jnp.where(d[None, :] < ROPE_DIM, cos[:, f], 1.0)
    rs1 = jnp.where(d[None, :] < half, -sin[:, f], 0.0)
    rs2 = jnp.where((d[None, :] >= half) & (d[None, :] < ROPE_DIM), sin[:, f], 0.0)
    return rc.astype(F32), rs1.astype(F32), rs2.astype(F32)


def _q_scale_columns():
    blk = jnp.arange(D_IN) // LANES
    is_q = (blk < A_K_BLK) | ((blk >= B_Q_BLK) & (blk < B_K_BLK)) | ((blk >= C_Q_BLK) & (blk < C_K_BLK))
    return jnp.where(is_q, HEAD_DIM ** -0.5 * math.log2(math.e), 1.0).astype(F32)


def _gain_t(g, n_blk, width):
    return jnp.broadcast_to(g.reshape(n_blk * LANES, 1), (n_blk * LANES, width)).astype(F32)


def kernel(x, norm_pre_mix, w_in, lam_q1, lam_k1, lam_q2, lam_k2, subln_a, rel_bias, gn_b, gn_c, w_out,
           norm_post_mix, norm_pre_mlp, w_up, w_down, norm_post_mlp):
    bsz, seq, d = x.shape
    depth = w_in.shape[0]
    xf = x.reshape(bsz * seq, d)
    rc, rs1, rs2 = _rope_tables(seq)
    q_scale = _q_scale_columns()
    row = lambda v: v.reshape(1, -1)
    for l in range(depth):
        lam_init = 0.8 - 0.6 * math.exp(-0.3 * l)
        proj, vt_a, vt_b, vt_c = _inproj(xf, row(norm_pre_mix[l]), w_in, l, row(q_scale), rc, rs1, rs2, seq)
        lam_vecs = jnp.stack([lam_q1[l], lam_k1[l], lam_q2[l], lam_k2[l]]).astype(F32)
        oa = _attn_a(proj, vt_a, lam_vecs, _gain_t(subln_a[l], 1, A_TILE), lam_init, bsz, seq)
        ob = _attn_b(proj, vt_b, _bias_tiles(rel_bias[l]), _gain_t(gn_b[l], B_HEADS // 2, Q_TILE), bsz, seq)
        oc = _attn_c(proj, vt_c, _gain_t(gn_c[l], C_HEADS // 2, Q_TILE), bsz, seq)
        xf = _outmlp(xf, oa, ob, oc, w_out[l].astype(BF16), row(norm_post_mix[l]), row(norm_pre_mlp[l]),
                     w_up[l].astype(BF16), w_down[l].astype(BF16), row(norm_post_mlp[l]))
    return xf.reshape(bsz, seq, d)
```

```python
import functools
import math

import jax
import jax.numpy as jnp
from jax import lax
from jax.experimental import pallas as pl
from jax.experimental.pallas import tpu as pltpu

F32 = jnp.float32
BF16 = jnp.bfloat16

EPS = 1e-6
ROPE_THETA = 500000.0
HEAD_DIM = 64
ROPE_DIM = HEAD_DIM // 4
CHUNK = 64
LANES = 128
A_HEADS = 4
B_HEADS = 4
C_HEADS = 4
B_LEFT_CHUNKS = 8
REL_CLIP = 128
NEG = -1e30
M_RESET = -5e29
SP_SWITCH = 64.0
C_EXIT = -160.0

A_Q_BLK, A_K_BLK, A_V_BLK = 0, 4, 8
B_Q_BLK, B_K_BLK, B_V_BLK = 12, 14, 16
C_Q_BLK, C_K_BLK, C_V_BLK = 18, 20, 22
D_IN = 24 * LANES

Q_TILE = 256
K_TILE = 256
A_TILE = 512
A_UNROLL = 4
B_UNROLL = 8
V_ROWS = LANES + 16
B_KEYS = 3 * K_TILE
B_TABLE_W = B_KEYS + 2 * K_TILE
B_V_ROWS = HEAD_DIM + 16

VMEM_LIMIT = 56 * 1024 * 1024


def _params(n_axes, vmem=VMEM_LIMIT):
    return pltpu.CompilerParams(dimension_semantics=("arbitrary",) * n_axes, vmem_limit_bytes=vmem)


def _rms(x, g):
    return x * lax.rsqrt(jnp.mean(x * x, axis=-1, keepdims=True) + EPS) * g


def _dot_t(a, b):
    return lax.dot_general(a, b, (((1,), (1,)), ((), ())), preferred_element_type=F32)


def _dot(a, b):
    return jnp.dot(a, b, preferred_element_type=F32)


def _split_heads(q):
    lane = lax.broadcasted_iota(jnp.int32, q.shape, 1)
    zero = jnp.zeros_like(q)
    return jnp.where(lane < HEAD_DIM, q, zero), jnp.where(lane >= HEAD_DIM, q, zero)


def _inproj_kernel(x_ref, g_ref, w_ref, qs_ref, rc_ref, rs1_ref, rs2_ref, o_ref, vta_ref, vtb_ref, vtc_ref, *, n_chunk):
    h = _rms(x_ref[...], g_ref[...]).astype(BF16)
    rc, rs1, rs2 = rc_ref[...], rs1_ref[...], rs2_ref[...]
    tm = x_ref.shape[0]
    pad_rows = V_ROWS - LANES
    ones_rows = jnp.where(lax.broadcasted_iota(jnp.int32, (pad_rows, A_TILE), 0) == 0, 1.0, 0.0).astype(BF16)
    ones_rows_k = jnp.where(lax.broadcasted_iota(jnp.int32, (pad_rows, K_TILE), 0) == 0, 1.0, 0.0).astype(BF16)
    kt = tm // K_TILE
    for n0 in range(0, D_IN, n_chunk):
        y = _dot(h, w_ref[:, n0:n0 + n_chunk].astype(BF16))
        for j in range(n_chunk // LANES):
            blk = n0 // LANES + j
            yj = y[:, j * LANES:(j + 1) * LANES]
            if blk < A_K_BLK or B_Q_BLK <= blk < B_K_BLK or C_Q_BLK <= blk < C_K_BLK:
                yj = yj * qs_ref[:, blk * LANES:(blk + 1) * LANES]
            if blk < A_V_BLK:
                yj = yj * rc + pltpu.roll(yj, LANES - ROPE_DIM // 2, 1) * rs1 + pltpu.roll(yj, ROPE_DIM // 2, 1) * rs2
            o_ref[:, n0 + j * LANES:n0 + (j + 1) * LANES] = yj.astype(BF16)
            if A_V_BLK <= blk < B_Q_BLK:
                yt = yj.T.astype(BF16)
                for t in range(tm // A_TILE):
                    vta_ref[t, blk - A_V_BLK, 0:LANES, :] = yt[:, t * A_TILE:(t + 1) * A_TILE]
                    vta_ref[t, blk - A_V_BLK, LANES:V_ROWS, :] = ones_rows
            if B_V_BLK <= blk < C_Q_BLK:
                yt = yj.T.astype(BF16)
                for t in range(kt):
                    for n in range(2):
                        r0 = n * B_V_ROWS
                        vtb_ref[t, blk - B_V_BLK, r0:r0 + HEAD_DIM, :] = (
                            yt[n * HEAD_DIM:(n + 1) * HEAD_DIM, t * K_TILE:(t + 1) * K_TILE])
                        vtb_ref[t, blk - B_V_BLK, r0 + HEAD_DIM:r0 + B_V_ROWS, :] = ones_rows_k
            if blk >= C_V_BLK:
                yt = yj.T.astype(BF16)
                for t in range(kt):
                    vtc_ref[t, blk - C_V_BLK] = yt[:, t * K_TILE:(t + 1) * K_TILE]


def _inproj(x, g, w_all, layer, q_scale, rc, rs1, rs2, seq, tm=2 * A_TILE, n_chunk=512):
    m, d = x.shape
    pos_blocks = seq // tm
    rope_spec = pl.BlockSpec((tm, LANES), lambda i: (i % pos_blocks, 0))
    kt = tm // K_TILE
    return pl.pallas_call(
        functools.partial(_inproj_kernel, n_chunk=n_chunk),
        out_shape=(jax.ShapeDtypeStruct((m, D_IN), BF16),
                   jax.ShapeDtypeStruct((m // A_TILE, A_HEADS, V_ROWS, A_TILE), BF16),
                   jax.ShapeDtypeStruct((m // K_TILE, B_HEADS // 2, 2 * B_V_ROWS, K_TILE), BF16),
                   jax.ShapeDtypeStruct((m // K_TILE, C_HEADS // 2, LANES, K_TILE), BF16)),
        grid=(m // tm,),
        in_specs=[
            pl.BlockSpec((tm, d), lambda i: (i, 0)),
            pl.BlockSpec((1, d), lambda i: (0, 0)),
            pl.BlockSpec((None, d, D_IN), lambda i: (layer, 0, 0), pipeline_mode=pl.Buffered(1)),
            pl.BlockSpec((1, D_IN), lambda i: (0, 0)),
            rope_spec, rope_spec, rope_spec,
        ],
        out_specs=(pl.BlockSpec((tm, D_IN), lambda i: (i, 0)),
                   pl.BlockSpec((tm // A_TILE, A_HEADS, V_ROWS, A_TILE), lambda i: (i, 0, 0, 0)),
                   pl.BlockSpec((kt, B_HEADS // 2, 2 * B_V_ROWS, K_TILE), lambda i: (i, 0, 0, 0)),
                   pl.BlockSpec((kt, C_HEADS // 2, LANES, K_TILE), lambda i: (i, 0, 0, 0))),
        compiler_params=_params(1),
        name="inproj",
    )(x, g, w_all, q_scale, rc, rs1, rs2)


def _attn_a_kernel(q_ref, k_ref, vt_ref, lam_ref, gt_ref, o_ref, qx_ref, m_ref, mx_ref, s_ref, acc_ref,
                   *, lam_init, nq):
    tq = A_TILE
    n_chunks = tq // CHUNK
    fr = lax.broadcasted_iota(jnp.int32, (2 * tq, LANES), 0)
    fc = lax.broadcasted_iota(jnp.int32, (2 * tq, LANES), 1)
    q_feat = jnp.where((fc < n_chunks) & ((fr % tq) // CHUNK == fc), 1.0, 0.0).astype(BF16)
    q_feat_t = q_feat.astype(F32).T.astype(BF16)
    for blk in range(nq):
        q0, q1 = _split_heads(q_ref[blk * tq:(blk + 1) * tq, :])
        qx_ref[blk, 0:LANES, 0:tq] = q0.astype(F32).T.astype(BF16)
        qx_ref[blk, 0:LANES, tq:2 * tq] = q1.astype(F32).T.astype(BF16)
        qx_ref[blk, LANES:2 * LANES, :] = q_feat_t
    kr = lax.broadcasted_iota(jnp.int32, (tq, LANES), 0)
    kc_ = lax.broadcasted_iota(jnp.int32, (tq, LANES), 1)
    k_feat = jnp.where((kc_ < n_chunks) & (kr // CHUNK > kc_), NEG, 0.0).astype(BF16)
    k_nofeat = jnp.zeros_like(k_feat)

    m_ref[...] = jnp.full(m_ref.shape, M_RESET, F32)

    @pl.when((pl.program_id(0) == 0) & (pl.program_id(1) == 0))
    def _():
        acc_ref[...] = jnp.zeros(acc_ref.shape, F32)

    def score_stage(qa, ka, slot):
        k = k_ref[pl.ds(pl.multiple_of(ka * tq, tq), tq), :]
        kx = jnp.concatenate([k, jnp.where(ka == qa, k_feat, k_nofeat)], axis=1)
        s_new = _dot(kx, qx_ref[qa])
        s_ref[slot] = s_new
        mx_ref[slot, 0:1, :] = jnp.max(s_new, axis=0, keepdims=True)

    def value_stage(qb, kb, slot):
        m_old = jnp.where(kb == 0, M_RESET, m_ref[0:1, :])
        m_new = jnp.maximum(m_old, mx_ref[slot, 0:1, :])
        m_ref[0:1, :] = m_new
        p = jnp.exp2(s_ref[slot] - m_new).astype(BF16)
        acc_ref[qb] = jnp.exp2(m_old - m_new) * acc_ref[qb] + _dot(vt_ref[kb], p)

    def advance(st):
        qa, ka, _, _ = st
        last = ka == qa
        return (jnp.where(last, qa + 1, qa), jnp.where(last, 0, ka + 1), qa, ka)

    def step(st, slot):
        qa, ka, qb, kb = st
        score_stage(qa, ka, slot)
        value_stage(qb, kb, 1 - slot)
        return advance(st)

    def steps(i, st):
        for u in range(A_UNROLL):
            st = step(st, (u + 1) % 2)
        return st

    n_tiles = nq * (nq + 1) // 2
    zero = jnp.int32(0)
    score_stage(zero, zero, 0)
    st = advance((zero, zero, zero, zero))
    n_loop = (n_tiles - 1) // A_UNROLL
    st = lax.fori_loop(0, n_loop, steps, st)
    for f in range(1 + n_loop * A_UNROLL, n_tiles):
        st = step(st, f % 2)
    value_stage(st[2], st[3], (n_tiles - 1) % 2)

    lv = lam_ref[...]
    lam = (jnp.exp(jnp.sum(lv[0:1] * lv[1:2], axis=-1, keepdims=True))
           - jnp.exp(jnp.sum(lv[2:3] * lv[3:4], axis=-1, keepdims=True)) + lam_init)
    for blk in range(nq):
        acc = acc_ref[blk]
        on = acc[0:LANES] * (1.0 / acc[LANES:LANES + 1])
        o = on[:, :tq] - lam * on[:, tq:]
        o = o * lax.rsqrt(jnp.mean(o * o, axis=0, keepdims=True) + EPS) * gt_ref[...] * (1.0 - lam_init)
        o_ref[blk * tq:(blk + 1) * tq, :] = o.T.astype(BF16)


def _attn_a(proj, vt, lam_vecs, subln_gt, lam_init, bsz, seq):
    nq = seq // A_TILE
    return pl.pallas_call(
        functools.partial(_attn_a_kernel, lam_init=lam_init, nq=nq),
        out_shape=jax.ShapeDtypeStruct((bsz * seq, A_HEADS * LANES), BF16),
        grid=(bsz, A_HEADS),
        in_specs=[
            pl.BlockSpec((seq, LANES), lambda b, h: (b, A_Q_BLK + h)),
            pl.BlockSpec((seq, LANES), lambda b, h: (b, A_K_BLK + h)),
            pl.BlockSpec((nq, None, V_ROWS, A_TILE), lambda b, h: (b, h, 0, 0)),
            pl.BlockSpec((4, HEAD_DIM), lambda b, h: (0, 0)),
            pl.BlockSpec((LANES, A_TILE), lambda b, h: (0, 0)),
        ],
        out_specs=pl.BlockSpec((seq, LANES), lambda b, h: (b, h)),
        scratch_shapes=[pltpu.VMEM((nq, 2 * LANES, 2 * A_TILE), BF16), pltpu.VMEM((8, 2 * A_TILE), F32),
                        pltpu.VMEM((2, 8, 2 * A_TILE), F32),
                        pltpu.VMEM((2, A_TILE, 2 * A_TILE), F32),
                        pltpu.VMEM((nq, V_ROWS, 2 * A_TILE), F32)],
        compiler_params=_params(2),
        name="attn_a",
    )(proj, proj, vt, lam_vecs, subln_gt)


def _bias_kernel(base_ref, o_ref):
    win = 3 * LANES
    base = base_ref[...] * math.log2(math.e)
    spread = jnp.broadcast_to(base, (CHUNK, win + LANES))
    ws = tuple(pltpu.roll(spread, shift, 1, stride=1, stride_axis=0)[:, 0:win] for shift in (0, CHUNK))
    far = base[:, win:win + 1]

    shape = (Q_TILE, B_TABLE_W)
    row = lax.broadcasted_iota(jnp.int32, shape, 0)
    col = lax.broadcasted_iota(jnp.int32, shape, 1)
    kc = col // CHUNK - row // CHUNK
    band = (kc >= 0) & (kc <= B_LEFT_CHUNKS)

    win_start = (3 * LANES, 3 * LANES, 4 * LANES, 4 * LANES)
    rows = []
    for c in range(Q_TILE // CHUNK):
        pieces = []
        for j in range(B_TABLE_W // LANES):
            lo = j * LANES
            bnd = band[c * CHUNK:(c + 1) * CHUNK, lo:lo + LANES]
            if win_start[c] <= lo < win_start[c] + win:
                off = lo - win_start[c]
                pieces.append(jnp.where(bnd, ws[c % 2][:, off:off + LANES], NEG))
            else:
                pieces.append(jnp.where(bnd, far, NEG))
        rows.append(jnp.concatenate(pieces, axis=1))
    table = jnp.concatenate(rows, axis=0)
    for v in range(3):
        c0 = (2 - v) * K_TILE
        o_ref[v] = table[:, c0:c0 + B_KEYS].T


def _bias_tiles(rel_bias):
    heads = rel_bias.shape[0]
    win = 3 * LANES
    base = jnp.concatenate([rel_bias[:, ::-1],
                            jnp.broadcast_to(rel_bias[:, :1], (heads, win - 2 * REL_CLIP - 1)),
                            jnp.broadcast_to(rel_bias[:, -1:], (heads, LANES))], axis=1).astype(F32)
    return pl.pallas_call(
        _bias_kernel,
        out_shape=jax.ShapeDtypeStruct((3, heads, B_KEYS, Q_TILE), F32),
        grid=(heads,),
        in_specs=[pl.BlockSpec((None, 1, win + LANES), lambda h: (h, 0, 0))],
        out_specs=pl.BlockSpec((3, None, B_KEYS, Q_TILE), lambda h: (0, h, 0, 0)),
        compiler_params=_params(1),
        name="bias_tiles",
    )(base.reshape(heads, 1, win + LANES))


def _attn_b_kernel(q_ref, k_ref, vt_ref, bias_ref, gt_ref, o_ref, s_ref, mx_ref, res_ref, *, nq):
    tq = Q_TILE
    s_ref[1] = jnp.zeros(s_ref.shape[1:], F32)
    mx_ref[...] = jnp.zeros(mx_ref.shape, F32)
    gt = gt_ref[...]

    def score_stage(i, slot):
        first_key = pl.multiple_of(jnp.maximum(i - 2, 0) * K_TILE, K_TILE)
        qcat = jnp.concatenate(_split_heads(q_ref[pl.ds(pl.multiple_of(i * tq, tq), tq), :]), axis=0)
        var = jnp.minimum(i, 2)
        s = (_dot_t(k_ref[pl.ds(first_key, B_KEYS), :], qcat)
             + jnp.concatenate([bias_ref[var, 0], bias_ref[var, 1]], axis=1))
        s_ref[slot] = s
        mx_ref[slot, 0:1, :] = jnp.max(s, axis=0, keepdims=True)

    def value_stage(i, slot):
        jb = jnp.maximum(i - 2, 0)
        p = jnp.exp2(s_ref[slot] - mx_ref[slot, 0:1, :]).astype(BF16)
        outs = []
        for n in range(2):
            acc = sum(_dot(vt_ref[jb + t][n * B_V_ROWS:(n + 1) * B_V_ROWS, :],
                           p[t * K_TILE:(t + 1) * K_TILE, n * tq:(n + 1) * tq]) for t in range(B_KEYS // K_TILE))
            o = acc[0:HEAD_DIM] * (1.0 / acc[HEAD_DIM:HEAD_DIM + 1])
            rs = lax.rsqrt(jnp.mean(o * o, axis=0, keepdims=True) + EPS)
            outs.append(o * rs * gt[n * HEAD_DIM:(n + 1) * HEAD_DIM, :])
        res_ref[jnp.where(i < 0, nq, i)] = jnp.concatenate(outs, axis=0).T.astype(BF16)

    def step(i, slot):
        score_stage(i, slot)
        value_stage(i - 1, 1 - slot)

    def steps(it, carry):
        for u in range(B_UNROLL):
            step(B_UNROLL * it + u, u % 2)
        return carry

    lax.fori_loop(0, nq // B_UNROLL, steps, 0)
    for i in range(nq // B_UNROLL * B_UNROLL, nq):
        step(i, i % 2)
    value_stage(nq - 1, (nq - 1) % 2)
    for blk in range(nq):
        o_ref[blk * tq:(blk + 1) * tq, :] = res_ref[blk]


def _attn_b(proj, vt, bias_tiles, gn_t, bsz, seq):
    nq = seq // Q_TILE
    nk = seq // K_TILE
    pairs = B_HEADS // 2
    return pl.pallas_call(
        functools.partial(_attn_b_kernel, nq=nq),
        out_shape=jax.ShapeDtypeStruct((bsz * seq, pairs * LANES), BF16),
        grid=(bsz, pairs),
        in_specs=[
            pl.BlockSpec((seq, LANES), lambda b, h: (b, B_Q_BLK + h)),
            pl.BlockSpec((seq, LANES), lambda b, h: (b, B_K_BLK + h)),
            pl.BlockSpec((nk, None, 2 * B_V_ROWS, K_TILE), lambda b, h: (b, h, 0, 0)),
            pl.BlockSpec((3, 2, B_KEYS, Q_TILE), lambda b, h: (0, h, 0, 0)),
            pl.BlockSpec((LANES, Q_TILE), lambda b, h: (h, 0)),
        ],
        out_specs=pl.BlockSpec((seq, LANES), lambda b, h: (b, h)),
        scratch_shapes=[pltpu.VMEM((2, B_KEYS, 2 * Q_TILE), F32), pltpu.VMEM((2, 8, 2 * Q_TILE), F32),
                        pltpu.VMEM((nq + 1, Q_TILE, LANES), BF16)],
        compiler_params=_params(2),
        name="attn_b",
    )(proj, proj, vt, bias_tiles, gn_t)


def _softplus2(z):
    return jnp.maximum(z, jnp.log2(1.0 + jnp.exp2(jnp.minimum(z, SP_SWITCH))))


def _attn_c_kernel(q_ref, k_ref, vt_ref, mask_ref, gt_ref, o_ref, *, n_sub):
    tq, tk = Q_TILE, K_TILE
    tri_r = lax.broadcasted_iota(jnp.int32, (tk, tk), 0)
    tri_c = lax.broadcasted_iota(jnp.int32, (tk, tk), 1)
    later = jnp.where(tri_c > tri_r, 1.0, 0.0).astype(BF16)

    def suffix_sums(x):
        return _dot(later, x.astype(BF16))

    def values(vt, a):
        return tuple(_dot(vt[n * HEAD_DIM:(n + 1) * HEAD_DIM, :], a[:, n * tq:(n + 1) * tq]) for n in range(2))

    subs = range(n_sub)
    qis = [pl.program_id(2) * n_sub + u for u in subs]
    qcats = [jnp.concatenate(_split_heads(q_ref[u * tq:(u + 1) * tq, :]), axis=0) for u in subs]

    j0s = [jnp.maximum(qi - 1, 0) for qi in qis]
    zs = [_dot_t(k_ref[pl.ds(pl.multiple_of(j0 * tk, tk), 2 * tk), :], qcat) + mask_ref[jnp.minimum(qi, 1)]
          for j0, qi, qcat in zip(j0s, qis, qcats)]
    sps = [_softplus2(z) for z in zs]
    gone = [jnp.concatenate([suffix_sums(sp[:tk]) + jnp.sum(sp[tk:], axis=0, keepdims=True),
                             suffix_sums(sp[tk:])], axis=0) for sp in sps]
    a_s = [jnp.exp2((z - sp) - g).astype(BF16) for z, sp, g in zip(zs, sps, gone)]
    cs = [jnp.sum(sp, axis=0, keepdims=True) for sp in sps]
    accs = [tuple(x + y for x, y in zip(values(vt_ref[j0], a[:tk]), values(vt_ref[j0 + 1], a[tk:])))
            for j0, a in zip(j0s, a_s)]

    gt = gt_ref[...]
    for u in subs:
        j0, qcat = j0s[u], qcats[u]

        def live(state):
            t, c, _, _ = state
            return (t < j0) & (jnp.min(c) < -C_EXIT)

        def sweep(state):
            t, c, acc0, acc1 = state
            kj = j0 - 1 - t
            z = _dot_t(k_ref[pl.ds(pl.multiple_of(kj * tk, tk), tk), :], qcat)
            sp = _softplus2(z)
            a = jnp.exp2((z - sp) - suffix_sums(sp) - c).astype(BF16)
            d0, d1 = values(vt_ref[kj], a)
            return t + 1, c + jnp.sum(sp, axis=0, keepdims=True), acc0 + d0, acc1 + d1

        _, _, acc0, acc1 = lax.while_loop(live, sweep, (jnp.int32(0), cs[u], accs[u][0], accs[u][1]))
        outs = []
        for n, o in enumerate((acc0, acc1)):
            rs = lax.rsqrt(jnp.mean(o * o, axis=0, keepdims=True) + EPS)
            outs.append(o * rs * gt[n * HEAD_DIM:(n + 1) * HEAD_DIM, :])
        o_ref[u * tq:(u + 1) * tq, :] = jnp.concatenate(outs, axis=0).T.astype(BF16)


def _causal_bias_c():
    key = jnp.arange(2 * K_TILE)[:, None]
    query = jnp.arange(2 * Q_TILE)[None, :] % Q_TILE
    return jnp.stack([jnp.where(key < query + off, 0.0, NEG) for off in (0, K_TILE)]).astype(F32)


def _attn_c(proj, vt, gn_t, bsz, seq, n_sub=4):
    nq = seq // (Q_TILE * n_sub)
    nk = seq // K_TILE
    pairs = C_HEADS // 2
    return pl.pallas_call(
        functools.partial(_attn_c_kernel, n_sub=n_sub),
        out_shape=jax.ShapeDtypeStruct((bsz * seq, pairs * LANES), BF16),
        grid=(bsz, pairs, nq),
        in_specs=[
            pl.BlockSpec((n_sub * Q_TILE, LANES), lambda b, h, i: (b * nq + i, C_Q_BLK + h)),
            pl.BlockSpec((seq, LANES), lambda b, h, i: (b, C_K_BLK + h)),
            pl.BlockSpec((nk, None, LANES, K_TILE), lambda b, h, i: (b, h, 0, 0)),
            pl.BlockSpec((2, 2 * K_TILE, 2 * Q_TILE), lambda b, h, i: (0, 0, 0)),
            pl.BlockSpec((LANES, Q_TILE), lambda b, h, i: (h, 0)),
        ],
        out_specs=pl.BlockSpec((n_sub * Q_TILE, LANES), lambda b, h, i: (b * nq + i, h)),
        compiler_params=_params(3),
        name="attn_c",
    )(proj, proj, vt, _causal_bias_c(), gn_t)


def _outmlp_kernel(x_ref, oa_ref, ob_ref, oc_ref, wo_ref, g1_ref, g2_ref, wu_ref, wd_ref, g3_ref, o_ref, acc_ref,
                   *, f_chunk, n_part):
    wa, wb = oa_ref.shape[1], ob_ref.shape[1]
    rows = x_ref.shape[0] // n_part
    parts = [slice(r * rows, (r + 1) * rows) for r in range(n_part)]
    d_ff = wu_ref.shape[1]
    ys = [_dot(oa_ref[p, :], wo_ref[0:wa, :]) + _dot(ob_ref[p, :], wo_ref[wa:wa + wb, :])
          + _dot(oc_ref[p, :], wo_ref[wa + wb:, :]) for p in parts]
    x1s = [x_ref[p, :] + _rms(y, g1_ref[...]) for p, y in zip(parts, ys)]
    hs = [_rms(x1, g2_ref[...]).astype(BF16) for x1 in x1s]
    for p, x1, h in zip(parts, x1s, hs):
        for f0 in range(0, d_ff, f_chunk):
            u = jnp.maximum(_dot(h, wu_ref[:, f0:f0 + f_chunk]), 0.0)
            part = _dot((u * u).astype(BF16), wd_ref[f0:f0 + f_chunk, :])
            if f0 == 0:
                acc_ref[p, :] = part
            else:
                acc_ref[p, :] += part
        o_ref[p, :] = x1 + _rms(acc_ref[p, :], g3_ref[...])


def _outmlp(x, oa, ob, oc, w_out, g_post, g_pre, w_up, w_down, g_post2, tm=1024, f_chunk=1024, n_part=4):
    m, d = x.shape
    d_ff = w_up.shape[1]
    row = lambda w: pl.BlockSpec((tm, w), lambda i: (i, 0))
    full = lambda a, b: pl.BlockSpec((a, b), lambda i: (0, 0), pipeline_mode=pl.Buffered(1))
    return pl.pallas_call(
        functools.partial(_outmlp_kernel, f_chunk=f_chunk, n_part=n_part),
        out_shape=jax.ShapeDtypeStruct((m, d), F32),
        grid=(m // tm,),
        in_specs=[
            row(d), row(oa.shape[1]), row(ob.shape[1]), row(oc.shape[1]),
            full(d, d), full(1, d), full(1, d), full(d, d_ff), full(d_ff, d), full(1, d),
        ],
        out_specs=row(d),
        scratch_shapes=[pltpu.VMEM((tm, d), F32)],
        compiler_params=_params(1),
        name="outproj_mlp",
    )(x, oa, ob, oc, w_out, g_post, g_pre, w_up, w_down, g_post2)


def _rope_tables(seq):
    pos = jnp.arange(seq, dtype=F32)
    inv_freq = ROPE_THETA ** (-jnp.arange(0, ROPE_DIM, 2, dtype=F32) / ROPE_DIM)
    ang = pos[:, None] * inv_freq[None, :]
    cos, sin = jnp.cos(ang), jnp.sin(ang)
    half = ROPE_DIM // 2
    d = jnp.arange(LANES) % HEAD_DIM
    f = d % half
    rc = jnp.where(d[None, :] < ROPE_DIM, cos[:, f], 1.0)
    rs1 = jnp.where(d[None, :] < half, -sin[:, f], 0.0)
    rs2 = jnp.where((d[None, :] >= half) & (d[None, :] < ROPE_DIM), sin[:, f], 0.0)
    return rc.astype(F32), rs1.astype(F32), rs2.astype(F32)


def _q_scale_columns():
    blk = jnp.arange(D_IN) // LANES
    is_q = (blk < A_K_BLK) | ((blk >= B_Q_BLK) & (blk < B_K_BLK)) | ((blk >= C_Q_BLK) & (blk < C_K_BLK))
    return jnp.where(is_q, HEAD_DIM ** -0.5 * math.log2(math.e), 1.0).astype(F32)


def _gain_t(g, n_blk, width):
    return jnp.broadcast_to(g.reshape(n_blk * LANES, 1), (n_blk * LANES, width)).astype(F32)


def kernel(x, norm_pre_mix, w_in, lam_q1, lam_k1, lam_q2, lam_k2, subln_a, rel_bias, gn_b, gn_c, w_out,
           norm_post_mix, norm_pre_mlp, w_up, w_down, norm_post_mlp):
    bsz, seq, d = x.shape
    depth = w_in.shape[0]
    xf = x.reshape(bsz * seq, d)
    rc, rs1, rs2 = _rope_tables(seq)
    q_scale = _q_scale_columns()
    row = lambda v: v.reshape(1, -1)
    for l in range(depth):
        lam_init = 0.8 - 0.6 * math.exp(-0.3 * l)
        proj, vt_a, vt_b, vt_c = _inproj(xf, row(norm_pre_mix[l]), w_in, l, row(q_scale), rc, rs1, rs2, seq)
        lam_vecs = jnp.stack([lam_q1[l], lam_k1[l], lam_q2[l], lam_k2[l]]).astype(F32)
        oa = _attn_a(proj, vt_a, lam_vecs, _gain_t(subln_a[l], 1, A_TILE), lam_init, bsz, seq)
        ob = _attn_b(proj, vt_b, _bias_tiles(rel_bias[l]), _gain_t(gn_b[l], B_HEADS // 2, Q_TILE), bsz, seq)
        oc = _attn_c(proj, vt_c, _gain_t(gn_c[l], C_HEADS // 2, Q_TILE), bsz, seq)
        xf = _outmlp(xf, oa, ob, oc, w_out[l].astype(BF16), row(norm_post_mix[l]), row(norm_pre_mlp[l]),
                     w_up[l].astype(BF16), w_down[l].astype(BF16), row(norm_post_mlp[l]))
    return xf.reshape(bsz, seq, d)
```

```python
import functools
import math

import jax
import jax.numpy as jnp
from jax import lax
from jax.experimental import pallas as pl
from jax.experimental.pallas import tpu as pltpu

F32 = jnp.float32
BF16 = jnp.bfloat16

EPS = 1e-6
ROPE_THETA = 500000.0
HEAD_DIM = 64
ROPE_DIM = HEAD_DIM // 4
CHUNK = 64
LANES = 128
SUBLANES = 8
A_HEADS = 4
B_HEADS = 4
C_HEADS = 4
B_LEFT_CHUNKS = 8
REL_CLIP = 128
NEG = -1e30
M_RESET = -5e29
SP_SWITCH = 64.0
C_EXIT = -160.0

A_Q_BLK, A_K_BLK, A_V_BLK = 0, 4, 8
B_Q_BLK, B_K_BLK, B_V_BLK = 12, 14, 16
C_Q_BLK, C_K_BLK, C_V_BLK = 18, 20, 22
D_IN = 24 * LANES

Q_TILE = 256
K_TILE = 256
A_TILE = 512
A_UNROLL = 4
B_UNROLL = 8
V_ROWS = LANES + 16
B_KEYS = 3 * K_TILE
B_TABLE_W = B_KEYS + 2 * K_TILE
B_V_ROWS = HEAD_DIM + 16

VMEM_LIMIT = 56 * 1024 * 1024


def _params(n_axes, vmem=VMEM_LIMIT):
    return pltpu.CompilerParams(dimension_semantics=("arbitrary",) * n_axes, vmem_limit_bytes=vmem)


def _rms(x, g):
    return x * lax.rsqrt(jnp.mean(x * x, axis=-1, keepdims=True) + EPS) * g


def _dot_t(a, b):
    return lax.dot_general(a, b, (((1,), (1,)), ((), ())), preferred_element_type=F32)


def _dot(a, b):
    return jnp.dot(a, b, preferred_element_type=F32)


def _split_heads(q):
    lane = lax.broadcasted_iota(jnp.int32, q.shape, 1)
    zero = jnp.zeros_like(q)
    return jnp.where(lane < HEAD_DIM, q, zero), jnp.where(lane >= HEAD_DIM, q, zero)


def _inproj_kernel(x_ref, g_ref, w_ref, qs_ref, rc_ref, rs1_ref, rs2_ref, o_ref, vta_ref, vtb_ref, vtc_ref, *, n_chunk):
    h = _rms(x_ref[...], g_ref[...]).astype(BF16)
    rc, rs1, rs2 = rc_ref[...], rs1_ref[...], rs2_ref[...]
    tm = x_ref.shape[0]
    pad_rows = V_ROWS - LANES
    ones_rows = jnp.where(lax.broadcasted_iota(jnp.int32, (pad_rows, A_TILE), 0) == 0, 1.0, 0.0).astype(BF16)
    ones_rows_k = jnp.where(lax.broadcasted_iota(jnp.int32, (pad_rows, K_TILE), 0) == 0, 1.0, 0.0).astype(BF16)
    kt = tm // K_TILE
    for n0 in range(0, D_IN, n_chunk):
        y = _dot(h, w_ref[:, n0:n0 + n_chunk].astype(BF16))
        for j in range(n_chunk // LANES):
            blk = n0 // LANES + j
            yj = y[:, j * LANES:(j + 1) * LANES]
            if blk < A_K_BLK or B_Q_BLK <= blk < B_K_BLK or C_Q_BLK <= blk < C_K_BLK:
                yj = yj * qs_ref[:, blk * LANES:(blk + 1) * LANES]
            if blk < A_V_BLK:
                yj = yj * rc + pltpu.roll(yj, LANES - ROPE_DIM // 2, 1) * rs1 + pltpu.roll(yj, ROPE_DIM // 2, 1) * rs2
            o_ref[:, n0 + j * LANES:n0 + (j + 1) * LANES] = yj.astype(BF16)
            if A_V_BLK <= blk < B_Q_BLK:
                yt = yj.T.astype(BF16)
                for t in range(tm // A_TILE):
                    vta_ref[t, blk - A_V_BLK, 0:LANES, :] = yt[:, t * A_TILE:(t + 1) * A_TILE]
                    vta_ref[t, blk - A_V_BLK, LANES:V_ROWS, :] = ones_rows
            if B_V_BLK <= blk < C_Q_BLK:
                yt = yj.T.astype(BF16)
                for t in range(kt):
                    for n in range(2):
                        r0 = n * B_V_ROWS
                        vtb_ref[t, blk - B_V_BLK, r0:r0 + HEAD_DIM, :] = (
                            yt[n * HEAD_DIM:(n + 1) * HEAD_DIM, t * K_TILE:(t + 1) * K_TILE])
                        vtb_ref[t, blk - B_V_BLK, r0 + HEAD_DIM:r0 + B_V_ROWS, :] = ones_rows_k
            if blk >= C_V_BLK:
                yt = yj.T.astype(BF16)
                for t in range(kt):
                    vtc_ref[t, blk - C_V_BLK] = yt[:, t * K_TILE:(t + 1) * K_TILE]


def _inproj(x, g, w_all, layer, q_scale, rc, rs1, rs2, seq, tm=2 * A_TILE, n_chunk=512):
    m, d = x.shape
    pos_blocks = seq // tm
    rope_spec = pl.BlockSpec((tm, LANES), lambda i: (i % pos_blocks, 0))
    kt = tm // K_TILE
    return pl.pallas_call(
        functools.partial(_inproj_kernel, n_chunk=n_chunk),
        out_shape=(jax.ShapeDtypeStruct((m, D_IN), BF16),
                   jax.ShapeDtypeStruct((m // A_TILE, A_HEADS, V_ROWS, A_TILE), BF16),
                   jax.ShapeDtypeStruct((m // K_TILE, B_HEADS // 2, 2 * B_V_ROWS, K_TILE), BF16),
                   jax.ShapeDtypeStruct((m // K_TILE, C_HEADS // 2, LANES, K_TILE), BF16)),
        grid=(m // tm,),
        in_specs=[
            pl.BlockSpec((tm, d), lambda i: (i, 0)),
            pl.BlockSpec((1, d), lambda i: (0, 0)),
            pl.BlockSpec((None, d, D_IN), lambda i: (layer, 0, 0), pipeline_mode=pl.Buffered(1)),
            pl.BlockSpec((1, D_IN), lambda i: (0, 0)),
            rope_spec, rope_spec, rope_spec,
        ],
        out_specs=(pl.BlockSpec((tm, D_IN), lambda i: (i, 0)),
                   pl.BlockSpec((tm // A_TILE, A_HEADS, V_ROWS, A_TILE), lambda i: (i, 0, 0, 0)),
                   pl.BlockSpec((kt, B_HEADS // 2, 2 * B_V_ROWS, K_TILE), lambda i: (i, 0, 0, 0)),
                   pl.BlockSpec((kt, C_HEADS // 2, LANES, K_TILE), lambda i: (i, 0, 0, 0))),
        compiler_params=_params(1),
        name="inproj",
    )(x, g, w_all, q_scale, rc, rs1, rs2)


def _attn_a_kernel(q_ref, k_ref, vt_ref, lam_ref, gt_ref, o_ref, qx_ref, m_ref, mx_ref, s_ref, acc_ref,
                   *, lam_init, nq):
    tq = A_TILE
    n_chunks = tq // CHUNK
    fr = lax.broadcasted_iota(jnp.int32, (2 * tq, LANES), 0)
    fc = lax.broadcasted_iota(jnp.int32, (2 * tq, LANES), 1)
    q_feat = jnp.where((fc < n_chunks) & ((fr % tq) // CHUNK == fc), 1.0, 0.0).astype(BF16)
    q_feat_t = q_feat.astype(F32).T.astype(BF16)
    for blk in range(nq):
        q0, q1 = _split_heads(q_ref[blk * tq:(blk + 1) * tq, :])
        qx_ref[blk, 0:LANES, 0:tq] = q0.astype(F32).T.astype(BF16)
        qx_ref[blk, 0:LANES, tq:2 * tq] = q1.astype(F32).T.astype(BF16)
        qx_ref[blk, LANES:2 * LANES, :] = q_feat_t
    kr = lax.broadcasted_iota(jnp.int32, (tq, LANES), 0)
    kc_ = lax.broadcasted_iota(jnp.int32, (tq, LANES), 1)
    k_feat = jnp.where((kc_ < n_chunks) & (kr // CHUNK > kc_), NEG, 0.0).astype(BF16)
    k_nofeat = jnp.zeros_like(k_feat)

    m_ref[...] = jnp.full(m_ref.shape, M_RESET, F32)

    @pl.when((pl.program_id(0) == 0) & (pl.program_id(1) == 0))
    def _():
        acc_ref[...] = jnp.zeros(acc_ref.shape, F32)

    def score_stage(qa, ka, slot):
        k = k_ref[pl.ds(pl.multiple_of(ka * tq, tq), tq), :]
        kx = jnp.concatenate([k, jnp.where(ka == qa, k_feat, k_nofeat)], axis=1)
        s_new = _dot(kx, qx_ref[qa])
        s_ref[slot] = s_new
        mx_ref[slot, 0:1, :] = jnp.max(s_new, axis=0, keepdims=True)

    def value_stage(qb, kb, slot):
        m_old = jnp.where(kb == 0, M_RESET, m_ref[0:1, :])
        m_new = jnp.maximum(m_old, mx_ref[slot, 0:1, :])
        m_ref[0:1, :] = m_new
        p = jnp.exp2(s_ref[slot] - m_new).astype(BF16)
        acc_ref[qb] = jnp.exp2(m_old - m_new) * acc_ref[qb] + _dot(vt_ref[kb], p)

    def advance(st):
        qa, ka, _, _ = st
        last = ka == qa
        return (jnp.where(last, qa + 1, qa), jnp.where(last, 0, ka + 1), qa, ka)

    def step(st, slot):
        qa, ka, qb, kb = st
        score_stage(qa, ka, slot)
        value_stage(qb, kb, 1 - slot)
        return advance(st)

    def steps(i, st):
        for u in range(A_UNROLL):
            st = step(st, (u + 1) % 2)
        return st

    n_tiles = nq * (nq + 1) // 2
    zero = jnp.int32(0)
    score_stage(zero, zero, 0)
    st = advance((zero, zero, zero, zero))
    n_loop = (n_tiles - 1) // A_UNROLL
    st = lax.fori_loop(0, n_loop, steps, st)
    for f in range(1 + n_loop * A_UNROLL, n_tiles):
        st = step(st, f % 2)
    value_stage(st[2], st[3], (n_tiles - 1) % 2)

    lv = lam_ref[...]
    lam = (jnp.exp(jnp.sum(lv[0:1] * lv[1:2], axis=-1, keepdims=True))
           - jnp.exp(jnp.sum(lv[2:3] * lv[3:4], axis=-1, keepdims=True)) + lam_init)
    for blk in range(nq):
        acc = acc_ref[blk]
        on = acc[0:LANES] * (1.0 / acc[LANES:LANES + 1])
        o = on[:, :tq] - lam * on[:, tq:]
        o = o * lax.rsqrt(jnp.mean(o * o, axis=0, keepdims=True) + EPS) * gt_ref[...] * (1.0 - lam_init)
        o_ref[blk * tq:(blk + 1) * tq, :] = o.T.astype(BF16)


def _attn_a(proj, vt, lam_vecs, subln_gt, lam_init, bsz, seq):
    nq = seq // A_TILE
    return pl.pallas_call(
        functools.partial(_attn_a_kernel, lam_init=lam_init, nq=nq),
        out_shape=jax.ShapeDtypeStruct((bsz * seq, A_HEADS * LANES), BF16),
        grid=(bsz, A_HEADS),
        in_specs=[
            pl.BlockSpec((seq, LANES), lambda b, h: (b, A_Q_BLK + h)),
            pl.BlockSpec((seq, LANES), lambda b, h: (b, A_K_BLK + h)),
            pl.BlockSpec((nq, None, V_ROWS, A_TILE), lambda b, h: (b, h, 0, 0)),
            pl.BlockSpec((4, HEAD_DIM), lambda b, h: (0, 0)),
            pl.BlockSpec((LANES, A_TILE), lambda b, h: (0, 0)),
        ],
        out_specs=pl.BlockSpec((seq, LANES), lambda b, h: (b, h)),
        scratch_shapes=[pltpu.VMEM((nq, 2 * LANES, 2 * A_TILE), BF16), pltpu.VMEM((SUBLANES, 2 * A_TILE), F32),
                        pltpu.VMEM((2, SUBLANES, 2 * A_TILE), F32),
                        pltpu.VMEM((2, A_TILE, 2 * A_TILE), F32),
                        pltpu.VMEM((nq, V_ROWS, 2 * A_TILE), F32)],
        compiler_params=_params(2),
        name="attn_a",
    )(proj, proj, vt, lam_vecs, subln_gt)


def _bias_kernel(base_ref, o_ref):
    win = 3 * LANES
    base = base_ref[...] * math.log2(math.e)
    spread = jnp.broadcast_to(base, (CHUNK, win + LANES))
    ws = tuple(pltpu.roll(spread, shift, 1, stride=1, stride_axis=0)[:, 0:win] for shift in (0, CHUNK))
    far = base[:, win:win + 1]

    shape = (Q_TILE, B_TABLE_W)
    row = lax.broadcasted_iota(jnp.int32, shape, 0)
    col = lax.broadcasted_iota(jnp.int32, shape, 1)
    kc = col // CHUNK - row // CHUNK
    band = (kc >= 0) & (kc <= B_LEFT_CHUNKS)

    win_start = (3 * LANES, 3 * LANES, 4 * LANES, 4 * LANES)
    rows = []
    for c in range(Q_TILE // CHUNK):
        pieces = []
        for j in range(B_TABLE_W // LANES):
            lo = j * LANES
            bnd = band[c * CHUNK:(c + 1) * CHUNK, lo:lo + LANES]
            if win_start[c] <= lo < win_start[c] + win:
                off = lo - win_start[c]
                pieces.append(jnp.where(bnd, ws[c % 2][:, off:off + LANES], NEG))
            else:
                pieces.append(jnp.where(bnd, far, NEG))
        rows.append(jnp.concatenate(pieces, axis=1))
    table = jnp.concatenate(rows, axis=0)
    for v in range(3):
        c0 = (2 - v) * K_TILE
        o_ref[v] = table[:, c0:c0 + B_KEYS].T


def _bias_tiles(rel_bias):
    heads = rel_bias.shape[0]
    win = 3 * LANES
    base = jnp.concatenate([rel_bias[:, ::-1],
                            jnp.broadcast_to(rel_bias[:, :1], (heads, win - 2 * REL_CLIP - 1)),
                            jnp.broadcast_to(rel_bias[:, -1:], (heads, LANES))], axis=1).astype(F32)
    return pl.pallas_call(
        _bias_kernel,
        out_shape=jax.ShapeDtypeStruct((3, heads, B_KEYS, Q_TILE), F32),
        grid=(heads,),
        in_specs=[pl.BlockSpec((None, 1, win + LANES), lambda h: (h, 0, 0))],
        out_specs=pl.BlockSpec((3, None, B_KEYS, Q_TILE), lambda h: (0, h, 0, 0)),
        compiler_params=_params(1),
        name="bias_tiles",
    )(base.reshape(heads, 1, win + LANES))


def _attn_b_kernel(q_ref, k_ref, vt_ref, bias_ref, gt_ref, o_ref, s_ref, mx_ref, res_ref, *, nq):
    tq = Q_TILE
    s_ref[1] = jnp.zeros(s_ref.shape[1:], F32)
    mx_ref[...] = jnp.zeros(mx_ref.shape, F32)
    gt = gt_ref[...]

    def score_stage(i, slot):
        first_key = pl.multiple_of(jnp.maximum(i - 2, 0) * K_TILE, K_TILE)
        qcat = jnp.concatenate(_split_heads(q_ref[pl.ds(pl.multiple_of(i * tq, tq), tq), :]), axis=0)
        var = jnp.minimum(i, 2)
        s = (_dot_t(k_ref[pl.ds(first_key, B_KEYS), :], qcat)
             + jnp.concatenate([bias_ref[var, 0], bias_ref[var, 1]], axis=1))
        s_ref[slot] = s
        mx_ref[slot, 0:1, :] = jnp.max(s, axis=0, keepdims=True)

    def value_stage(i, slot):
        jb = jnp.maximum(i - 2, 0)
        p = jnp.exp2(s_ref[slot] - mx_ref[slot, 0:1, :]).astype(BF16)
        outs = []
        for n in range(2):
            acc = sum(_dot(vt_ref[jb + t][n * B_V_ROWS:(n + 1) * B_V_ROWS, :],
                           p[t * K_TILE:(t + 1) * K_TILE, n * tq:(n + 1) * tq]) for t in range(B_KEYS // K_TILE))
            o = acc[0:HEAD_DIM] * (1.0 / acc[HEAD_DIM:HEAD_DIM + 1])
            rs = lax.rsqrt(jnp.mean(o * o, axis=0, keepdims=True) + EPS)
            outs.append(o * rs * gt[n * HEAD_DIM:(n + 1) * HEAD_DIM, :])
        res_ref[jnp.where(i < 0, nq, i)] = jnp.concatenate(outs, axis=0).T.astype(BF16)

    def step(i, slot):
        score_stage(i, slot)
        value_stage(i - 1, 1 - slot)

    def steps(it, carry):
        for u in range(B_UNROLL):
            step(B_UNROLL * it + u, u % 2)
        return carry

    lax.fori_loop(0, nq // B_UNROLL, steps, 0)
    for i in range(nq // B_UNROLL * B_UNROLL, nq):
        step(i, i % 2)
    value_stage(nq - 1, (nq - 1) % 2)
    for blk in range(nq):
        o_ref[blk * tq:(blk + 1) * tq, :] = res_ref[blk]


def _attn_b(proj, vt, bias_tiles, gn_t, bsz, seq):
    nq = seq // Q_TILE
    nk = seq // K_TILE
    pairs = B_HEADS // 2
    return pl.pallas_call(
        functools.partial(_attn_b_kernel, nq=nq),
        out_shape=jax.ShapeDtypeStruct((bsz * seq, pairs * LANES), BF16),
        grid=(bsz, pairs),
        in_specs=[
            pl.BlockSpec((seq, LANES), lambda b, h: (b, B_Q_BLK + h)),
            pl.BlockSpec((seq, LANES), lambda b, h: (b, B_K_BLK + h)),
            pl.BlockSpec((nk, None, 2 * B_V_ROWS, K_TILE), lambda b, h: (b, h, 0, 0)),
            pl.BlockSpec((3, 2, B_KEYS, Q_TILE), lambda b, h: (0, h, 0, 0)),
            pl.BlockSpec((LANES, Q_TILE), lambda b, h: (h, 0)),
        ],
        out_specs=pl.BlockSpec((seq, LANES), lambda b, h: (b, h)),
        scratch_shapes=[pltpu.VMEM((2, B_KEYS, 2 * Q_TILE), F32), pltpu.VMEM((2, SUBLANES, 2 * Q_TILE), F32),
                        pltpu.VMEM((nq + 1, Q_TILE, LANES), BF16)],
        compiler_params=_params(2),
        name="attn_b",
    )(proj, proj, vt, bias_tiles, gn_t)


def _softplus2(z):
    return jnp.maximum(z, jnp.log2(1.0 + jnp.exp2(jnp.minimum(z, SP_SWITCH))))


def _attn_c_kernel(q_ref, k_ref, vt_ref, mask_ref, gt_ref, o_ref, *, n_sub):
    tq, tk = Q_TILE, K_TILE
    tri_r = lax.broadcasted_iota(jnp.int32, (tk, tk), 0)
    tri_c = lax.broadcasted_iota(jnp.int32, (tk, tk), 1)
    later = jnp.where(tri_c > tri_r, 1.0, 0.0).astype(BF16)

    def suffix_sums(x):
        return _dot(later, x.astype(BF16))

    def values(vt, a):
        return tuple(_dot(vt[n * HEAD_DIM:(n + 1) * HEAD_DIM, :], a[:, n * tq:(n + 1) * tq]) for n in range(2))

    subs = range(n_sub)
    qis = [pl.program_id(2) * n_sub + u for u in subs]
    qcats = [jnp.concatenate(_split_heads(q_ref[u * tq:(u + 1) * tq, :]), axis=0) for u in subs]

    j0s = [jnp.maximum(qi - 1, 0) for qi in qis]
    zs = [_dot_t(k_ref[pl.ds(pl.multiple_of(j0 * tk, tk), 2 * tk), :], qcat) + mask_ref[jnp.minimum(qi, 1)]
          for j0, qi, qcat in zip(j0s, qis, qcats)]
    sps = [_softplus2(z) for z in zs]
    gone = [jnp.concatenate([suffix_sums(sp[:tk]) + jnp.sum(sp[tk:], axis=0, keepdims=True),
                             suffix_sums(sp[tk:])], axis=0) for sp in sps]
    a_s = [jnp.exp2((z - sp) - g).astype(BF16) for z, sp, g in zip(zs, sps, gone)]
    cs = [jnp.sum(sp, axis=0, keepdims=True) for sp in sps]
    accs = [tuple(x + y for x, y in zip(values(vt_ref[j0], a[:tk]), values(vt_ref[j0 + 1], a[tk:])))
            for j0, a in zip(j0s, a_s)]

    gt = gt_ref[...]
    for u in subs:
        j0, qcat = j0s[u], qcats[u]

        def live(state):
            t, c, _, _ = state
            return (t < j0) & (jnp.min(c) < -C_EXIT)

        def sweep(state):
            t, c, acc0, acc1 = state
            kj = j0 - 1 - t
            z = _dot_t(k_ref[pl.ds(pl.multiple_of(kj * tk, tk), tk), :], qcat)
            sp = _softplus2(z)
            a = jnp.exp2((z - sp) - suffix_sums(sp) - c).astype(BF16)
            d0, d1 = values(vt_ref[kj], a)
            return t + 1, c + jnp.sum(sp, axis=0, keepdims=True), acc0 + d0, acc1 + d1

        _, _, acc0, acc1 = lax.while_loop(live, sweep, (jnp.int32(0), cs[u], accs[u][0], accs[u][1]))
        outs = []
        for n, o in enumerate((acc0, acc1)):
            rs = lax.rsqrt(jnp.mean(o * o, axis=0, keepdims=True) + EPS)
            outs.append(o * rs * gt[n * HEAD_DIM:(n + 1) * HEAD_DIM, :])
        o_ref[u * tq:(u + 1) * tq, :] = jnp.concatenate(outs, axis=0).T.astype(BF16)


def _causal_bias_c():
    key = jnp.arange(2 * K_TILE)[:, None]
    query = jnp.arange(2 * Q_TILE)[None, :] % Q_TILE
    return jnp.stack([jnp.where(key < query + off, 0.0, NEG) for off in (0, K_TILE)]).astype(F32)


def _attn_c(proj, vt, gn_t, bsz, seq, n_sub=4):
    nq = seq // (Q_TILE * n_sub)
    nk = seq // K_TILE
    pairs = C_HEADS // 2
    return pl.pallas_call(
        functools.partial(_attn_c_kernel, n_sub=n_sub),
        out_shape=jax.ShapeDtypeStruct((bsz * seq, pairs * LANES), BF16),
        grid=(bsz, pairs, nq),
        in_specs=[
            pl.BlockSpec((n_sub * Q_TILE, LANES), lambda b, h, i: (b * nq + i, C_Q_BLK + h)),
            pl.BlockSpec((seq, LANES), lambda b, h, i: (b, C_K_BLK + h)),
            pl.BlockSpec((nk, None, LANES, K_TILE), lambda b, h, i: (b, h, 0, 0)),
            pl.BlockSpec((2, 2 * K_TILE, 2 * Q_TILE), lambda b, h, i: (0, 0, 0)),
            pl.BlockSpec((LANES, Q_TILE), lambda b, h, i: (h, 0)),
        ],
        out_specs=pl.BlockSpec((n_sub * Q_TILE, LANES), lambda b, h, i: (b * nq + i, h)),
        compiler_params=_params(3),
        name="attn_c",
    )(proj, proj, vt, _causal_bias_c(), gn_t)


def _outmlp_kernel(x_ref, oa_ref, ob_ref, oc_ref, wo_ref, g1_ref, g2_ref, wu_ref, wd_ref, g3_ref, o_ref, acc_ref,
                   *, f_chunk, n_part):
    wa, wb = oa_ref.shape[1], ob_ref.shape[1]
    rows = x_ref.shape[0] // n_part
    parts = [slice(r * rows, (r + 1) * rows) for r in range(n_part)]
    d_ff = wu_ref.shape[1]
    ys = [_dot(oa_ref[p, :], wo_ref[0:wa, :].astype(BF16)) + _dot(ob_ref[p, :], wo_ref[wa:wa + wb, :].astype(BF16))
          + _dot(oc_ref[p, :], wo_ref[wa + wb:, :].astype(BF16)) for p in parts]
    x1s = [x_ref[p, :] + _rms(y, g1_ref[...]) for p, y in zip(parts, ys)]
    hs = [_rms(x1, g2_ref[...]).astype(BF16) for x1 in x1s]
    for p, x1, h in zip(parts, x1s, hs):
        for f0 in range(0, d_ff, f_chunk):
            u = jnp.maximum(_dot(h, wu_ref[:, f0:f0 + f_chunk]), 0.0)
            part = _dot((u * u).astype(BF16), wd_ref[f0:f0 + f_chunk, :])
            if f0 == 0:
                acc_ref[p, :] = part
            else:
                acc_ref[p, :] += part
        o_ref[p, :] = x1 + _rms(acc_ref[p, :], g3_ref[...])


def _outmlp(x, oa, ob, oc, w_out_all, layer, g_post, g_pre, w_up, w_down, g_post2, tm=1024, f_chunk=1024, n_part=4):
    m, d = x.shape
    d_ff = w_up.shape[1]
    row = lambda w: pl.BlockSpec((tm, w), lambda i: (i, 0))
    full = lambda a, b: pl.BlockSpec((a, b), lambda i: (0, 0), pipeline_mode=pl.Buffered(1))
    return pl.pallas_call(
        functools.partial(_outmlp_kernel, f_chunk=f_chunk, n_part=n_part),
        out_shape=jax.ShapeDtypeStruct((m, d), F32),
        grid=(m // tm,),
        in_specs=[
            row(d), row(oa.shape[1]), row(ob.shape[1]), row(oc.shape[1]),
            pl.BlockSpec((None, d, d), lambda i: (layer, 0, 0), pipeline_mode=pl.Buffered(1)),
            full(1, d), full(1, d), full(d, d_ff), full(d_ff, d), full(1, d),
        ],
        out_specs=row(d),
        scratch_shapes=[pltpu.VMEM((tm, d), F32)],
        compiler_params=_params(1),
        name="outproj_mlp",
    )(x, oa, ob, oc, w_out_all, g_post, g_pre, w_up, w_down, g_post2)


def _rope_tables(seq):
    pos = jnp.arange(seq, dtype=F32)
    inv_freq = ROPE_THETA ** (-jnp.arange(0, ROPE_DIM, 2, dtype=F32) / ROPE_DIM)
    ang = pos[:, None] * inv_freq[None, :]
    cos, sin = jnp.cos(ang), jnp.sin(ang)
    half = ROPE_DIM // 2
    d = jnp.arange(LANES) % HEAD_DIM
    f = d % half
    rc = jnp.where(d[None, :] < ROPE_DIM, cos[:, f], 1.0)
    rs1 = jnp.where(d[None, :] < half, -sin[:, f], 0.0)
    rs2 = jnp.where((d[None, :] >= half) & (d[None, :] < ROPE_DIM), sin[:, f], 0.0)
    return rc.astype(F32), rs1.astype(F32), rs2.astype(F32)


def _q_scale_columns():
    blk = jnp.arange(D_IN) // LANES
    is_q = (blk < A_K_BLK) | ((blk >= B_Q_BLK) & (blk < B_K_BLK)) | ((blk >= C_Q_BLK) & (blk < C_K_BLK))
    return jnp.where(is_q, HEAD_DIM ** -0.5 * math.log2(math.e), 1.0).astype(F32)


def _gain_t(g, n_blk, width):
    return jnp.broadcast_to(g.reshape(n_blk * LANES, 1), (n_blk * LANES, width)).astype(F32)


def kernel(x, norm_pre_mix, w_in, lam_q1, lam_k1, lam_q2, lam_k2, subln_a, rel_bias, gn_b, gn_c, w_out,
           norm_post_mix, norm_pre_mlp, w_up, w_down, norm_post_mlp):
    bsz, seq, d = x.shape
    depth = w_in.shape[0]
    xf = x.reshape(bsz * seq, d)
    rc, rs1, rs2 = _rope_tables(seq)
    q_scale = _q_scale_columns()
    row = lambda v: v.reshape(1, -1)
    for l in range(depth):
        lam_init = 0.8 - 0.6 * math.exp(-0.3 * l)
        proj, vt_a, vt_b, vt_c = _inproj(xf, row(norm_pre_mix[l]), w_in, l, row(q_scale), rc, rs1, rs2, seq)
        lam_vecs = jnp.stack([lam_q1[l], lam_k1[l], lam_q2[l], lam_k2[l]]).astype(F32)
        oa = _attn_a(proj, vt_a, lam_vecs, _gain_t(subln_a[l], 1, A_TILE), lam_init, bsz, seq)
        ob = _attn_b(proj, vt_b, _bias_tiles(rel_bias[l]), _gain_t(gn_b[l], B_HEADS // 2, Q_TILE), bsz, seq)
        oc = _attn_c(proj, vt_c, _gain_t(gn_c[l], C_HEADS // 2, Q_TILE), bsz, seq)
        xf = _outmlp(xf, oa, ob, oc, w_out, l, row(norm_post_mix[l]), row(norm_pre_mlp[l]),
                     w_up[l].astype(BF16), w_down[l].astype(BF16), row(norm_post_mlp[l]))
    return xf.reshape(bsz, seq, d)
```

```python
import functools
import math

import jax
import jax.numpy as jnp
from jax import lax
from jax.experimental import pallas as pl
from jax.experimental.pallas import tpu as pltpu

F32 = jnp.float32
BF16 = jnp.bfloat16

EPS = 1e-6
ROPE_THETA = 500000.0
HEAD_DIM = 64
ROPE_DIM = HEAD_DIM // 4
CHUNK = 64
LANES = 128
SUBLANES = 8
A_HEADS = 4
B_HEADS = 4
C_HEADS = 4
B_LEFT_CHUNKS = 8
REL_CLIP = 128
NEG = -1e30
M_RESET = -5e29
SP_SWITCH = 64.0
C_EXIT = -160.0

A_Q_BLK, A_K_BLK, A_V_BLK = 0, 4, 8
B_Q_BLK, B_K_BLK, B_V_BLK = 12, 14, 16
C_Q_BLK, C_K_BLK, C_V_BLK = 18, 20, 22
D_IN = 24 * LANES

Q_TILE = 256
K_TILE = 256
A_TILE = 512
A_UNROLL = 4
B_UNROLL = 8
V_ROWS = LANES + 16
B_KEYS = 3 * K_TILE
B_TABLE_W = B_KEYS + 2 * K_TILE
B_V_ROWS = HEAD_DIM + 16

VMEM_LIMIT = 56 * 1024 * 1024


def _params(n_axes, vmem=VMEM_LIMIT):
    return pltpu.CompilerParams(dimension_semantics=("arbitrary",) * n_axes, vmem_limit_bytes=vmem)


def _rms(x, g):
    return x * lax.rsqrt(jnp.mean(x * x, axis=-1, keepdims=True) + EPS) * g


def _dot_t(a, b):
    return lax.dot_general(a, b, (((1,), (1,)), ((), ())), preferred_element_type=F32)


def _dot(a, b):
    return jnp.dot(a, b, preferred_element_type=F32)


def _split_heads(q):
    lane = lax.broadcasted_iota(jnp.int32, q.shape, 1)
    zero = jnp.zeros_like(q)
    return jnp.where(lane < HEAD_DIM, q, zero), jnp.where(lane >= HEAD_DIM, q, zero)


def _inproj_kernel(x_ref, g_ref, w_ref, qs_ref, rc_ref, rs1_ref, rs2_ref, o_ref, vta_ref, vtb_ref, vtc_ref, *, n_chunk):
    h = _rms(x_ref[...], g_ref[...]).astype(BF16)
    rc, rs1, rs2 = rc_ref[...], rs1_ref[...], rs2_ref[...]
    tm = x_ref.shape[0]
    pad_rows = V_ROWS - LANES
    ones_rows = jnp.where(lax.broadcasted_iota(jnp.int32, (pad_rows, A_TILE), 0) == 0, 1.0, 0.0).astype(BF16)
    ones_rows_k = jnp.where(lax.broadcasted_iota(jnp.int32, (pad_rows, K_TILE), 0) == 0, 1.0, 0.0).astype(BF16)
    kt = tm // K_TILE
    for n0 in range(0, D_IN, n_chunk):
        y = _dot(h, w_ref[:, n0:n0 + n_chunk].astype(BF16))
        for j in range(n_chunk // LANES):
            blk = n0 // LANES + j
            yj = y[:, j * LANES:(j + 1) * LANES]
            if blk < A_K_BLK or B_Q_BLK <= blk < B_K_BLK or C_Q_BLK <= blk < C_K_BLK:
                yj = yj * qs_ref[:, blk * LANES:(blk + 1) * LANES]
            if blk < A_V_BLK:
                yj = yj * rc + pltpu.roll(yj, LANES - ROPE_DIM // 2, 1) * rs1 + pltpu.roll(yj, ROPE_DIM // 2, 1) * rs2
            o_ref[:, n0 + j * LANES:n0 + (j + 1) * LANES] = yj.astype(BF16)
            if A_V_BLK <= blk < B_Q_BLK:
                yt = yj.T.astype(BF16)
                for t in range(tm // A_TILE):
                    vta_ref[t, blk - A_V_BLK, 0:LANES, :] = yt[:, t * A_TILE:(t + 1) * A_TILE]
                    vta_ref[t, blk - A_V_BLK, LANES:V_ROWS, :] = ones_rows
            if B_V_BLK <= blk < C_Q_BLK:
                yt = yj.T.astype(BF16)
                for t in range(kt):
                    for n in range(2):
                        r0 = n * B_V_ROWS
                        vtb_ref[t, blk - B_V_BLK, r0:r0 + HEAD_DIM, :] = (
                            yt[n * HEAD_DIM:(n + 1) * HEAD_DIM, t * K_TILE:(t + 1) * K_TILE])
                        vtb_ref[t, blk - B_V_BLK, r0 + HEAD_DIM:r0 + B_V_ROWS, :] = ones_rows_k
            if blk >= C_V_BLK:
                yt = yj.T.astype(BF16)
                for t in range(kt):
                    vtc_ref[t, blk - C_V_BLK] = yt[:, t * K_TILE:(t + 1) * K_TILE]


def _inproj(x, g, w_all, layer, q_scale, rc, rs1, rs2, seq, tm=2 * A_TILE, n_chunk=512):
    m, d = x.shape
    pos_blocks = seq // tm
    rope_spec = pl.BlockSpec((tm, LANES), lambda i: (i % pos_blocks, 0))
    kt = tm // K_TILE
    return pl.pallas_call(
        functools.partial(_inproj_kernel, n_chunk=n_chunk),
        out_shape=(jax.ShapeDtypeStruct((m, D_IN), BF16),
                   jax.ShapeDtypeStruct((m // A_TILE, A_HEADS, V_ROWS, A_TILE), BF16),
                   jax.ShapeDtypeStruct((m // K_TILE, B_HEADS // 2, 2 * B_V_ROWS, K_TILE), BF16),
                   jax.ShapeDtypeStruct((m // K_TILE, C_HEADS // 2, LANES, K_TILE), BF16)),
        grid=(m // tm,),
        in_specs=[
            pl.BlockSpec((tm, d), lambda i: (i, 0)),
            pl.BlockSpec((1, d), lambda i: (0, 0)),
            pl.BlockSpec((None, d, D_IN), lambda i: (layer, 0, 0), pipeline_mode=pl.Buffered(1)),
            pl.BlockSpec((1, D_IN), lambda i: (0, 0)),
            rope_spec, rope_spec, rope_spec,
        ],
        out_specs=(pl.BlockSpec((tm, D_IN), lambda i: (i, 0)),
                   pl.BlockSpec((tm // A_TILE, A_HEADS, V_ROWS, A_TILE), lambda i: (i, 0, 0, 0)),
                   pl.BlockSpec((kt, B_HEADS // 2, 2 * B_V_ROWS, K_TILE), lambda i: (i, 0, 0, 0)),
                   pl.BlockSpec((kt, C_HEADS // 2, LANES, K_TILE), lambda i: (i, 0, 0, 0))),
        compiler_params=_params(1),
        name="inproj",
    )(x, g, w_all, q_scale, rc, rs1, rs2)


def _attn_a_kernel(q_ref, k_ref, vt_ref, lam_ref, gt_ref, o_ref, qx_ref, m_ref, mx_ref, s_ref, acc_ref,
                   *, lam_init, nq):
    tq = A_TILE
    n_chunks = tq // CHUNK
    fr = lax.broadcasted_iota(jnp.int32, (2 * tq, LANES), 0)
    fc = lax.broadcasted_iota(jnp.int32, (2 * tq, LANES), 1)
    q_feat = jnp.where((fc < n_chunks) & ((fr % tq) // CHUNK == fc), 1.0, 0.0).astype(BF16)
    q_feat_t = q_feat.astype(F32).T.astype(BF16)
    for blk in range(nq):
        q0, q1 = _split_heads(q_ref[blk * tq:(blk + 1) * tq, :])
        qx_ref[blk, 0:LANES, 0:tq] = q0.astype(F32).T.astype(BF16)
        qx_ref[blk, 0:LANES, tq:2 * tq] = q1.astype(F32).T.astype(BF16)
        qx_ref[blk, LANES:2 * LANES, :] = q_feat_t
    kr = lax.broadcasted_iota(jnp.int32, (tq, LANES), 0)
    kc_ = lax.broadcasted_iota(jnp.int32, (tq, LANES), 1)
    k_feat = jnp.where((kc_ < n_chunks) & (kr // CHUNK > kc_), NEG, 0.0).astype(BF16)
    k_nofeat = jnp.zeros_like(k_feat)

    m_ref[...] = jnp.full(m_ref.shape, M_RESET, F32)

    @pl.when((pl.program_id(0) == 0) & (pl.program_id(1) == 0))
    def _():
        acc_ref[...] = jnp.zeros(acc_ref.shape, F32)

    def score_stage(qa, ka, slot):
        k = k_ref[pl.ds(pl.multiple_of(ka * tq, tq), tq), :]
        kx = jnp.concatenate([k, jnp.where(ka == qa, k_feat, k_nofeat)], axis=1)
        s_new = _dot(kx, qx_ref[qa])
        s_ref[slot] = s_new
        mx_ref[slot, 0:1, :] = jnp.max(s_new, axis=0, keepdims=True)

    def value_stage(qb, kb, slot):
        m_old = jnp.where(kb == 0, M_RESET, m_ref[0:1, :])
        m_new = jnp.maximum(m_old, mx_ref[slot, 0:1, :])
        m_ref[0:1, :] = m_new
        p = jnp.exp2(s_ref[slot] - m_new).astype(BF16)
        acc_ref[qb] = jnp.exp2(m_old - m_new) * acc_ref[qb] + _dot(vt_ref[kb], p)

    def advance(st):
        qa, ka, _, _ = st
        last = ka == qa
        return (jnp.where(last, qa + 1, qa), jnp.where(last, 0, ka + 1), qa, ka)

    def step(st, slot):
        qa, ka, qb, kb = st
        score_stage(qa, ka, slot)
        value_stage(qb, kb, 1 - slot)
        return advance(st)

    def steps(i, st):
        for u in range(A_UNROLL):
            st = step(st, (u + 1) % 2)
        return st

    n_tiles = nq * (nq + 1) // 2
    zero = jnp.int32(0)
    score_stage(zero, zero, 0)
    st = advance((zero, zero, zero, zero))
    n_loop = (n_tiles - 1) // A_UNROLL
    st = lax.fori_loop(0, n_loop, steps, st)
    for f in range(1 + n_loop * A_UNROLL, n_tiles):
        st = step(st, f % 2)
    value_stage(st[2], st[3], (n_tiles - 1) % 2)

    lv = lam_ref[...]
    lam = (jnp.exp(jnp.sum(lv[0:1] * lv[1:2], axis=-1, keepdims=True))
           - jnp.exp(jnp.sum(lv[2:3] * lv[3:4], axis=-1, keepdims=True)) + lam_init)
    for blk in range(nq):
        acc = acc_ref[blk]
        on = acc[0:LANES] * (1.0 / acc[LANES:LANES + 1])
        o = on[:, :tq] - lam * on[:, tq:]
        o = o * lax.rsqrt(jnp.mean(o * o, axis=0, keepdims=True) + EPS) * gt_ref[...] * (1.0 - lam_init)
        o_ref[blk * tq:(blk + 1) * tq, :] = o.T.astype(BF16)


def _attn_a(proj, vt, lam_vecs, subln_gt, lam_init, bsz, seq):
    nq = seq // A_TILE
    return pl.pallas_call(
        functools.partial(_attn_a_kernel, lam_init=lam_init, nq=nq),
        out_shape=jax.ShapeDtypeStruct((bsz * seq, A_HEADS * LANES), BF16),
        grid=(bsz, A_HEADS),
        in_specs=[
            pl.BlockSpec((seq, LANES), lambda b, h: (b, A_Q_BLK + h)),
            pl.BlockSpec((seq, LANES), lambda b, h: (b, A_K_BLK + h)),
            pl.BlockSpec((nq, None, V_ROWS, A_TILE), lambda b, h: (b, h, 0, 0)),
            pl.BlockSpec((4, HEAD_DIM), lambda b, h: (0, 0)),
            pl.BlockSpec((LANES, A_TILE), lambda b, h: (0, 0)),
        ],
        out_specs=pl.BlockSpec((seq, LANES), lambda b, h: (b, h)),
        scratch_shapes=[pltpu.VMEM((nq, 2 * LANES, 2 * A_TILE), BF16), pltpu.VMEM((SUBLANES, 2 * A_TILE), F32),
                        pltpu.VMEM((2, SUBLANES, 2 * A_TILE), F32),
                        pltpu.VMEM((2, A_TILE, 2 * A_TILE), F32),
                        pltpu.VMEM((nq, V_ROWS, 2 * A_TILE), F32)],
        compiler_params=_params(2),
        name="attn_a",
    )(proj, proj, vt, lam_vecs, subln_gt)


def _bias_kernel(base_ref, o_ref):
    win = 3 * LANES
    base = base_ref[...] * math.log2(math.e)
    spread = jnp.broadcast_to(base, (CHUNK, win + LANES))
    ws = tuple(pltpu.roll(spread, shift, 1, stride=1, stride_axis=0)[:, 0:win] for shift in (0, CHUNK))
    far = base[:, win:win + 1]

    shape = (Q_TILE, B_TABLE_W)
    row = lax.broadcasted_iota(jnp.int32, shape, 0)
    col = lax.broadcasted_iota(jnp.int32, shape, 1)
    kc = col // CHUNK - row // CHUNK
    band = (kc >= 0) & (kc <= B_LEFT_CHUNKS)

    win_start = (3 * LANES, 3 * LANES, 4 * LANES, 4 * LANES)
    rows = []
    for c in range(Q_TILE // CHUNK):
        pieces = []
        for j in range(B_TABLE_W // LANES):
            lo = j * LANES
            bnd = band[c * CHUNK:(c + 1) * CHUNK, lo:lo + LANES]
            if win_start[c] <= lo < win_start[c] + win:
                off = lo - win_start[c]
                pieces.append(jnp.where(bnd, ws[c % 2][:, off:off + LANES], NEG))
            else:
                pieces.append(jnp.where(bnd, far, NEG))
        rows.append(jnp.concatenate(pieces, axis=1))
    table = jnp.concatenate(rows, axis=0)
    for v in range(3):
        c0 = (2 - v) * K_TILE
        o_ref[v] = table[:, c0:c0 + B_KEYS].T


def _bias_tiles(rel_bias):
    heads = rel_bias.shape[0]
    win = 3 * LANES
    base = jnp.concatenate([rel_bias[:, ::-1],
                            jnp.broadcast_to(rel_bias[:, :1], (heads, win - 2 * REL_CLIP - 1)),
                            jnp.broadcast_to(rel_bias[:, -1:], (heads, LANES))], axis=1).astype(F32)
    return pl.pallas_call(
        _bias_kernel,
        out_shape=jax.ShapeDtypeStruct((3, heads, B_KEYS, Q_TILE), F32),
        grid=(heads,),
        in_specs=[pl.BlockSpec((None, 1, win + LANES), lambda h: (h, 0, 0))],
        out_specs=pl.BlockSpec((3, None, B_KEYS, Q_TILE), lambda h: (0, h, 0, 0)),
        compiler_params=_params(1),
        name="bias_tiles",
    )(base.reshape(heads, 1, win + LANES))


def _attn_b_kernel(q_ref, k_ref, vt_ref, bias_ref, gt_ref, o_ref, s_ref, mx_ref, res_ref, *, nq):
    tq = Q_TILE
    s_ref[1] = jnp.zeros(s_ref.shape[1:], F32)
    mx_ref[...] = jnp.zeros(mx_ref.shape, F32)
    gt = gt_ref[...]

    def score_stage(i, slot):
        first_key = pl.multiple_of(jnp.maximum(i - 2, 0) * K_TILE, K_TILE)
        qcat = jnp.concatenate(_split_heads(q_ref[pl.ds(pl.multiple_of(i * tq, tq), tq), :]), axis=0)
        var = jnp.minimum(i, 2)
        s = (_dot_t(k_ref[pl.ds(first_key, B_KEYS), :], qcat)
             + jnp.concatenate([bias_ref[var, 0], bias_ref[var, 1]], axis=1))
        s_ref[slot] = s
        mx_ref[slot, 0:1, :] = jnp.max(s, axis=0, keepdims=True)

    def value_stage(i, slot):
        jb = jnp.maximum(i - 2, 0)
        p = jnp.exp2(s_ref[slot] - mx_ref[slot, 0:1, :]).astype(BF16)
        outs = []
        for n in range(2):
            acc = sum(_dot(vt_ref[jb + t][n * B_V_ROWS:(n + 1) * B_V_ROWS, :],
                           p[t * K_TILE:(t + 1) * K_TILE, n * tq:(n + 1) * tq]) for t in range(B_KEYS // K_TILE))
            o = acc[0:HEAD_DIM] * (1.0 / acc[HEAD_DIM:HEAD_DIM + 1])
            rs = lax.rsqrt(jnp.mean(o * o, axis=0, keepdims=True) + EPS)
            outs.append(o * rs * gt[n * HEAD_DIM:(n + 1) * HEAD_DIM, :])
        res_ref[jnp.where(i < 0, nq, i)] = jnp.concatenate(outs, axis=0).T.astype(BF16)

    def step(i, slot):
        score_stage(i, slot)
        value_stage(i - 1, 1 - slot)

    def steps(it, carry):
        for u in range(B_UNROLL):
            step(B_UNROLL * it + u, u % 2)
        return carry

    lax.fori_loop(0, nq // B_UNROLL, steps, 0)
    for i in range(nq // B_UNROLL * B_UNROLL, nq):
        step(i, i % 2)
    value_stage(nq - 1, (nq - 1) % 2)
    for blk in range(nq):
        o_ref[blk * tq:(blk + 1) * tq, :] = res_ref[blk]


def _attn_b(proj, vt, bias_tiles, gn_t, bsz, seq):
    nq = seq // Q_TILE
    nk = seq // K_TILE
    pairs = B_HEADS // 2
    return pl.pallas_call(
        functools.partial(_attn_b_kernel, nq=nq),
        out_shape=jax.ShapeDtypeStruct((bsz * seq, pairs * LANES), BF16),
        grid=(bsz, pairs),
        in_specs=[
            pl.BlockSpec((seq, LANES), lambda b, h: (b, B_Q_BLK + h)),
            pl.BlockSpec((seq, LANES), lambda b, h: (b, B_K_BLK + h)),
            pl.BlockSpec((nk, None, 2 * B_V_ROWS, K_TILE), lambda b, h: (b, h, 0, 0)),
            pl.BlockSpec((3, 2, B_KEYS, Q_TILE), lambda b, h: (0, h, 0, 0)),
            pl.BlockSpec((LANES, Q_TILE), lambda b, h: (h, 0)),
        ],
        out_specs=pl.BlockSpec((seq, LANES), lambda b, h: (b, h)),
        scratch_shapes=[pltpu.VMEM((2, B_KEYS, 2 * Q_TILE), F32), pltpu.VMEM((2, SUBLANES, 2 * Q_TILE), F32),
                        pltpu.VMEM((nq + 1, Q_TILE, LANES), BF16)],
        compiler_params=_params(2),
        name="attn_b",
    )(proj, proj, vt, bias_tiles, gn_t)


def _softplus2(z):
    return jnp.maximum(z, jnp.log2(1.0 + jnp.exp2(jnp.minimum(z, SP_SWITCH))))


def _attn_c_kernel(q_ref, k_ref, vt_ref, mask_ref, gt_ref, o_ref, *, n_sub):
    tq, tk = Q_TILE, K_TILE
    tri_r = lax.broadcasted_iota(jnp.int32, (tk, tk), 0)
    tri_c = lax.broadcasted_iota(jnp.int32, (tk, tk), 1)
    later = jnp.where(tri_c > tri_r, 1.0, 0.0).astype(BF16)

    def suffix_sums(x):
        return _dot(later, x.astype(BF16))

    def values(vt, a):
        return tuple(_dot(vt[n * HEAD_DIM:(n + 1) * HEAD_DIM, :], a[:, n * tq:(n + 1) * tq]) for n in range(2))

    subs = range(n_sub)
    qis = [pl.program_id(2) * n_sub + u for u in subs]
    qcats = [jnp.concatenate(_split_heads(q_ref[u * tq:(u + 1) * tq, :]), axis=0) for u in subs]

    j0s = [jnp.maximum(qi - 1, 0) for qi in qis]
    zs = [_dot_t(k_ref[pl.ds(pl.multiple_of(j0 * tk, tk), 2 * tk), :], qcat) + mask_ref[jnp.minimum(qi, 1)]
          for j0, qi, qcat in zip(j0s, qis, qcats)]
    sps = [_softplus2(z) for z in zs]
    gone = [jnp.concatenate([suffix_sums(sp[:tk]) + jnp.sum(sp[tk:], axis=0, keepdims=True),
                             suffix_sums(sp[tk:])], axis=0) for sp in sps]
    a_s = [jnp.exp2((z - sp) - g).astype(BF16) for z, sp, g in zip(zs, sps, gone)]
    cs = [jnp.sum(sp, axis=0, keepdims=True) for sp in sps]
    accs = [tuple(x + y for x, y in zip(values(vt_ref[j0], a[:tk]), values(vt_ref[j0 + 1], a[tk:])))
            for j0, a in zip(j0s, a_s)]

    gt = gt_ref[...]
    for u in subs:
        j0, qcat = j0s[u], qcats[u]

        def live(state):
            t, c, _, _ = state
            return (t < j0) & (jnp.min(c) < -C_EXIT)

        def sweep(state):
            t, c, acc0, acc1 = state
            kj = j0 - 1 - t
            z = _dot_t(k_ref[pl.ds(pl.multiple_of(kj * tk, tk), tk), :], qcat)
            sp = _softplus2(z)
            a = jnp.exp2((z - sp) - suffix_sums(sp) - c).astype(BF16)
            d0, d1 = values(vt_ref[kj], a)
            return t + 1, c + jnp.sum(sp, axis=0, keepdims=True), acc0 + d0, acc1 + d1

        _, _, acc0, acc1 = lax.while_loop(live, sweep, (jnp.int32(0), cs[u], accs[u][0], accs[u][1]))
        outs = []
        for n, o in enumerate((acc0, acc1)):
            rs = lax.rsqrt(jnp.mean(o * o, axis=0, keepdims=True) + EPS)
            outs.append(o * rs * gt[n * HEAD_DIM:(n + 1) * HEAD_DIM, :])
        o_ref[u * tq:(u + 1) * tq, :] = jnp.concatenate(outs, axis=0).T.astype(BF16)


def _causal_bias_c():
    key = jnp.arange(2 * K_TILE)[:, None]
    query = jnp.arange(2 * Q_TILE)[None, :] % Q_TILE
    return jnp.stack([jnp.where(key < query + off, 0.0, NEG) for off in (0, K_TILE)]).astype(F32)


def _attn_c(proj, vt, gn_t, bsz, seq, n_sub=4):
    nq = seq // (Q_TILE * n_sub)
    nk = seq // K_TILE
    pairs = C_HEADS // 2
    return pl.pallas_call(
        functools.partial(_attn_c_kernel, n_sub=n_sub),
        out_shape=jax.ShapeDtypeStruct((bsz * seq, pairs * LANES), BF16),
        grid=(bsz, pairs, nq),
        in_specs=[
            pl.BlockSpec((n_sub * Q_TILE, LANES), lambda b, h, i: (b * nq + i, C_Q_BLK + h)),
            pl.BlockSpec((seq, LANES), lambda b, h, i: (b, C_K_BLK + h)),
            pl.BlockSpec((nk, None, LANES, K_TILE), lambda b, h, i: (b, h, 0, 0)),
            pl.BlockSpec((2, 2 * K_TILE, 2 * Q_TILE), lambda b, h, i: (0, 0, 0)),
            pl.BlockSpec((LANES, Q_TILE), lambda b, h, i: (h, 0)),
        ],
        out_specs=pl.BlockSpec((n_sub * Q_TILE, LANES), lambda b, h, i: (b * nq + i, h)),
        compiler_params=_params(3),
        name="attn_c",
    )(proj, proj, vt, _causal_bias_c(), gn_t)


def _outmlp_kernel(x_ref, oa_ref, ob_ref, oc_ref, wo_ref, g1_ref, g2_ref, wu_ref, wd_ref, g3_ref, o_ref, acc_ref,
                   *, f_chunk, n_part):
    wa, wb = oa_ref.shape[1], ob_ref.shape[1]
    rows = x_ref.shape[0] // n_part
    parts = [slice(r * rows, (r + 1) * rows) for r in range(n_part)]
    d_ff = wu_ref.shape[1]
    ys = [_dot(oa_ref[p, :], wo_ref[0:wa, :].astype(BF16)) + _dot(ob_ref[p, :], wo_ref[wa:wa + wb, :].astype(BF16))
          + _dot(oc_ref[p, :], wo_ref[wa + wb:, :].astype(BF16)) for p in parts]
    x1s = [x_ref[p, :] + _rms(y, g1_ref[...]) for p, y in zip(parts, ys)]
    hs = [_rms(x1, g2_ref[...]).astype(BF16) for x1 in x1s]
    for p, x1, h in zip(parts, x1s, hs):
        for f0 in range(0, d_ff, f_chunk):
            u = jnp.maximum(_dot(h, wu_ref[:, f0:f0 + f_chunk]), 0.0)
            part = _dot((u * u).astype(BF16), wd_ref[f0:f0 + f_chunk, :])
            if f0 == 0:
                acc_ref[p, :] = part
            else:
                acc_ref[p, :] += part
        o_ref[p, :] = x1 + _rms(acc_ref[p, :], g3_ref[...])


def _outmlp(x, oa, ob, oc, w_out_all, w_up_all, w_down_all, layer, g_post, g_pre, g_post2,
            tm=1024, f_chunk=1024, n_part=4):
    m, d = x.shape
    d_ff = w_up_all.shape[2]
    row = lambda w: pl.BlockSpec((tm, w), lambda i: (i, 0))
    full = lambda a, b: pl.BlockSpec((a, b), lambda i: (0, 0), pipeline_mode=pl.Buffered(1))
    of_layer = lambda a, b: pl.BlockSpec((None, a, b), lambda i: (layer, 0, 0), pipeline_mode=pl.Buffered(1))
    return pl.pallas_call(
        functools.partial(_outmlp_kernel, f_chunk=f_chunk, n_part=n_part),
        out_shape=jax.ShapeDtypeStruct((m, d), F32),
        grid=(m // tm,),
        in_specs=[
            row(d), row(oa.shape[1]), row(ob.shape[1]), row(oc.shape[1]),
            of_layer(d, d), full(1, d), full(1, d), of_layer(d, d_ff), of_layer(d_ff, d), full(1, d),
        ],
        out_specs=row(d),
        scratch_shapes=[pltpu.VMEM((tm, d), F32)],
        compiler_params=_params(1),
        name="outproj_mlp",
    )(x, oa, ob, oc, w_out_all, g_post, g_pre, w_up_all, w_down_all, g_post2)


def _rope_tables(seq):
    pos = jnp.arange(seq, dtype=F32)
    inv_freq = ROPE_THETA ** (-jnp.arange(0, ROPE_DIM, 2, dtype=F32) / ROPE_DIM)
    ang = pos[:, None] * inv_freq[None, :]
    cos, sin = jnp.cos(ang), jnp.sin(ang)
    half = ROPE_DIM // 2
    d = jnp.arange(LANES) % HEAD_DIM
    f = d % half
    rc = jnp.where(d[None, :] < ROPE_DIM, cos[:, f], 1.0)
    rs1 = jnp.where(d[None, :] < half, -sin[:, f], 0.0)
    rs2 = jnp.where((d[None, :] >= half) & (d[None, :] < ROPE_DIM), sin[:, f], 0.0)
    return rc.astype(F32), rs1.astype(F32), rs2.astype(F32)


def _q_scale_columns():
    blk = jnp.arange(D_IN) // LANES
    is_q = (blk < A_K_BLK) | ((blk >= B_Q_BLK) & (blk < B_K_BLK)) | ((blk >= C_Q_BLK) & (blk < C_K_BLK))
    return jnp.where(is_q, HEAD_DIM ** -0.5 * math.log2(math.e), 1.0).astype(F32)


def _gain_t(g, n_blk, width):
    return jnp.broadcast_to(g.reshape(n_blk * LANES, 1), (n_blk * LANES, width)).astype(F32)


def kernel(x, norm_pre_mix, w_in, lam_q1, lam_k1, lam_q2, lam_k2, subln_a, rel_bias, gn_b, gn_c, w_out,
           norm_post_mix, norm_pre_mlp, w_up, w_down, norm_post_mlp):
    bsz, seq, d = x.shape
    depth = w_in.shape[0]
    xf = x.reshape(bsz * seq, d)
    rc, rs1, rs2 = _rope_tables(seq)
    q_scale = _q_scale_columns()
    w_up_bf, w_down_bf = w_up.astype(BF16), w_down.astype(BF16)
    row = lambda v: v.reshape(1, -1)
    for l in range(depth):
        lam_init = 0.8 - 0.6 * math.exp(-0.3 * l)
        proj, vt_a, vt_b, vt_c = _inproj(xf, row(norm_pre_mix[l]), w_in, l, row(q_scale), rc, rs1, rs2, seq)
        lam_vecs = jnp.stack([lam_q1[l], lam_k1[l], lam_q2[l], lam_k2[l]]).astype(F32)
        oa = _attn_a(proj, vt_a, lam_vecs, _gain_t(subln_a[l], 1, A_TILE), lam_init, bsz, seq)
        ob = _attn_b(proj, vt_b, _bias_tiles(rel_bias[l]), _gain_t(gn_b[l], B_HEADS // 2, Q_TILE), bsz, seq)
        oc = _attn_c(proj, vt_c, _gain_t(gn_c[l], C_HEADS // 2, Q_TILE), bsz, seq)
        xf = _outmlp(xf, oa, ob, oc, w_out, w_up_bf, w_down_bf, l, row(norm_post_mix[l]), row(norm_pre_mlp[l]),
                     row(norm_post_mlp[l]))
    return xf.reshape(bsz, seq, d)
```

```python
import functools
import math

import jax
import jax.numpy as jnp
from jax import lax
from jax.experimental import pallas as pl
from jax.experimental.pallas import tpu as pltpu

F32 = jnp.float32
BF16 = jnp.bfloat16

EPS = 1e-6
ROPE_THETA = 500000.0
HEAD_DIM = 64
ROPE_DIM = HEAD_DIM // 4
CHUNK = 64
LANES = 128
SUBLANES = 8
A_HEADS = 4
B_HEADS = 4
C_HEADS = 4
B_LEFT_CHUNKS = 8
REL_CLIP = 128
NEG = -1e30
M_RESET = -5e29
SP_SWITCH = 64.0
C_EXIT = -160.0

A_Q_BLK, A_K_BLK, A_V_BLK = 0, 4, 8
B_Q_BLK, B_K_BLK, B_V_BLK = 12, 14, 16
C_Q_BLK, C_K_BLK, C_V_BLK = 18, 20, 22
D_IN = 24 * LANES

Q_TILE = 256
K_TILE = 256
A_TILE = 512
A_UNROLL = 4
B_UNROLL = 8
V_ROWS = LANES + 16
B_KEYS = 3 * K_TILE
B_TABLE_W = B_KEYS + 2 * K_TILE
B_V_ROWS = HEAD_DIM + 16

VMEM_LIMIT = 56 * 1024 * 1024


def _params(n_axes, vmem=VMEM_LIMIT):
    return pltpu.CompilerParams(dimension_semantics=("arbitrary",) * n_axes, vmem_limit_bytes=vmem)


def _rms(x, g):
    return x * lax.rsqrt(jnp.mean(x * x, axis=-1, keepdims=True) + EPS) * g


def _dot_t(a, b):
    return lax.dot_general(a, b, (((1,), (1,)), ((), ())), preferred_element_type=F32)


def _dot(a, b):
    return jnp.dot(a, b, preferred_element_type=F32)


def _split_heads(q):
    lane = lax.broadcasted_iota(jnp.int32, q.shape, 1)
    zero = jnp.zeros_like(q)
    return jnp.where(lane < HEAD_DIM, q, zero), jnp.where(lane >= HEAD_DIM, q, zero)


def _inproj_kernel(x_ref, g_ref, w_ref, qs_ref, rc_ref, rs1_ref, rs2_ref, o_ref, vta_ref, vtb_ref, vtc_ref, *, n_chunk):
    h = _rms(x_ref[...], g_ref[...]).astype(BF16)
    rc, rs1, rs2 = rc_ref[...], rs1_ref[...], rs2_ref[...]
    tm = x_ref.shape[0]
    pad_rows = V_ROWS - LANES
    ones_rows = jnp.where(lax.broadcasted_iota(jnp.int32, (pad_rows, A_TILE), 0) == 0, 1.0, 0.0).astype(BF16)
    ones_rows_k = jnp.where(lax.broadcasted_iota(jnp.int32, (pad_rows, K_TILE), 0) == 0, 1.0, 0.0).astype(BF16)
    kt = tm // K_TILE
    for n0 in range(0, D_IN, n_chunk):
        y = _dot(h, w_ref[:, n0:n0 + n_chunk].astype(BF16))
        for j in range(n_chunk // LANES):
            blk = n0 // LANES + j
            yj = y[:, j * LANES:(j + 1) * LANES]
            if blk < A_K_BLK or B_Q_BLK <= blk < B_K_BLK or C_Q_BLK <= blk < C_K_BLK:
                yj = yj * qs_ref[:, blk * LANES:(blk + 1) * LANES]
            if blk < A_V_BLK:
                yj = yj * rc + pltpu.roll(yj, LANES - ROPE_DIM // 2, 1) * rs1 + pltpu.roll(yj, ROPE_DIM // 2, 1) * rs2
            o_ref[:, n0 + j * LANES:n0 + (j + 1) * LANES] = yj.astype(BF16)
            if A_V_BLK <= blk < B_Q_BLK:
                yt = yj.T.astype(BF16)
                for t in range(tm // A_TILE):
                    vta_ref[t, blk - A_V_BLK, 0:LANES, :] = yt[:, t * A_TILE:(t + 1) * A_TILE]
                    vta_ref[t, blk - A_V_BLK, LANES:V_ROWS, :] = ones_rows
            if B_V_BLK <= blk < C_Q_BLK:
                yt = yj.T.astype(BF16)
                for t in range(kt):
                    for n in range(2):
                        r0 = n * B_V_ROWS
                        vtb_ref[t, blk - B_V_BLK, r0:r0 + HEAD_DIM, :] = (
                            yt[n * HEAD_DIM:(n + 1) * HEAD_DIM, t * K_TILE:(t + 1) * K_TILE])
                        vtb_ref[t, blk - B_V_BLK, r0 + HEAD_DIM:r0 + B_V_ROWS, :] = ones_rows_k
            if blk >= C_V_BLK:
                yt = yj.T.astype(BF16)
                for t in range(kt):
                    vtc_ref[t, blk - C_V_BLK] = yt[:, t * K_TILE:(t + 1) * K_TILE]


def _inproj(x, g, w_all, layer, q_scale, rc, rs1, rs2, seq, tm=2 * A_TILE, n_chunk=512):
    m, d = x.shape
    pos_blocks = seq // tm
    rope_spec = pl.BlockSpec((tm, LANES), lambda i: (i % pos_blocks, 0))
    kt = tm // K_TILE
    return pl.pallas_call(
        functools.partial(_inproj_kernel, n_chunk=n_chunk),
        out_shape=(jax.ShapeDtypeStruct((m, D_IN), BF16),
                   jax.ShapeDtypeStruct((m // A_TILE, A_HEADS, V_ROWS, A_TILE), BF16),
                   jax.ShapeDtypeStruct((m // K_TILE, B_HEADS // 2, 2 * B_V_ROWS, K_TILE), BF16),
                   jax.ShapeDtypeStruct((m // K_TILE, C_HEADS // 2, LANES, K_TILE), BF16)),
        grid=(m // tm,),
        in_specs=[
            pl.BlockSpec((tm, d), lambda i: (i, 0)),
            pl.BlockSpec((1, d), lambda i: (0, 0)),
            pl.BlockSpec((None, d, D_IN), lambda i: (layer, 0, 0), pipeline_mode=pl.Buffered(1)),
            pl.BlockSpec((1, D_IN), lambda i: (0, 0)),
            rope_spec, rope_spec, rope_spec,
        ],
        out_specs=(pl.BlockSpec((tm, D_IN), lambda i: (i, 0)),
                   pl.BlockSpec((tm // A_TILE, A_HEADS, V_ROWS, A_TILE), lambda i: (i, 0, 0, 0)),
                   pl.BlockSpec((kt, B_HEADS // 2, 2 * B_V_ROWS, K_TILE), lambda i: (i, 0, 0, 0)),
                   pl.BlockSpec((kt, C_HEADS // 2, LANES, K_TILE), lambda i: (i, 0, 0, 0))),
        compiler_params=_params(1),
        name="inproj",
    )(x, g, w_all, q_scale, rc, rs1, rs2)


def _attn_a_kernel(q_ref, k_ref, vt_ref, lam_ref, gt_ref, o_ref, qx_ref, m_ref, mx_ref, s_ref, acc_ref,
                   *, lam_init, nq):
    tq = A_TILE
    n_chunks = tq // CHUNK
    fr = lax.broadcasted_iota(jnp.int32, (2 * tq, LANES), 0)
    fc = lax.broadcasted_iota(jnp.int32, (2 * tq, LANES), 1)
    q_feat = jnp.where((fc < n_chunks) & ((fr % tq) // CHUNK == fc), 1.0, 0.0).astype(BF16)
    q_feat_t = q_feat.astype(F32).T.astype(BF16)
    for blk in range(nq):
        q0, q1 = _split_heads(q_ref[blk * tq:(blk + 1) * tq, :])
        qx_ref[blk, 0:LANES, 0:tq] = q0.astype(F32).T.astype(BF16)
        qx_ref[blk, 0:LANES, tq:2 * tq] = q1.astype(F32).T.astype(BF16)
        qx_ref[blk, LANES:2 * LANES, :] = q_feat_t
    kr = lax.broadcasted_iota(jnp.int32, (tq, LANES), 0)
    kc_ = lax.broadcasted_iota(jnp.int32, (tq, LANES), 1)
    k_feat = jnp.where((kc_ < n_chunks) & (kr // CHUNK > kc_), NEG, 0.0).astype(BF16)
    k_nofeat = jnp.zeros_like(k_feat)

    m_ref[...] = jnp.full(m_ref.shape, M_RESET, F32)

    @pl.when((pl.program_id(0) == 0) & (pl.program_id(1) == 0))
    def _():
        acc_ref[...] = jnp.zeros(acc_ref.shape, F32)

    def score_stage(qa, ka, slot):
        k = k_ref[pl.ds(pl.multiple_of(ka * tq, tq), tq), :]
        kx = jnp.concatenate([k, jnp.where(ka == qa, k_feat, k_nofeat)], axis=1)
        s_new = _dot(kx, qx_ref[qa])
        s_ref[slot] = s_new
        mx_ref[slot, 0:1, :] = jnp.max(s_new, axis=0, keepdims=True)

    def value_stage(qb, kb, slot):
        m_old = jnp.where(kb == 0, M_RESET, m_ref[0:1, :])
        m_new = jnp.maximum(m_old, mx_ref[slot, 0:1, :])
        m_ref[0:1, :] = m_new
        p = jnp.exp2(s_ref[slot] - m_new).astype(BF16)
        acc_ref[qb] = jnp.exp2(m_old - m_new) * acc_ref[qb] + _dot(vt_ref[kb], p)

    def advance(st):
        qa, ka, _, _ = st
        last = ka == qa
        return (jnp.where(last, qa + 1, qa), jnp.where(last, 0, ka + 1), qa, ka)

    def step(st, slot):
        qa, ka, qb, kb = st
        score_stage(qa, ka, slot)
        value_stage(qb, kb, 1 - slot)
        return advance(st)

    def steps(i, st):
        for u in range(A_UNROLL):
            st = step(st, (u + 1) % 2)
        return st

    n_tiles = nq * (nq + 1) // 2
    zero = jnp.int32(0)
    score_stage(zero, zero, 0)
    st = advance((zero, zero, zero, zero))
    n_loop = (n_tiles - 1) // A_UNROLL
    st = lax.fori_loop(0, n_loop, steps, st)
    for f in range(1 + n_loop * A_UNROLL, n_tiles):
        st = step(st, f % 2)
    value_stage(st[2], st[3], (n_tiles - 1) % 2)

    lv = lam_ref[...]
    lam = (jnp.exp(jnp.sum(lv[0:1] * lv[1:2], axis=-1, keepdims=True))
           - jnp.exp(jnp.sum(lv[2:3] * lv[3:4], axis=-1, keepdims=True)) + lam_init)
    for blk in range(nq):
        acc = acc_ref[blk]
        on = acc[0:LANES] * (1.0 / acc[LANES:LANES + 1])
        o = on[:, :tq] - lam * on[:, tq:]
        o = o * lax.rsqrt(jnp.mean(o * o, axis=0, keepdims=True) + EPS) * gt_ref[...] * (1.0 - lam_init)
        o_ref[blk * tq:(blk + 1) * tq, :] = o.T.astype(BF16)


def _attn_a(proj, vt, lam_vecs, subln_gt, lam_init, bsz, seq):
    nq = seq // A_TILE
    return pl.pallas_call(
        functools.partial(_attn_a_kernel, lam_init=lam_init, nq=nq),
        out_shape=jax.ShapeDtypeStruct((bsz * seq, A_HEADS * LANES), BF16),
        grid=(bsz, A_HEADS),
        in_specs=[
            pl.BlockSpec((seq, LANES), lambda b, h: (b, A_Q_BLK + h)),
            pl.BlockSpec((seq, LANES), lambda b, h: (b, A_K_BLK + h)),
            pl.BlockSpec((nq, None, V_ROWS, A_TILE), lambda b, h: (b, h, 0, 0)),
            pl.BlockSpec((4, HEAD_DIM), lambda b, h: (0, 0)),
            pl.BlockSpec((LANES, A_TILE), lambda b, h: (0, 0)),
        ],
        out_specs=pl.BlockSpec((seq, LANES), lambda b, h: (b, h)),
        scratch_shapes=[pltpu.VMEM((nq, 2 * LANES, 2 * A_TILE), BF16), pltpu.VMEM((SUBLANES, 2 * A_TILE), F32),
                        pltpu.VMEM((2, SUBLANES, 2 * A_TILE), F32),
                        pltpu.VMEM((2, A_TILE, 2 * A_TILE), F32),
                        pltpu.VMEM((nq, V_ROWS, 2 * A_TILE), F32)],
        compiler_params=_params(2),
        name="attn_a",
    )(proj, proj, vt, lam_vecs, subln_gt)


def _bias_kernel(base_ref, o_ref):
    win = 3 * LANES
    base = base_ref[...] * math.log2(math.e)
    spread = jnp.broadcast_to(base, (CHUNK, win + LANES))
    ws = tuple(pltpu.roll(spread, shift, 1, stride=1, stride_axis=0)[:, 0:win] for shift in (0, CHUNK))
    far = base[:, win:win + 1]

    shape = (Q_TILE, B_TABLE_W)
    row = lax.broadcasted_iota(jnp.int32, shape, 0)
    col = lax.broadcasted_iota(jnp.int32, shape, 1)
    kc = col // CHUNK - row // CHUNK
    band = (kc >= 0) & (kc <= B_LEFT_CHUNKS)

    win_start = (3 * LANES, 3 * LANES, 4 * LANES, 4 * LANES)
    rows = []
    for c in range(Q_TILE // CHUNK):
        pieces = []
        for j in range(B_TABLE_W // LANES):
            lo = j * LANES
            bnd = band[c * CHUNK:(c + 1) * CHUNK, lo:lo + LANES]
            if win_start[c] <= lo < win_start[c] + win:
                off = lo - win_start[c]
                pieces.append(jnp.where(bnd, ws[c % 2][:, off:off + LANES], NEG))
            else:
                pieces.append(jnp.where(bnd, far, NEG))
        rows.append(jnp.concatenate(pieces, axis=1))
    table = jnp.concatenate(rows, axis=0)
    for v in range(3):
        c0 = (2 - v) * K_TILE
        o_ref[v] = table[:, c0:c0 + B_KEYS].T


def _bias_tiles(rel_bias):
    heads = rel_bias.shape[0]
    win = 3 * LANES
    base = jnp.concatenate([rel_bias[:, ::-1],
                            jnp.broadcast_to(rel_bias[:, :1], (heads, win - 2 * REL_CLIP - 1)),
                            jnp.broadcast_to(rel_bias[:, -1:], (heads, LANES))], axis=1).astype(F32)
    return pl.pallas_call(
        _bias_kernel,
        out_shape=jax.ShapeDtypeStruct((3, heads, B_KEYS, Q_TILE), F32),
        grid=(heads,),
        in_specs=[pl.BlockSpec((None, 1, win + LANES), lambda h: (h, 0, 0))],
        out_specs=pl.BlockSpec((3, None, B_KEYS, Q_TILE), lambda h: (0, h, 0, 0)),
        compiler_params=_params(1),
        name="bias_tiles",
    )(base.reshape(heads, 1, win + LANES))


def _attn_b_kernel(q_ref, k_ref, vt_ref, bias_ref, gt_ref, o_ref, s_ref, mx_ref, res_ref, *, nq):
    tq = Q_TILE
    s_ref[1] = jnp.zeros(s_ref.shape[1:], F32)
    mx_ref[...] = jnp.zeros(mx_ref.shape, F32)
    gt = gt_ref[...]

    def score_stage(i, slot):
        first_key = pl.multiple_of(jnp.maximum(i - 2, 0) * K_TILE, K_TILE)
        qcat = jnp.concatenate(_split_heads(q_ref[pl.ds(pl.multiple_of(i * tq, tq), tq), :]), axis=0)
        var = jnp.minimum(i, 2)
        s = (_dot_t(k_ref[pl.ds(first_key, B_KEYS), :], qcat)
             + jnp.concatenate([bias_ref[var, 0], bias_ref[var, 1]], axis=1))
        s_ref[slot] = s
        mx_ref[slot, 0:1, :] = jnp.max(s, axis=0, keepdims=True)

    def value_stage(i, slot):
        jb = jnp.maximum(i - 2, 0)
        p = jnp.exp2(s_ref[slot] - mx_ref[slot, 0:1, :]).astype(BF16)
        outs = []
        for n in range(2):
            acc = sum(_dot(vt_ref[jb + t][n * B_V_ROWS:(n + 1) * B_V_ROWS, :],
                           p[t * K_TILE:(t + 1) * K_TILE, n * tq:(n + 1) * tq]) for t in range(B_KEYS // K_TILE))
            o = acc[0:HEAD_DIM] * (1.0 / acc[HEAD_DIM:HEAD_DIM + 1])
            rs = lax.rsqrt(jnp.mean(o * o, axis=0, keepdims=True) + EPS)
            outs.append(o * rs * gt[n * HEAD_DIM:(n + 1) * HEAD_DIM, :])
        res_ref[jnp.where(i < 0, nq, i)] = jnp.concatenate(outs, axis=0).T.astype(BF16)

    def step(i, slot):
        score_stage(i, slot)
        value_stage(i - 1, 1 - slot)

    def steps(it, carry):
        for u in range(B_UNROLL):
            step(B_UNROLL * it + u, u % 2)
        return carry

    lax.fori_loop(0, nq // B_UNROLL, steps, 0)
    for i in range(nq // B_UNROLL * B_UNROLL, nq):
        step(i, i % 2)
    value_stage(nq - 1, (nq - 1) % 2)
    for blk in range(nq):
        o_ref[blk * tq:(blk + 1) * tq, :] = res_ref[blk]


def _attn_b(proj, vt, bias_tiles, gn_t, bsz, seq):
    nq = seq // Q_TILE
    nk = seq // K_TILE
    pairs = B_HEADS // 2
    return pl.pallas_call(
        functools.partial(_attn_b_kernel, nq=nq),
        out_shape=jax.ShapeDtypeStruct((bsz * seq, pairs * LANES), BF16),
        grid=(bsz, pairs),
        in_specs=[
            pl.BlockSpec((seq, LANES), lambda b, h: (b, B_Q_BLK + h)),
            pl.BlockSpec((seq, LANES), lambda b, h: (b, B_K_BLK + h)),
            pl.BlockSpec((nk, None, 2 * B_V_ROWS, K_TILE), lambda b, h: (b, h, 0, 0)),
            pl.BlockSpec((3, 2, B_KEYS, Q_TILE), lambda b, h: (0, h, 0, 0)),
            pl.BlockSpec((LANES, Q_TILE), lambda b, h: (h, 0)),
        ],
        out_specs=pl.BlockSpec((seq, LANES), lambda b, h: (b, h)),
        scratch_shapes=[pltpu.VMEM((2, B_KEYS, 2 * Q_TILE), F32), pltpu.VMEM((2, SUBLANES, 2 * Q_TILE), F32),
                        pltpu.VMEM((nq + 1, Q_TILE, LANES), BF16)],
        compiler_params=_params(2),
        name="attn_b",
    )(proj, proj, vt, bias_tiles, gn_t)


def _softplus2(z):
    return jnp.maximum(z, jnp.log2(1.0 + jnp.exp2(jnp.minimum(z, SP_SWITCH))))


def _attn_c_kernel(q_ref, k_ref, vt_ref, mask_ref, gt_ref, o_ref, *, n_sub):
    tq, tk = Q_TILE, K_TILE
    tri_r = lax.broadcasted_iota(jnp.int32, (tk, tk), 0)
    tri_c = lax.broadcasted_iota(jnp.int32, (tk, tk), 1)
    later = jnp.where(tri_c > tri_r, 1.0, 0.0).astype(BF16)

    def suffix_sums(x):
        return _dot(later, x.astype(BF16))

    def values(vt, a):
        return tuple(_dot(vt[n * HEAD_DIM:(n + 1) * HEAD_DIM, :], a[:, n * tq:(n + 1) * tq]) for n in range(2))

    subs = range(n_sub)
    qis = [pl.program_id(2) * n_sub + u for u in subs]
    qcats = [jnp.concatenate(_split_heads(q_ref[u * tq:(u + 1) * tq, :]), axis=0) for u in subs]

    j0s = [jnp.maximum(qi - 1, 0) for qi in qis]
    zs = [_dot_t(k_ref[pl.ds(pl.multiple_of(j0 * tk, tk), 2 * tk), :], qcat) + mask_ref[jnp.minimum(qi, 1)]
          for j0, qi, qcat in zip(j0s, qis, qcats)]
    sps = [_softplus2(z) for z in zs]
    gone = [jnp.concatenate([suffix_sums(sp[:tk]) + jnp.sum(sp[tk:], axis=0, keepdims=True),
                             suffix_sums(sp[tk:])], axis=0) for sp in sps]
    a_s = [jnp.exp2((z - sp) - g).astype(BF16) for z, sp, g in zip(zs, sps, gone)]
    cs = [jnp.sum(sp, axis=0, keepdims=True) for sp in sps]
    accs = [tuple(x + y for x, y in zip(values(vt_ref[j0], a[:tk]), values(vt_ref[j0 + 1], a[tk:])))
            for j0, a in zip(j0s, a_s)]

    gt = gt_ref[...]
    for u in subs:
        j0, qcat = j0s[u], qcats[u]

        def live(state):
            t, c, _, _ = state
            return (t < j0) & (jnp.min(c) < -C_EXIT)

        def sweep(state):
            t, c, acc0, acc1 = state
            kj = j0 - 1 - t
            z = _dot_t(k_ref[pl.ds(pl.multiple_of(kj * tk, tk), tk), :], qcat)
            sp = _softplus2(z)
            a = jnp.exp2((z - sp) - suffix_sums(sp) - c).astype(BF16)
            d0, d1 = values(vt_ref[kj], a)
            return t + 1, c + jnp.sum(sp, axis=0, keepdims=True), acc0 + d0, acc1 + d1

        _, _, acc0, acc1 = lax.while_loop(live, sweep, (jnp.int32(0), cs[u], accs[u][0], accs[u][1]))
        outs = []
        for n, o in enumerate((acc0, acc1)):
            rs = lax.rsqrt(jnp.mean(o * o, axis=0, keepdims=True) + EPS)
            outs.append(o * rs * gt[n * HEAD_DIM:(n + 1) * HEAD_DIM, :])
        o_ref[u * tq:(u + 1) * tq, :] = jnp.concatenate(outs, axis=0).T.astype(BF16)


def _causal_bias_c():
    key = jnp.arange(2 * K_TILE)[:, None]
    query = jnp.arange(2 * Q_TILE)[None, :] % Q_TILE
    return jnp.stack([jnp.where(key < query + off, 0.0, NEG) for off in (0, K_TILE)]).astype(F32)


def _attn_c(proj, vt, gn_t, bsz, seq, n_sub=8):
    nq = seq // (Q_TILE * n_sub)
    nk = seq // K_TILE
    pairs = C_HEADS // 2
    return pl.pallas_call(
        functools.partial(_attn_c_kernel, n_sub=n_sub),
        out_shape=jax.ShapeDtypeStruct((bsz * seq, pairs * LANES), BF16),
        grid=(bsz, pairs, nq),
        in_specs=[
            pl.BlockSpec((n_sub * Q_TILE, LANES), lambda b, h, i: (b * nq + i, C_Q_BLK + h)),
            pl.BlockSpec((seq, LANES), lambda b, h, i: (b, C_K_BLK + h)),
            pl.BlockSpec((nk, None, LANES, K_TILE), lambda b, h, i: (b, h, 0, 0)),
            pl.BlockSpec((2, 2 * K_TILE, 2 * Q_TILE), lambda b, h, i: (0, 0, 0)),
            pl.BlockSpec((LANES, Q_TILE), lambda b, h, i: (h, 0)),
        ],
        out_specs=pl.BlockSpec((n_sub * Q_TILE, LANES), lambda b, h, i: (b * nq + i, h)),
        compiler_params=_params(3),
        name="attn_c",
    )(proj, proj, vt, _causal_bias_c(), gn_t)


def _outmlp_kernel(x_ref, oa_ref, ob_ref, oc_ref, wo_ref, g1_ref, g2_ref, wu_ref, wd_ref, g3_ref, o_ref, acc_ref,
                   *, f_chunk, n_part):
    wa, wb = oa_ref.shape[1], ob_ref.shape[1]
    rows = x_ref.shape[0] // n_part
    parts = [slice(r * rows, (r + 1) * rows) for r in range(n_part)]
    d_ff = wu_ref.shape[1]
    ys = [_dot(oa_ref[p, :], wo_ref[0:wa, :].astype(BF16)) + _dot(ob_ref[p, :], wo_ref[wa:wa + wb, :].astype(BF16))
          + _dot(oc_ref[p, :], wo_ref[wa + wb:, :].astype(BF16)) for p in parts]
    x1s = [x_ref[p, :] + _rms(y, g1_ref[...]) for p, y in zip(parts, ys)]
    hs = [_rms(x1, g2_ref[...]).astype(BF16) for x1 in x1s]
    for p, x1, h in zip(parts, x1s, hs):
        for f0 in range(0, d_ff, f_chunk):
            u = jnp.maximum(_dot(h, wu_ref[:, f0:f0 + f_chunk]), 0.0)
            part = _dot((u * u).astype(BF16), wd_ref[f0:f0 + f_chunk, :])
            if f0 == 0:
                acc_ref[p, :] = part
            else:
                acc_ref[p, :] += part
        o_ref[p, :] = x1 + _rms(acc_ref[p, :], g3_ref[...])


def _outmlp(x, oa, ob, oc, w_out_all, w_up_all, w_down_all, layer, g_post, g_pre, g_post2,
            tm=1024, f_chunk=1024, n_part=4):
    m, d = x.shape
    d_ff = w_up_all.shape[2]
    row = lambda w: pl.BlockSpec((tm, w), lambda i: (i, 0))
    full = lambda a, b: pl.BlockSpec((a, b), lambda i: (0, 0), pipeline_mode=pl.Buffered(1))
    of_layer = lambda a, b: pl.BlockSpec((None, a, b), lambda i: (layer, 0, 0), pipeline_mode=pl.Buffered(1))
    return pl.pallas_call(
        functools.partial(_outmlp_kernel, f_chunk=f_chunk, n_part=n_part),
        out_shape=jax.ShapeDtypeStruct((m, d), F32),
        grid=(m // tm,),
        in_specs=[
            row(d), row(oa.shape[1]), row(ob.shape[1]), row(oc.shape[1]),
            of_layer(d, d), full(1, d), full(1, d), of_layer(d, d_ff), of_layer(d_ff, d), full(1, d),
        ],
        out_specs=row(d),
        scratch_shapes=[pltpu.VMEM((tm, d), F32)],
        compiler_params=_params(1),
        name="outproj_mlp",
    )(x, oa, ob, oc, w_out_all, g_post, g_pre, w_up_all, w_down_all, g_post2)


def _rope_tables(seq):
    pos = jnp.arange(seq, dtype=F32)
    inv_freq = ROPE_THETA ** (-jnp.arange(0, ROPE_DIM, 2, dtype=F32) / ROPE_DIM)
    ang = pos[:, None] * inv_freq[None, :]
    cos, sin = jnp.cos(ang), jnp.sin(ang)
    half = ROPE_DIM // 2
    d = jnp.arange(LANES) % HEAD_DIM
    f = d % half
    rc = jnp.where(d[None, :] < ROPE_DIM, cos[:, f], 1.0)
    rs1 = jnp.where(d[None, :] < half, -sin[:, f], 0.0)
    rs2 = jnp.where((d[None, :] >= half) & (d[None, :] < ROPE_DIM), sin[:, f], 0.0)
    return rc.astype(F32), rs1.astype(F32), rs2.astype(F32)


def _q_scale_columns():
    blk = jnp.arange(D_IN) // LANES
    is_q = (blk < A_K_BLK) | ((blk >= B_Q_BLK) & (blk < B_K_BLK)) | ((blk >= C_Q_BLK) & (blk < C_K_BLK))
    return jnp.where(is_q, HEAD_DIM ** -0.5 * math.log2(math.e), 1.0).astype(F32)


def _gain_t(g, n_blk, width):
    return jnp.broadcast_to(g.reshape(n_blk * LANES, 1), (n_blk * LANES, width)).astype(F32)


def kernel(x, norm_pre_mix, w_in, lam_q1, lam_k1, lam_q2, lam_k2, subln_a, rel_bias, gn_b, gn_c, w_out,
           norm_post_mix, norm_pre_mlp, w_up, w_down, norm_post_mlp):
    bsz, seq, d = x.shape
    depth = w_in.shape[0]
    xf = x.reshape(bsz * seq, d)
    rc, rs1, rs2 = _rope_tables(seq)
    q_scale = _q_scale_columns()
    w_up_bf, w_down_bf = w_up.astype(BF16), w_down.astype(BF16)
    row = lambda v: v.reshape(1, -1)
    for l in range(depth):
        lam_init = 0.8 - 0.6 * math.exp(-0.3 * l)
        proj, vt_a, vt_b, vt_c = _inproj(xf, row(norm_pre_mix[l]), w_in, l, row(q_scale), rc, rs1, rs2, seq)
        lam_vecs = jnp.stack([lam_q1[l], lam_k1[l], lam_q2[l], lam_k2[l]]).astype(F32)
        oa = _attn_a(proj, vt_a, lam_vecs, _gain_t(subln_a[l], 1, A_TILE), lam_init, bsz, seq)
        ob = _attn_b(proj, vt_b, _bias_tiles(rel_bias[l]), _gain_t(gn_b[l], B_HEADS // 2, Q_TILE), bsz, seq)
        oc = _attn_c(proj, vt_c, _gain_t(gn_c[l], C_HEADS // 2, Q_TILE), bsz, seq)
        xf = _outmlp(xf, oa, ob, oc, w_out, w_up_bf, w_down_bf, l, row(norm_post_mix[l]), row(norm_pre_mlp[l]),
                     row(norm_post_mlp[l]))
    return xf.reshape(bsz, seq, d)
```
